```python
import functools
import jax
import jax.numpy as jnp
from jax import lax
import numpy as np

D_MODEL = 4096
BATCH = 4
SEQ = 2048
DEPTH = 1
DEC_BATCH = 128
DEC_SEQ = 1
PAST_LEN = 2048
PAGE_SIZE = 128

N_MEM = 256
NORM_EPS = 1e-6
RWKV_DIM = D_MODEL // 2
RWKV_HEAD_DIM = 64
RWKV_HEADS = RWKV_DIM // RWKV_HEAD_DIM
LORA_DECAY = 128
LORA_AAA = 128
LORA_GATE = 480
RWKV_PROJ = 3 * RWKV_DIM + LORA_DECAY + LORA_AAA + LORA_GATE
RWKV_LN_EPS = 64e-5
ATTN_HEADS = 16
HEAD_DIM = 128
KV_HEADS = 4
KV_GROUP = ATTN_HEADS // KV_HEADS
ATTN_DIM = ATTN_HEADS * HEAD_DIM
KV_DIM = KV_HEADS * HEAD_DIM
INDEX_HEADS = 32
INDEX_DIM = 128
TOPK_MAX = 256
Q_BLOCK = 128
ROPE_THETA = 10000.0
IDX_LN_EPS = 1e-6
MEM_HEADS = 4
MEM_HEAD_DIM = 128
MEM_DIM = MEM_HEADS * MEM_HEAD_DIM
FFN_HIDDEN = ((8 * D_MODEL + 3 * 256 - 1) // (3 * 256)) * 256
IN_PROJ = 2 * D_MODEL + RWKV_PROJ + ATTN_DIM + 2 * KV_DIM + INDEX_HEADS * INDEX_DIM + INDEX_DIM + INDEX_HEADS

kernel_name = "hybrid_rwkv7_dsa_gated_decoder_step"


def _split(t, sizes):
    out, start = [], 0
    for s in sizes:
        out.append(t[..., start:start + s])
        start += s
    return out


def _rms_norm(x, g):
    xf = x.astype(jnp.float32)
    y = xf * lax.rsqrt(jnp.mean(xf * xf, axis=-1, keepdims=True) + NORM_EPS)
    return (y * g.astype(jnp.float32)).astype(x.dtype)


def _layer_norm(x, g, b, eps):
    xf = x.astype(jnp.float32)
    mu = jnp.mean(xf, axis=-1, keepdims=True)
    var = jnp.mean(jnp.square(xf - mu), axis=-1, keepdims=True)
    y = (xf - mu) * lax.rsqrt(var + eps)
    return (y * g.astype(jnp.float32) + b.astype(jnp.float32)).astype(x.dtype)


def _rope(x, pos):
    half = x.shape[-1] // 2
    inv = jnp.power(ROPE_THETA, -jnp.arange(half, dtype=jnp.float32) / half)
    ang = pos.astype(jnp.float32)[:, None] * inv[None, :]
    cos, sin = jnp.cos(ang)[:, None, :], jnp.sin(ang)[:, None, :]
    xf = x.astype(jnp.float32)
    x1, x2 = xf[..., :half], xf[..., half:]
    return jnp.concatenate([x1 * cos - x2 * sin, x2 * cos + x1 * sin], axis=-1).astype(x.dtype)


def _rwkv7_branch(p, prev_row, s0, lw):
    f = jnp.float32
    B, T, _ = p.shape
    shifted = jnp.concatenate([prev_row[:, None, :].astype(p.dtype), p[:, :-1]], axis=1)
    xm = p + (shifted - p) * lw["rwkv_mu"]
    xr, xw, xk, xv, xa, xg = _split(xm, [RWKV_DIM, LORA_DECAY, RWKV_DIM, RWKV_DIM, LORA_AAA, LORA_GATE])
    r, k, v = xr.astype(f), xk.astype(f), xv.astype(f)
    w_raw = -jax.nn.softplus(-(lw["rwkv_w0"] + jnp.tanh(xw) @ lw["rwkv_w2"]).astype(f)) - 0.5
    decay = jnp.exp(-jnp.exp(w_raw))
    a = jax.nn.sigmoid((lw["rwkv_a0"] + xa @ lw["rwkv_a2"]).astype(f))
    g = jax.nn.sigmoid(xg) @ lw["rwkv_g2"]
    heads = lambda t: t.reshape(B, T, RWKV_HEADS, RWKV_HEAD_DIM)
    kk = heads(k * lw["rwkv_kk"].astype(f))
    kk = kk / jnp.maximum(jnp.sqrt(jnp.sum(kk * kk, axis=-1, keepdims=True)), 1e-12)
    k = heads(k * (1.0 + (a - 1.0) * lw["rwkv_ka"].astype(f)))
    r, v, decay, a = heads(r), heads(v), heads(decay), heads(a)

    def step(S, inp):
        r_t, w_t, k_t, v_t, kk_t, a_t = inp
        sa = jnp.einsum("bhij,bhj->bhi", S, -kk_t)
        S = (S * w_t[:, :, None, :] + sa[..., None] * (kk_t * a_t)[:, :, None, :]
             + v_t[..., None] * k_t[:, :, None, :])
        return S, jnp.einsum("bhij,bhj->bhi", S, r_t)

    tm = lambda t: jnp.moveaxis(t, 1, 0)
    s_fin, y = lax.scan(step, s0.astype(f), (tm(r), tm(decay), tm(k), tm(v), tm(kk), tm(a)))
    y = jnp.moveaxis(y, 0, 1)
    mu = jnp.mean(y, axis=-1, keepdims=True)
    var = jnp.mean(jnp.square(y - mu), axis=-1, keepdims=True)
    y = ((y - mu) * lax.rsqrt(var + RWKV_LN_EPS)).reshape(B, T, RWKV_DIM)
    y = y * lw["rwkv_ln_g"].astype(f) + lw["rwkv_ln_b"].astype(f)
    bonus = (jnp.sum(r * k * lw["rwkv_rk"].astype(f), axis=-1, keepdims=True) * v).reshape(B, T, RWKV_DIM)
    out = (y + bonus).astype(p.dtype) * g
    return out, s_fin.astype(s0.dtype), p[:, -1]


def _dsa_select_attend(q, qi, wi, q_pos, ki_all, gather_kv, n_sel):
    f = jnp.float32
    B, Q = q.shape[:2]
    L = ki_all.shape[1]
    rel = jax.nn.relu(jnp.einsum("bqhd,bsd->bqhs", qi.astype(f), ki_all.astype(f)))
    score = jnp.einsum("bqh,bqhs->bqs", wi.astype(f), rel)
    admissible = jnp.arange(L, dtype=jnp.int32)[None, :] <= q_pos[:, None]
    score = jnp.where(admissible[None], score, -jnp.inf)
    _, sel = lax.top_k(score, n_sel)
    kg, vg = gather_kv(sel)
    valid = sel <= q_pos[None, :, None]
    qg = q.reshape(B, Q, KV_HEADS, KV_GROUP, HEAD_DIM).astype(f)
    logits = jnp.einsum("bqgrd,bqkgd->bqgrk", qg, kg.astype(f)) * (HEAD_DIM ** -0.5)
    logits = jnp.where(valid[:, :, None, None, :], logits, -jnp.inf)
    prob = jax.nn.softmax(logits, axis=-1)
    o = jnp.einsum("bqgrk,bqkgd->bqgrd", prob, vg.astype(f))
    return o.reshape(B, Q, ATTN_DIM).astype(q.dtype)


def _attend_prompt(q, qi, wi, k, v, ki, pos):
    B, S = q.shape[:2]
    nb = S // Q_BLOCK
    n_sel = min(TOPK_MAX, S // 4)
    take = jax.vmap(lambda rows, idx: rows[idx])

    def gather(sel):
        return take(k, sel), take(v, sel)

    def blocks(t):
        return jnp.moveaxis(t.reshape((B, nb, Q_BLOCK) + t.shape[2:]), 1, 0)

    def one_block(args):
        qb, qib, wib, pb = args
        return _dsa_select_attend(qb, qib, wib, pb, ki, gather, n_sel)

    out = lax.map(one_block, (blocks(q), blocks(qi), blocks(wi), pos.reshape(nb, Q_BLOCK)))
    return jnp.moveaxis(out, 0, 1).reshape(B, S, ATTN_DIM)


def _attend_sample(q, qi, wi, k, v, ki, pos, pool_k, pool_v, pool_idx_k, page_table):
    DB, T = q.shape[:2]
    past = page_table.shape[1] * PAGE_SIZE
    ki_past = pool_idx_k[page_table].reshape(DB, past, INDEX_DIM).astype(ki.dtype)
    ki_all = jnp.concatenate([ki_past, ki], axis=1)
    flat_k = pool_k.reshape(-1, KV_HEADS, HEAD_DIM)
    flat_v = pool_v.reshape(-1, KV_HEADS, HEAD_DIM)
    take = jax.vmap(lambda rows, idx: rows[idx])

    def gather(sel):
        in_past = sel < past
        sp = jnp.minimum(sel, past - 1)
        page = jnp.take_along_axis(page_table, (sp // PAGE_SIZE).reshape(DB, -1), axis=1).reshape(sel.shape)
        row = page * PAGE_SIZE + sp % PAGE_SIZE
        sn = jnp.clip(sel - past, 0, T - 1)
        m = in_past[..., None, None]
        kg = jnp.where(m, flat_k[row].astype(k.dtype), take(k, sn))
        vg = jnp.where(m, flat_v[row].astype(v.dtype), take(v, sn))
        return kg, vg

    n_sel = min(TOPK_MAX, (past + T) // 4)
    return _dsa_select_attend(q, qi, wi, pos, ki_all, gather, n_sel)


def _mixer_block(x, pos, shift_prev, s0, attend, lw):
    B, T = x.shape[:2]
    h = _rms_norm(x, lw["norm_mix"])
    proj = h @ lw["w_in"]
    gate_logits, p_rwkv, q, k, v, qi, ki, wi = _split(
        proj, [2 * D_MODEL, RWKV_PROJ, ATTN_DIM, KV_DIM, KV_DIM, INDEX_HEADS * INDEX_DIM, INDEX_DIM, INDEX_HEADS])
    y_r, s_fin, shift_new = _rwkv7_branch(p_rwkv, shift_prev, s0, lw)
    q = _rope(q.reshape(B, T, ATTN_HEADS, HEAD_DIM), pos)
    k = _rope(k.reshape(B, T, KV_HEADS, HEAD_DIM), pos)
    v = v.reshape(B, T, KV_HEADS, HEAD_DIM)
    qi = _rope(qi.reshape(B, T, INDEX_HEADS, INDEX_DIM), pos)
    ki = _rope(_layer_norm(ki, lw["idx_ln_g"], lw["idx_ln_b"], IDX_LN_EPS)[:, :, None, :], pos)[:, :, 0, :]
    wi = wi * (INDEX_HEADS ** -0.5 * INDEX_DIM ** -0.5)
    y_a = attend(q, qi, wi, k, v, ki)
    g_r, g_a = jnp.split(jax.nn.sigmoid(gate_logits), 2, axis=-1)
    merged = g_r * (y_r @ lw["w_br_rwkv"]) + g_a * (y_a @ lw["w_br_attn"])
    return x + merged @ lw["w_out"], k, v, ki, s_fin, shift_new


def _memory_kv(mem, lw):
    Bm = mem.shape[0]
    mn = _rms_norm(mem, lw["norm_mem"])
    mk = (mn @ lw["w_k_mem"]).reshape(Bm, -1, MEM_HEADS, MEM_HEAD_DIM)
    mv = (mn @ lw["w_v_mem"]).reshape(Bm, -1, MEM_HEADS, MEM_HEAD_DIM)
    return mk, mv


def _cross_attend(x, mem_k, mem_v, lw):
    B, T = x.shape[:2]
    hc = _rms_norm(x, lw["norm_cross"])
    q = (hc @ lw["w_q_mem"]).reshape(B, T, MEM_HEADS, MEM_HEAD_DIM).astype(jnp.float32)
    logits = jnp.einsum("bthd,bmhd->bhtm", q, mem_k.astype(jnp.float32)) * (MEM_HEAD_DIM ** -0.5)
    prob = jax.nn.softmax(logits, axis=-1)
    o = jnp.einsum("bhtm,bmhd->bthd", prob, mem_v.astype(jnp.float32)).reshape(B, T, MEM_DIM)
    return x + o.astype(x.dtype) @ lw["w_o_mem"]


def _ffn(x, lw):
    hf = _rms_norm(x, lw["norm_ffn"])
    gate, up = jnp.split(hf @ lw["w_up"], 2, axis=-1)
    return x + (jax.nn.silu(gate) * up) @ lw["w_down"]


def setup_inputs(seed: int = 0) -> dict:
    key = jax.random.key(seed)
    keys = jax.random.split(key, 48)
    counter = [0]
    f = jnp.float32

    def nk():
        counter[0] += 1
        return keys[counter[0] - 1]

    def nrm(shape, scale):
        return jax.random.normal(nk(), shape, f) * scale

    def gain(shape):
        return 1.0 + nrm(shape, 0.02)

    n_pages = PAST_LEN // PAGE_SIZE
    n_used = DEC_BATCH * n_pages
    n_pool = n_used + n_used // 4
    Ld = (DEPTH,)
    x_prompt = nrm((BATCH, SEQ, D_MODEL), 1.0)
    mem_prompt = nrm((BATCH, N_MEM, D_MODEL), 1.0)
    x_sample = nrm((DEC_BATCH, DEC_SEQ, D_MODEL), 1.0)
    cache_k = nrm(Ld + (n_pool, PAGE_SIZE, KV_HEADS, HEAD_DIM), 1.0)
    cache_v = nrm(Ld + (n_pool, PAGE_SIZE, KV_HEADS, HEAD_DIM), 1.0)
    cache_idx_k = nrm(Ld + (n_pool, PAGE_SIZE, INDEX_DIM), 1.0)
    cache_mem_k = nrm(Ld + (DEC_BATCH, N_MEM, MEM_HEADS, MEM_HEAD_DIM), 1.0)
    cache_mem_v = nrm(Ld + (DEC_BATCH, N_MEM, MEM_HEADS, MEM_HEAD_DIM), 1.0)
    state_rwkv = nrm(Ld + (DEC_BATCH, RWKV_HEADS, RWKV_HEAD_DIM, RWKV_HEAD_DIM), 0.5)
    state_shift = nrm(Ld + (DEC_BATCH, RWKV_PROJ), 1.0)
    page_table = jax.random.permutation(nk(), n_pool)[:n_used].reshape(DEC_BATCH, n_pages).astype(jnp.int32)
    return {
        "x_prompt": x_prompt, "mem_prompt": mem_prompt, "x_sample": x_sample,
        "cache_k": cache_k, "cache_v": cache_v, "cache_idx_k": cache_idx_k,
        "cache_mem_k": cache_mem_k, "cache_mem_v": cache_mem_v,
        "state_rwkv": state_rwkv, "state_shift": state_shift, "page_table": page_table,
        "norm_mix": gain(Ld + (D_MODEL,)),
        "w_in": nrm(Ld + (D_MODEL, IN_PROJ), D_MODEL ** -0.5),
        "rwkv_mu": jax.random.uniform(nk(), Ld + (RWKV_PROJ,), f),
        "rwkv_w0": jax.random.uniform(nk(), Ld + (RWKV_DIM,), f, -6.0, 1.0),
        "rwkv_w2": nrm(Ld + (LORA_DECAY, RWKV_DIM), 0.1 * LORA_DECAY ** -0.5),
        "rwkv_a0": nrm(Ld + (RWKV_DIM,), 0.1),
        "rwkv_a2": nrm(Ld + (LORA_AAA, RWKV_DIM), 0.1 * LORA_AAA ** -0.5),
        "rwkv_g2": nrm(Ld + (LORA_GATE, RWKV_DIM), LORA_GATE ** -0.5),
        "rwkv_kk": 0.85 + nrm(Ld + (RWKV_DIM,), 0.02),
        "rwkv_ka": gain(Ld + (RWKV_DIM,)),
        "rwkv_rk": nrm(Ld + (RWKV_HEADS, RWKV_HEAD_DIM), 0.1),
        "rwkv_ln_g": gain(Ld + (RWKV_DIM,)),
        "rwkv_ln_b": nrm(Ld + (RWKV_DIM,), 0.02),
        "idx_ln_g": gain(Ld + (INDEX_DIM,)),
        "idx_ln_b": nrm(Ld + (INDEX_DIM,), 0.02),
        "w_br_rwkv": nrm(Ld + (RWKV_DIM, D_MODEL), RWKV_DIM ** -0.5),
        "w_br_attn": nrm(Ld + (ATTN_DIM, D_MODEL), ATTN_DIM ** -0.5),
        "w_out": nrm(Ld + (D_MODEL, D_MODEL), D_MODEL ** -0.5),
        "norm_cross": gain(Ld + (D_MODEL,)),
        "norm_mem": gain(Ld + (D_MODEL,)),
        "w_q_mem": nrm(Ld + (D_MODEL, MEM_DIM), D_MODEL ** -0.5),
        "w_k_mem": nrm(Ld + (D_MODEL, MEM_DIM), D_MODEL ** -0.5),
        "w_v_mem": nrm(Ld + (D_MODEL, MEM_DIM), D_MODEL ** -0.5),
        "w_o_mem": nrm(Ld + (MEM_DIM, D_MODEL), MEM_DIM ** -0.5),
        "norm_ffn": gain(Ld + (D_MODEL,)),
        "w_up": nrm(Ld + (D_MODEL, 2 * FFN_HIDDEN), D_MODEL ** -0.5),
        "w_down": nrm(Ld + (FFN_HIDDEN, D_MODEL), FFN_HIDDEN ** -0.5),
        "final_norm": gain((D_MODEL,)),
    }


def reference(x_prompt, mem_prompt, x_sample, cache_k, cache_v, cache_idx_k, cache_mem_k, cache_mem_v,
              state_rwkv, state_shift, page_table, norm_mix, w_in, rwkv_mu, rwkv_w0, rwkv_w2, rwkv_a0,
              rwkv_a2, rwkv_g2, rwkv_kk, rwkv_ka, rwkv_rk, rwkv_ln_g, rwkv_ln_b, idx_ln_g, idx_ln_b,
              w_br_rwkv, w_br_attn, w_out, norm_cross, norm_mem, w_q_mem, w_k_mem, w_v_mem, w_o_mem,
              norm_ffn, w_up, w_down, final_norm):
    B, S = x_prompt.shape[:2]
    T = x_sample.shape[1]
    past = page_table.shape[1] * PAGE_SIZE
    pos_p = jnp.arange(S, dtype=jnp.int32)
    pos_s = past + jnp.arange(T, dtype=jnp.int32)
    xp, xs = x_prompt, x_sample
    k_p, v_p, ki_p, mk_p, mv_p, st_p, sh_p = [], [], [], [], [], [], []
    k_s, v_s, ki_s, st_s, sh_s = [], [], [], [], []
    for l in range(DEPTH):
        lw = {
            "norm_mix": norm_mix[l], "w_in": w_in[l], "rwkv_mu": rwkv_mu[l], "rwkv_w0": rwkv_w0[l],
            "rwkv_w2": rwkv_w2[l], "rwkv_a0": rwkv_a0[l], "rwkv_a2": rwkv_a2[l], "rwkv_g2": rwkv_g2[l],
            "rwkv_kk": rwkv_kk[l], "rwkv_ka": rwkv_ka[l], "rwkv_rk": rwkv_rk[l], "rwkv_ln_g": rwkv_ln_g[l],
            "rwkv_ln_b": rwkv_ln_b[l], "idx_ln_g": idx_ln_g[l], "idx_ln_b": idx_ln_b[l],
            "w_br_rwkv": w_br_rwkv[l], "w_br_attn": w_br_attn[l], "w_out": w_out[l],
            "norm_cross": norm_cross[l], "norm_mem": norm_mem[l], "w_q_mem": w_q_mem[l],
            "w_k_mem": w_k_mem[l], "w_v_mem": w_v_mem[l], "w_o_mem": w_o_mem[l],
            "norm_ffn": norm_ffn[l], "w_up": w_up[l], "w_down": w_down[l],
        }
        xp, kp, vp, kip, sfp, shp = _mixer_block(
            xp, pos_p, jnp.zeros((B, RWKV_PROJ), xp.dtype),
            jnp.zeros((B, RWKV_HEADS, RWKV_HEAD_DIM, RWKV_HEAD_DIM), xp.dtype),
            functools.partial(_attend_prompt, pos=pos_p), lw)
        mkp, mvp = _memory_kv(mem_prompt, lw)
        xp = _ffn(_cross_attend(xp, mkp, mvp, lw), lw)
        xs, ks_, vs_, kis, sfs, shs = _mixer_block(
            xs, pos_s, state_shift[l], state_rwkv[l],
            functools.partial(_attend_sample, pos=pos_s, pool_k=cache_k[l], pool_v=cache_v[l],
                              pool_idx_k=cache_idx_k[l], page_table=page_table), lw)
        xs = _ffn(_cross_attend(xs, cache_mem_k[l], cache_mem_v[l], lw), lw)
        k_p.append(kp); v_p.append(vp); ki_p.append(kip); mk_p.append(mkp); mv_p.append(mvp)
        st_p.append(sfp); sh_p.append(shp)
        k_s.append(ks_); v_s.append(vs_); ki_s.append(kis); st_s.append(sfs); sh_s.append(shs)
    y_prompt = _rms_norm(xp, final_norm)
    y_sample = _rms_norm(xs, final_norm)
    return (y_prompt, y_sample, jnp.stack(k_p), jnp.stack(v_p), jnp.stack(ki_p), jnp.stack(mk_p),
            jnp.stack(mv_p), jnp.stack(st_p), jnp.stack(sh_p), jnp.stack(k_s), jnp.stack(v_s),
            jnp.stack(ki_s), jnp.stack(st_s), jnp.stack(sh_s))
```

```python
import functools

import jax
import jax.numpy as jnp
from jax import lax
from jax.experimental import pallas as pl
from jax.experimental.pallas import tpu as pltpu

F32 = jnp.float32
BF16 = jnp.bfloat16
LANE = 128
SUBLANE = 8
VMEM_LIMIT = 56 * 1024 * 1024

RWKV_HEAD_DIM = 64
HEAD_DIM = 128
INDEX_DIM = 128
TOPK_MAX = 256
Q_BLOCK = 128
ROPE_THETA = 10000.0
NORM_EPS = 1e-6
RWKV_LN_EPS = 64e-5
IDX_LN_EPS = 1e-6
PREFIX_CHUNK = 256
NT_DIMS = (((1,), (1,)), ((), ()))


def _cparams(sem):
    return pltpu.CompilerParams(dimension_semantics=sem, vmem_limit_bytes=VMEM_LIMIT)


def _round_up(n, m):
    return (n + m - 1) // m * m


def _tile(n, pref, unit):
    best = None
    t = unit
    while t <= min(n, pref):
        if n % t == 0:
            best = t
        t += unit
    return best if best is not None else n


def _split_bf16(x):
    hi = x.astype(BF16)
    lo = (x - hi.astype(F32)).astype(BF16)
    return hi, lo


def _rmsnorm_kernel(x_ref, g_ref, o_ref):
    x = x_ref[...]
    ms = jnp.mean(x * x, axis=-1, keepdims=True)
    o_ref[...] = (x * lax.rsqrt(ms + NORM_EPS) * g_ref[...]).astype(o_ref.dtype)


def _rmsnorm(x, g, out_dtype):
    m, d = x.shape
    tm = _tile(m, 256, SUBLANE)
    return pl.pallas_call(
        _rmsnorm_kernel,
        grid=(m // tm,),
        in_specs=[pl.BlockSpec((tm, d), lambda i: (i, 0)), pl.BlockSpec((1, d), lambda i: (0, 0))],
        out_specs=pl.BlockSpec((tm, d), lambda i: (i, 0)),
        out_shape=jax.ShapeDtypeStruct((m, d), out_dtype),
        compiler_params=_cparams(("parallel",)),
        name="rmsnorm",
    )(x, g.reshape(1, d))


def _mm_kernel(x_ref, w_ref, o_ref):
    o_ref[...] = jnp.dot(x_ref[...], w_ref[...], preferred_element_type=F32).astype(o_ref.dtype)


def _mm(x, w, out_dtype=F32, tm_pref=1024, tn_pref=512):
    m, k = x.shape
    n = w.shape[1]
    tm = _tile(m, tm_pref, SUBLANE)
    tn = _tile(n, tn_pref, LANE)
    return pl.pallas_call(
        _mm_kernel,
        grid=(m // tm, n // tn),
        in_specs=[pl.BlockSpec((tm, k), lambda i, j: (i, 0)), pl.BlockSpec((k, tn), lambda i, j: (0, j))],
        out_specs=pl.BlockSpec((tm, tn), lambda i, j: (i, j)),
        out_shape=jax.ShapeDtypeStruct((m, n), out_dtype),
        compiler_params=_cparams(("parallel", "parallel")),
        name="matmul",
    )(x, w)


def _mm_resid_kernel(x_ref, w_ref, r_ref, o_ref, acc_ref, *, nk):
    kk = pl.program_id(2)

    @pl.when(kk == 0)
    def _():
        acc_ref[...] = r_ref[...]

    acc_ref[...] += jnp.dot(x_ref[...], w_ref[...], preferred_element_type=F32)

    @pl.when(kk == nk - 1)
    def _():
        o_ref[...] = acc_ref[...]


def _mm_resid(x, w, resid, tm_pref=512, tn_pref=1024, tk_pref=2048):
    m, k = x.shape
    n = w.shape[1]
    tm = _tile(m, tm_pref, SUBLANE)
    tn = _tile(n, tn_pref, LANE)
    tk = _tile(k, tk_pref, LANE)
    nk = k // tk
    return pl.pallas_call(
        functools.partial(_mm_resid_kernel, nk=nk),
        grid=(m // tm, n // tn, nk),
        in_specs=[pl.BlockSpec((tm, tk), lambda i, j, kk: (i, kk)),
                  pl.BlockSpec((tk, tn), lambda i, j, kk: (kk, j)),
                  pl.BlockSpec((tm, tn), lambda i, j, kk: (i, j))],
        out_specs=pl.BlockSpec((tm, tn), lambda i, j, kk: (i, j)),
        out_shape=jax.ShapeDtypeStruct((m, n), F32),
        scratch_shapes=[pltpu.VMEM((tm, tn), F32)],
        compiler_params=_cparams(("parallel", "parallel", "arbitrary")),
        name="matmul_resid",
    )(x, w, resid)


def _swiglu_kernel(x_ref, wg_ref, wu_ref, o_ref):
    x = x_ref[...]
    g = jnp.dot(x, wg_ref[...], preferred_element_type=F32)
    u = jnp.dot(x, wu_ref[...], preferred_element_type=F32)
    o_ref[...] = (g * jax.nn.sigmoid(g) * u).astype(o_ref.dtype)


def _swiglu(x, w_up, tm_pref=1024, tn_pref=256):
    m, k = x.shape
    f = w_up.shape[1] // 2
    tm = _tile(m, tm_pref, SUBLANE)
    tn = _tile(f, tn_pref, LANE)
    nb = f // tn
    return pl.pallas_call(
        _swiglu_kernel,
        grid=(m // tm, nb),
        in_specs=[pl.BlockSpec((tm, k), lambda i, j: (i, 0)),
                  pl.BlockSpec((k, tn), lambda i, j: (0, j)),
                  pl.BlockSpec((k, tn), lambda i, j: (0, j + nb))],
        out_specs=pl.BlockSpec((tm, tn), lambda i, j: (i, j)),
        out_shape=jax.ShapeDtypeStruct((m, f), BF16),
        compiler_params=_cparams(("parallel", "parallel")),
        name="swiglu",
    )(x, w_up, w_up)


def _merge_kernel(yr_ref, g_ref, ya_ref, gr_ref, ga_ref, wr_ref, wa_ref, o_ref):
    lr = (yr_ref[...] * g_ref[...]).astype(BF16)
    br = jnp.dot(lr, wr_ref[...], preferred_element_type=F32)
    ba = jnp.dot(ya_ref[...], wa_ref[...], preferred_element_type=F32)
    o_ref[...] = (jax.nn.sigmoid(gr_ref[...]) * br + jax.nn.sigmoid(ga_ref[...]) * ba).astype(o_ref.dtype)


def _merge(yr, g, ya, gates, w_br_r, w_br_a, tm_pref=512, tn_pref=512):
    m, rd = yr.shape
    ad = ya.shape[1]
    d = w_br_r.shape[1]
    tm = _tile(m, tm_pref, SUBLANE)
    tn = _tile(d, tn_pref, LANE)
    nb = d // tn
    return pl.pallas_call(
        _merge_kernel,
        grid=(m // tm, nb),
        in_specs=[pl.BlockSpec((tm, rd), lambda i, j: (i, 0)),
                  pl.BlockSpec((tm, rd), lambda i, j: (i, 0)),
                  pl.BlockSpec((tm, ad), lambda i, j: (i, 0)),
                  pl.BlockSpec((tm, tn), lambda i, j: (i, j)),
                  pl.BlockSpec((tm, tn), lambda i, j: (i, j + nb)),
                  pl.BlockSpec((rd, tn), lambda i, j: (0, j)),
                  pl.BlockSpec((ad, tn), lambda i, j: (0, j))],
        out_specs=pl.BlockSpec((tm, tn), lambda i, j: (i, j)),
        out_shape=jax.ShapeDtypeStruct((m, d), BF16),
        compiler_params=_cparams(("parallel", "parallel")),
        name="merge",
    )(yr, g, ya, gates, gates, w_br_r, w_br_a)


def _rwkv_prep_kernel(p_ref, sh_ref, mu_ref, w0_ref, w2_ref, a0_ref, a2_ref, g2_ref,
                      r_ref, w_ref, k_ref, v_ref, a_ref, g_ref, *, rd, lw, la, lgp):
    p = p_ref[...]
    xm = p + (sh_ref[...] - p) * mu_ref[...]
    o = 0
    xr = xm[:, o:o + rd]; o += rd
    xw = xm[:, o:o + lw]; o += lw
    xk = xm[:, o:o + rd]; o += rd
    xv = xm[:, o:o + rd]; o += rd
    xa = xm[:, o:o + la]; o += la
    xg = xm[:, o:o + lgp]
    z = w0_ref[...] + jnp.dot(jnp.tanh(xw).astype(BF16), w2_ref[...], preferred_element_type=F32)
    sp = jnp.maximum(-z, 0.0) + jnp.log(1.0 + jnp.exp(-jnp.abs(z)))
    w_raw = -sp - 0.5
    r_ref[...] = xr
    w_ref[...] = jnp.exp(-jnp.exp(w_raw))
    k_ref[...] = xk
    v_ref[...] = xv
    a_ref[...] = jax.nn.sigmoid(
        a0_ref[...] + jnp.dot(xa.astype(BF16), a2_ref[...], preferred_element_type=F32))
    g_ref[...] = jnp.dot(jax.nn.sigmoid(xg).astype(BF16), g2_ref[...], preferred_element_type=F32)


def _rwkv_prep(p, shifted, mu, w0, w2, a0, a2, g2, rd, lw, la, lgp):
    m, rpp = p.shape
    tm = _tile(m, 128, SUBLANE)
    row = lambda i: (i, 0)
    fix = lambda i: (0, 0)
    out = jax.ShapeDtypeStruct((m, rd), F32)
    return pl.pallas_call(
        functools.partial(_rwkv_prep_kernel, rd=rd, lw=lw, la=la, lgp=lgp),
        grid=(m // tm,),
        in_specs=[pl.BlockSpec((tm, rpp), row), pl.BlockSpec((tm, rpp), row),
                  pl.BlockSpec((1, rpp), fix), pl.BlockSpec((1, rd), fix), pl.BlockSpec((lw, rd), fix),
                  pl.BlockSpec((1, rd), fix), pl.BlockSpec((la, rd), fix), pl.BlockSpec((lgp, rd), fix)],
        out_specs=[pl.BlockSpec((tm, rd), row)] * 6,
        out_shape=[out] * 6,
        compiler_params=_cparams(("parallel",)),
        name="rwkv_prep",
    )(p, shifted, mu, w0, w2, a0, a2, g2)


def _rwkv_scan_kernel(r_ref, w_ref, k_ref, v_ref, a_ref, kkw_ref, ka_ref, rk_ref, lng_ref, lnb_ref,
                      s0_ref, y_ref, s_ref, *, tb):
    n = RWKV_HEAD_DIM

    @pl.when(pl.program_id(1) == 0)
    def _():
        s_ref[...] = s0_ref[...]

    kkw = kkw_ref[...]
    ka = ka_ref[...]
    rk = rk_ref[...]
    lng = lng_ref[...]
    lnb = lnb_ref[...]

    def step(t, carry):
        r = r_ref[t]
        w = w_ref[t]
        k = k_ref[t]
        a = a_ref[t]
        kk = k * kkw
        nrm = jnp.sqrt(jnp.sum(kk * kk, axis=0, keepdims=True))
        kk = kk / jnp.maximum(nrm, 1e-12)
        kp = k * (1.0 + (a - 1.0) * ka)
        bb = kk * a

        def rows(i8, c):
            base = pl.multiple_of(i8 * SUBLANE, SUBLANE)
            v8 = v_ref[t, pl.ds(base, SUBLANE), :]
            ys = []
            for ii in range(SUBLANE):
                s = s_ref[base + ii]
                sa = -jnp.sum(s * kk, axis=0, keepdims=True)
                sn = s * w + sa * bb + v8[ii:ii + 1, :] * kp
                s_ref[base + ii] = sn
                ys.append(jnp.sum(sn * r, axis=0, keepdims=True))
            y_ref[t, pl.ds(base, SUBLANE), :] = jnp.concatenate(ys, axis=0)
            return c

        lax.fori_loop(0, n // SUBLANE, rows, 0)
        y = y_ref[t]
        mu = jnp.mean(y, axis=0, keepdims=True)
        yc = y - mu
        var = jnp.mean(yc * yc, axis=0, keepdims=True)
        yn = yc * lax.rsqrt(var + RWKV_LN_EPS) * lng + lnb
        bonus = jnp.sum(r * kp * rk, axis=0, keepdims=True) * v_ref[t]
        y_ref[t] = yn + bonus
        return carry

    lax.fori_loop(0, tb, step, 0)


def _rwkv_scan(r, w, k, v, a, kkw, ka, rk, lng, lnb, s0):
    t, n, nc = r.shape
    tb = _tile(t, 32, 1)
    seq = pl.BlockSpec((tb, n, LANE), lambda c, tt: (tt, 0, c))
    par = pl.BlockSpec((n, LANE), lambda c, tt: (0, c))
    st = pl.BlockSpec((n, n, LANE), lambda c, tt: (0, 0, c))
    return pl.pallas_call(
        functools.partial(_rwkv_scan_kernel, tb=tb),
        grid=(nc // LANE, t // tb),
        in_specs=[seq] * 5 + [par] * 5 + [st],
        out_specs=[seq, st],
        out_shape=[jax.ShapeDtypeStruct((t, n, nc), F32), jax.ShapeDtypeStruct((n, n, nc), F32)],
        compiler_params=_cparams(("parallel", "arbitrary")),
        name="rwkv_scan",
    )(r, w, k, v, a, kkw, ka, rk, lng, lnb, s0)


def _rope(x, cos2, sin2):
    return x * cos2 + pltpu.roll(x, HEAD_DIM // 2, axis=1) * sin2


def _attn_prep_kernel(a_ref, cos_ref, sin_ref, lng_ref, lnb_ref,
                      q_ref, k_ref, v_ref, qih_ref, qil_ref, ki_ref, wi_ref, *, nh, kvh, ih, wscale):
    cos2 = cos_ref[...]
    sin2 = sin_ref[...]
    o = 0
    for h in range(nh):
        q_ref[:, h * HEAD_DIM:(h + 1) * HEAD_DIM] = _rope(a_ref[:, o:o + HEAD_DIM], cos2, sin2).astype(BF16)
        o += HEAD_DIM
    for h in range(kvh):
        k_ref[:, h * HEAD_DIM:(h + 1) * HEAD_DIM] = _rope(a_ref[:, o:o + HEAD_DIM], cos2, sin2)
        o += HEAD_DIM
    v_ref[...] = a_ref[:, o:o + kvh * HEAD_DIM]
    o += kvh * HEAD_DIM
    for h in range(ih):
        hi, lo = _split_bf16(_rope(a_ref[:, o:o + INDEX_DIM], cos2, sin2))
        qih_ref[:, h * INDEX_DIM:(h + 1) * INDEX_DIM] = hi
        qil_ref[:, h * INDEX_DIM:(h + 1) * INDEX_DIM] = lo
        o += INDEX_DIM
    x = a_ref[:, o:o + INDEX_DIM]
    o += INDEX_DIM
    mu = jnp.mean(x, axis=-1, keepdims=True)
    xc = x - mu
    var = jnp.mean(xc * xc, axis=-1, keepdims=True)
    kin = xc * lax.rsqrt(var + IDX_LN_EPS) * lng_ref[...] + lnb_ref[...]
    ki_ref[...] = _rope(kin, cos2, sin2)
    wi_ref[...] = a_ref[:, o:o + LANE] * wscale


def _attn_prep(a, cos2, sin2, lng, lnb, nh, kvh, ih):
    m, na = a.shape
    tm = _tile(m, 128, SUBLANE)
    row = lambda i: (i, 0)
    fix = lambda i: (0, 0)
    ad, kvd, idd = nh * HEAD_DIM, kvh * HEAD_DIM, ih * INDEX_DIM
    wscale = float(ih) ** -0.5 * float(INDEX_DIM) ** -0.5
    return pl.pallas_call(
        functools.partial(_attn_prep_kernel, nh=nh, kvh=kvh, ih=ih, wscale=wscale),
        grid=(m // tm,),
        in_specs=[pl.BlockSpec((tm, na), row), pl.BlockSpec((tm, LANE), row), pl.BlockSpec((tm, LANE), row),
                  pl.BlockSpec((1, LANE), fix), pl.BlockSpec((1, LANE), fix)],
        out_specs=[pl.BlockSpec((tm, ad), row), pl.BlockSpec((tm, kvd), row), pl.BlockSpec((tm, kvd), row),
                   pl.BlockSpec((tm, idd), row), pl.BlockSpec((tm, idd), row),
                   pl.BlockSpec((tm, INDEX_DIM), row), pl.BlockSpec((tm, LANE), row)],
        out_shape=[jax.ShapeDtypeStruct((m, ad), BF16), jax.ShapeDtypeStruct((m, kvd), F32),
                   jax.ShapeDtypeStruct((m, kvd), F32), jax.ShapeDtypeStruct((m, idd), BF16),
                   jax.ShapeDtypeStruct((m, idd), BF16), jax.ShapeDtypeStruct((m, INDEX_DIM), F32),
                   jax.ShapeDtypeStruct((m, LANE), F32)],
        compiler_params=_cparams(("parallel",)),
        name="attn_prep",
    )(a, cos2, sin2, lng, lnb)


def _topk_mask(score, key_ref, n_sel):
    l = score.shape[0]
    bits = lax.bitcast_convert_type(score, jnp.int32)
    key_ref[...] = jnp.where(bits < 0, bits ^ jnp.int32(0x7FFFFFFF), bits)
    sign = jnp.int32(-2 ** 31)
    nf = jnp.float32(n_sel)

    def body(i, ans):
        cand = ans | lax.shift_left(jnp.int32(1), 31 - i)
        cnt = jnp.sum(jnp.where(key_ref[...] >= (cand ^ sign), 1.0, 0.0), axis=0, keepdims=True)
        return jnp.where(cnt >= nf, cand, ans)

    ans = lax.fori_loop(0, 32, body, jnp.zeros((1, LANE), jnp.int32))
    thr = ans ^ sign
    need = nf - jnp.sum(jnp.where(key_ref[...] > thr, 1.0, 0.0), axis=0, keepdims=True)
    ch = PREFIX_CHUNK
    tri = jnp.where(lax.broadcasted_iota(jnp.int32, (ch, ch), 0) > lax.broadcasted_iota(jnp.int32, (ch, ch), 1),
                    1.0, 0.0).astype(BF16)
    run = jnp.zeros((1, LANE), F32)
    ranks = []
    for c in range(l // ch):
        e = jnp.where(key_ref[c * ch:(c + 1) * ch, :] == thr, 1.0, 0.0)
        ranks.append(jnp.dot(tri, e.astype(BF16), preferred_element_type=F32) + run)
        run = run + jnp.sum(e, axis=0, keepdims=True)
    rank = jnp.concatenate(ranks, axis=0)
    key = key_ref[...]
    return jnp.logical_or(key > thr, jnp.logical_and(key == thr, rank < need))


def _dsa_prompt_kernel(qih_ref, qil_ref, wit_ref, ki_ref, q_ref, k_ref, v_ref, o_ref, sc_ref, key_ref,
                       *, n_sel, ih, nh, kvh):
    l = ki_ref.shape[1]
    qb = pl.program_id(1)
    kih, kil = _split_bf16(ki_ref[0])
    sc_ref[...] = jnp.zeros_like(sc_ref)
    for hp in range(ih // 2):
        c0 = 2 * hp * INDEX_DIM
        c1 = c0 + INDEX_DIM
        rh = jnp.concatenate([qih_ref[:, c0:c1], qih_ref[:, c1:c1 + INDEX_DIM]], axis=0)
        rl = jnp.concatenate([qil_ref[:, c0:c1], qil_ref[:, c1:c1 + INDEX_DIM]], axis=0)
        rel = (lax.dot_general(kih, rh, NT_DIMS, preferred_element_type=F32)
               + lax.dot_general(kih, rl, NT_DIMS, preferred_element_type=F32)
               + lax.dot_general(kil, rh, NT_DIMS, preferred_element_type=F32))
        rel = jnp.maximum(rel, 0.0)
        w0 = wit_ref[0, 0, 2 * hp:2 * hp + 1, :]
        w1 = wit_ref[0, 0, 2 * hp + 1:2 * hp + 2, :]
        sc_ref[...] += rel[:, :Q_BLOCK] * w0 + rel[:, Q_BLOCK:] * w1
    s_idx = lax.broadcasted_iota(jnp.int32, (l, Q_BLOCK), 0)
    q_pos = qb * Q_BLOCK + lax.broadcasted_iota(jnp.int32, (l, Q_BLOCK), 1)
    causal = s_idx <= q_pos
    sel = _topk_mask(jnp.where(causal, sc_ref[...], -jnp.inf), key_ref, n_sel)
    bias = jnp.where(jnp.logical_and(sel, causal), 0.0, -jnp.inf).T
    scale = float(HEAD_DIM) ** -0.5
    rep = nh // kvh
    for g in range(kvh):
        kg = k_ref[0, :, g * HEAD_DIM:(g + 1) * HEAD_DIM].astype(BF16)
        vg = v_ref[0, :, g * HEAD_DIM:(g + 1) * HEAD_DIM].astype(BF16)
        for r in range(rep):
            h = g * rep + r
            qh = q_ref[:, h * HEAD_DIM:(h + 1) * HEAD_DIM]
            lg = lax.dot_general(qh, kg, NT_DIMS, preferred_element_type=F32) * scale + bias
            p = jnp.exp(lg - jnp.max(lg, axis=-1, keepdims=True))
            den = jnp.sum(p, axis=-1, keepdims=True)
            o = jnp.dot(p.astype(BF16), vg, preferred_element_type=F32) / den
            o_ref[:, h * HEAD_DIM:(h + 1) * HEAD_DIM] = o.astype(o_ref.dtype)


def _dsa_prompt(qih, qil, wit, ki, q, k, v, b, s, n_sel, ih, nh, kvh):
    nq = s // Q_BLOCK
    row = lambda bb, qq: (bb * nq + qq, 0)
    per_b = lambda bb, qq: (bb, 0, 0)
    return pl.pallas_call(
        functools.partial(_dsa_prompt_kernel, n_sel=n_sel, ih=ih, nh=nh, kvh=kvh),
        grid=(b, nq),
        in_specs=[pl.BlockSpec((Q_BLOCK, ih * INDEX_DIM), row), pl.BlockSpec((Q_BLOCK, ih * INDEX_DIM), row),
                  pl.BlockSpec((1, 1, ih, Q_BLOCK), lambda bb, qq: (bb, qq, 0, 0)),
                  pl.BlockSpec((1, s, INDEX_DIM), per_b),
                  pl.BlockSpec((Q_BLOCK, nh * HEAD_DIM), row),
                  pl.BlockSpec((1, s, kvh * HEAD_DIM), per_b), pl.BlockSpec((1, s, kvh * HEAD_DIM), per_b)],
        out_specs=pl.BlockSpec((Q_BLOCK, nh * HEAD_DIM), row),
        out_shape=jax.ShapeDtypeStruct((b * s, nh * HEAD_DIM), BF16),
        scratch_shapes=[pltpu.VMEM((s, Q_BLOCK), F32), pltpu.VMEM((s, Q_BLOCK), jnp.int32)],
        compiler_params=_cparams(("parallel", "arbitrary")),
        name="dsa_prompt",
    )(qih, qil, wit, ki, q, k, v)


def _dsa_score_kernel(pt_ref, qih_ref, qil_ref, wi_ref, kin_ref, *rest, n_pages):
    pages = rest[:n_pages]
    past_ref, new_ref = rest[n_pages:]
    qh = qih_ref[0]
    ql = qil_ref[0]
    wcol = wi_ref[0]
    for p in range(n_pages):
        kh, kl = _split_bf16(pages[p][0])
        rel = (lax.dot_general(qh, kh, NT_DIMS, preferred_element_type=F32)
               + lax.dot_general(qh, kl, NT_DIMS, preferred_element_type=F32)
               + lax.dot_general(ql, kh, NT_DIMS, preferred_element_type=F32))
        past_ref[0, p] = jnp.sum(jnp.maximum(rel, 0.0) * wcol, axis=0, keepdims=True)
    qf = qh.astype(F32) + ql.astype(F32)
    rel_new = jnp.maximum(jnp.sum(qf * kin_ref[0], axis=-1, keepdims=True), 0.0)
    new = jnp.sum(rel_new * wcol, axis=0, keepdims=True)
    new_ref[0] = jnp.broadcast_to(new, (1, LANE))


def _dsa_scores(page_table, qih, qil, wi, ki_new, pool_idx):
    db, n_pages = page_table.shape
    ih = qih.shape[1]
    psz = pool_idx.shape[1]
    per_b = lambda bb, pt: (bb, 0, 0)
    page_specs = [pl.BlockSpec((1, psz, INDEX_DIM), functools.partial(lambda bb, pt, p: (pt[bb, p], 0, 0), p=p))
                  for p in range(n_pages)]
    grid_spec = pltpu.PrefetchScalarGridSpec(
        num_scalar_prefetch=1,
        grid=(db,),
        in_specs=[pl.BlockSpec((1, ih, INDEX_DIM), per_b), pl.BlockSpec((1, ih, INDEX_DIM), per_b),
                  pl.BlockSpec((1, ih, 1), per_b), pl.BlockSpec((1, 1, INDEX_DIM), per_b)] + page_specs,
        out_specs=[pl.BlockSpec((1, n_pages, 1, psz), lambda bb, pt: (bb, 0, 0, 0)),
                   pl.BlockSpec((1, 1, LANE), per_b)],
    )
    return pl.pallas_call(
        functools.partial(_dsa_score_kernel, n_pages=n_pages),
        grid_spec=grid_spec,
        out_shape=[jax.ShapeDtypeStruct((db, n_pages, 1, psz), F32), jax.ShapeDtypeStruct((db, 1, LANE), F32)],
        compiler_params=_cparams(("arbitrary",)),
        name="dsa_sample_scores",
    )(page_table, qih, qil, wi, ki_new, *([pool_idx] * n_pages))


def _mask_kernel(sc_ref, m_ref, key_ref, *, n_sel):
    m_ref[...] = jnp.where(_topk_mask(sc_ref[...], key_ref, n_sel), 1.0, 0.0)


def _select_mask(score_t, n_sel):
    l, n = score_t.shape
    blk = pl.BlockSpec((l, LANE), lambda i: (0, i))
    return pl.pallas_call(
        functools.partial(_mask_kernel, n_sel=n_sel),
        grid=(n // LANE,),
        in_specs=[blk],
        out_specs=blk,
        out_shape=jax.ShapeDtypeStruct((l, n), F32),
        scratch_shapes=[pltpu.VMEM((l, LANE), jnp.int32)],
        compiler_params=_cparams(("parallel",)),
        name="select_mask",
    )(score_t)


def _dsa_sample_kernel(pt_ref, q_ref, bias_ref, kn_ref, vn_ref, *rest, n_pages, nh, kvh):
    kpages = rest[:n_pages]
    vpages = rest[n_pages:2 * n_pages]
    o_ref = rest[2 * n_pages]
    psz = kpages[0].shape[1]
    kvd = kvh * HEAD_DIM
    rep = nh // kvh
    scale = float(HEAD_DIM) ** -0.5
    q = q_ref[0]
    grp = lax.broadcasted_iota(jnp.int32, (nh, kvd), 1) // HEAD_DIM
    own = grp == lax.broadcasted_iota(jnp.int32, (nh, kvd), 0) // rep
    qbd = jnp.where(own, jnp.concatenate([q] * kvh, axis=1), jnp.zeros((), BF16))
    lgs = []
    for p in range(n_pages):
        lg = lax.dot_general(qbd, kpages[p][0].astype(BF16), NT_DIMS, preferred_element_type=F32)
        lgs.append(lg * scale + bias_ref[0, :, p * psz:(p + 1) * psz])
    first = lax.broadcasted_iota(jnp.int32, (LANE, kvd), 0) == 0
    kn8 = jnp.where(first, jnp.broadcast_to(kn_ref[0], (LANE, kvd)), 0.0).astype(BF16)
    vn8 = jnp.where(first, jnp.broadcast_to(vn_ref[0], (LANE, kvd)), 0.0).astype(BF16)
    lg_new = lax.dot_general(qbd, kn8, NT_DIMS, preferred_element_type=F32) * scale
    lg_new = lg_new + bias_ref[0, :, n_pages * psz:n_pages * psz + LANE]
    m = jnp.max(lg_new, axis=-1, keepdims=True)
    for lg in lgs:
        m = jnp.maximum(m, jnp.max(lg, axis=-1, keepdims=True))
    p_new = jnp.exp(lg_new - m)
    den = jnp.sum(p_new, axis=-1, keepdims=True)
    acc = jnp.dot(p_new.astype(BF16), vn8, preferred_element_type=F32)
    for p in range(n_pages):
        pp = jnp.exp(lgs[p] - m)
        den = den + jnp.sum(pp, axis=-1, keepdims=True)
        acc = acc + jnp.dot(pp.astype(BF16), vpages[p][0].astype(BF16), preferred_element_type=F32)
    acc = jnp.where(own, acc, 0.0) / den
    out = acc[:, 0:HEAD_DIM]
    for g in range(1, kvh):
        out = out + acc[:, g * HEAD_DIM:(g + 1) * HEAD_DIM]
    o_ref[0] = out.astype(o_ref.dtype)


def _dsa_sample(page_table, q, bias, k_new, v_new, pool_k, pool_v, nh, kvh):
    db, n_pages = page_table.shape
    psz, kvd = pool_k.shape[1], pool_k.shape[2]
    lp = bias.shape[2]
    per_b = lambda bb, pt: (bb, 0, 0)
    page_specs = [pl.BlockSpec((1, psz, kvd), functools.partial(lambda bb, pt, p: (pt[bb, p], 0, 0), p=p))
                  for p in range(n_pages)]
    grid_spec = pltpu.PrefetchScalarGridSpec(
        num_scalar_prefetch=1,
        grid=(db,),
        in_specs=[pl.BlockSpec((1, nh, HEAD_DIM), per_b), pl.BlockSpec((1, 1, lp), per_b),
                  pl.BlockSpec((1, 1, kvd), per_b), pl.BlockSpec((1, 1, kvd), per_b)] + page_specs + page_specs,
        out_specs=pl.BlockSpec((1, nh, HEAD_DIM), per_b),
    )
    return pl.pallas_call(
        functools.partial(_dsa_sample_kernel, n_pages=n_pages, nh=nh, kvh=kvh),
        grid_spec=grid_spec,
        out_shape=jax.ShapeDtypeStruct((db, nh, HEAD_DIM), BF16),
        compiler_params=_cparams(("arbitrary",)),
        name="dsa_sample_attend",
    )(page_table, q, bias, k_new, v_new, *([pool_k] * n_pages), *([pool_v] * n_pages))


def _cross_kernel(q_ref, mk_ref, mv_ref, o_ref, *, mh, tq):
    scale = float(HEAD_DIM) ** -0.5
    for h in range(mh):
        cs = slice(h * HEAD_DIM, (h + 1) * HEAD_DIM)
        qh = q_ref[0, :, cs].astype(BF16)
        if tq < SUBLANE:
            qh = jnp.broadcast_to(qh[0:1], (SUBLANE, HEAD_DIM))
        kh = mk_ref[0, :, cs].astype(BF16)
        vh = mv_ref[0, :, cs].astype(BF16)
        lg = lax.dot_general(qh, kh, NT_DIMS, preferred_element_type=F32) * scale
        p = jnp.exp(lg - jnp.max(lg, axis=-1, keepdims=True))
        den = jnp.sum(p, axis=-1, keepdims=True)
        o = jnp.dot(p.astype(BF16), vh, preferred_element_type=F32) / den
        o_ref[0, :, cs] = o[0:tq].astype(o_ref.dtype)


def _cross_attend(q, mk, mv, mh):
    b, t, md = q.shape
    nm = mk.shape[1]
    tq = _tile(t, 512, SUBLANE)
    return pl.pallas_call(
        functools.partial(_cross_kernel, mh=mh, tq=tq),
        grid=(b, t // tq),
        in_specs=[pl.BlockSpec((1, tq, md), lambda bb, i: (bb, i, 0)),
                  pl.BlockSpec((1, nm, md), lambda bb, i: (bb, 0, 0)),
                  pl.BlockSpec((1, nm, md), lambda bb, i: (bb, 0, 0))],
        out_specs=pl.BlockSpec((1, tq, md), lambda bb, i: (bb, i, 0)),
        out_shape=jax.ShapeDtypeStruct((b, t, md), BF16),
        compiler_params=_cparams(("parallel", "parallel")),
        name="cross_attend",
    )(q, mk, mv)


def _to_chains(x, b, t, h):
    nc = b * h
    y = x.reshape(b, t, h, RWKV_HEAD_DIM).transpose(1, 3, 0, 2).reshape(t, RWKV_HEAD_DIM, nc)
    return jnp.pad(y, ((0, 0), (0, 0), (0, _round_up(nc, LANE) - nc)))


def _param_chains(p, b, h):
    nc = b * h
    y = jnp.tile(p.reshape(h, RWKV_HEAD_DIM).T, (1, b))
    return jnp.pad(y, ((0, 0), (0, _round_up(nc, LANE) - nc)))


def _rope_tables(pos):
    half = HEAD_DIM // 2
    inv = jnp.power(ROPE_THETA, -jnp.arange(half, dtype=F32) / half)
    ang = pos.astype(F32)[:, None] * inv[None, :]
    cos, sin = jnp.cos(ang), jnp.sin(ang)
    return jnp.concatenate([cos, cos], axis=-1), jnp.concatenate([-sin, sin], axis=-1)


def _mixer(x, b, t, pos, shift_prev, s0, lw, dims, attend):
    d, rd, lwd, lad, lgp, rp, rpp, nh, kvh, ih = dims
    h = rd // RWKV_HEAD_DIM
    m = b * t
    hn = _rmsnorm(x, lw["norm_mix"], BF16)
    gates = _mm(hn, lw["w_gates"])
    p = _mm(hn, lw["w_rwkv"])
    att = _mm(hn, lw["w_attn"])

    p3 = p.reshape(b, t, rpp)
    prev = jnp.pad(shift_prev, ((0, 0), (0, rpp - rp)))
    shifted = jnp.concatenate([prev[:, None, :], p3[:, :-1]], axis=1).reshape(m, rpp)
    r, w, k, v, a, g = _rwkv_prep(p, shifted, lw["mu"], lw["w0"], lw["w2"], lw["a0"], lw["a2"], lw["g2"],
                                  rd, lwd, lad, lgp)
    nc = b * h
    s0c = jnp.pad(s0.reshape(nc, RWKV_HEAD_DIM, RWKV_HEAD_DIM).transpose(1, 2, 0),
                  ((0, 0), (0, 0), (0, _round_up(nc, LANE) - nc)))
    yc, sc = _rwkv_scan(*[_to_chains(z, b, t, h) for z in (r, w, k, v, a)],
                        _param_chains(lw["kk"], b, h), _param_chains(lw["ka"], b, h),
                        _param_chains(lw["rk"], b, h), _param_chains(lw["ln_g"], b, h),
                        _param_chains(lw["ln_b"], b, h), s0c)
    yr = yc[:, :, :nc].reshape(t, RWKV_HEAD_DIM, b, h).transpose(2, 0, 3, 1).reshape(m, rd)
    s_fin = sc[:, :, :nc].transpose(2, 0, 1).reshape(b, h, RWKV_HEAD_DIM, RWKV_HEAD_DIM)
    shift_new = p3[:, -1, :rp]

    cos2, sin2 = _rope_tables(pos)
    cos2 = jnp.broadcast_to(cos2[None], (b, t, LANE)).reshape(m, LANE)
    sin2 = jnp.broadcast_to(sin2[None], (b, t, LANE)).reshape(m, LANE)
    q, kr, vv, qih, qil, ki, wi = _attn_prep(att, cos2, sin2, lw["idx_ln_g"], lw["idx_ln_b"], nh, kvh, ih)
    ya = attend(q, qih, qil, wi, kr, vv, ki)

    merged = _merge(yr, g, ya, gates, lw["w_br_rwkv"], lw["w_br_attn"])
    x1 = _mm_resid(merged, lw["w_out"], x)
    return x1, kr, vv, ki, s_fin, shift_new


def _attend_prompt(q, qih, qil, wi, kr, vv, ki, *, b, s, nh, kvh, ih):
    n_sel = min(TOPK_MAX, s // 4)
    nq = s // Q_BLOCK
    wit = wi[:, :ih].reshape(b, nq, Q_BLOCK, ih).transpose(0, 1, 3, 2)
    return _dsa_prompt(qih, qil, wit, ki.reshape(b, s, INDEX_DIM), q,
                       kr.reshape(b, s, kvh * HEAD_DIM), vv.reshape(b, s, kvh * HEAD_DIM),
                       b, s, n_sel, ih, nh, kvh)


def _attend_sample(q, qih, qil, wi, kr, vv, ki, *, pool_k, pool_v, pool_idx, page_table, nh, kvh, ih):
    db, n_pages = page_table.shape
    psz = pool_idx.shape[1]
    past = n_pages * psz
    n_sel = min(TOPK_MAX, (past + 1) // 4)
    kvd = kvh * HEAD_DIM
    sc_past, sc_new = _dsa_scores(page_table, qih.reshape(db, ih, INDEX_DIM), qil.reshape(db, ih, INDEX_DIM),
                                  wi[:, :ih].reshape(db, ih, 1), ki.reshape(db, 1, INDEX_DIM), pool_idx)
    lp = _round_up(past + 1, PREFIX_CHUNK)
    dbp = _round_up(db, LANE)
    score = jnp.concatenate([sc_past.reshape(db, past), sc_new[:, 0, :1]], axis=1)
    score_t = jnp.pad(score.T, ((0, lp - past - 1), (0, dbp - db)), constant_values=-jnp.inf)
    mask = _select_mask(score_t, n_sel)[:, :db].T
    bias = jnp.where(mask > 0.5, 0.0, -jnp.inf).astype(F32).reshape(db, 1, lp)
    o = _dsa_sample(page_table, q.reshape(db, nh, HEAD_DIM), bias, kr.reshape(db, 1, kvd),
                    vv.reshape(db, 1, kvd), pool_k.reshape(-1, psz, kvd), pool_v.reshape(-1, psz, kvd), nh, kvh)
    return o.reshape(db, nh * HEAD_DIM)


def _cross_and_ffn(x, b, t, mk, mv, lw, mh):
    hc = _rmsnorm(x, lw["norm_cross"], BF16)
    qm = _mm(hc, lw["w_q_mem"])
    o = _cross_attend(qm.reshape(b, t, -1), mk, mv, mh)
    x2 = _mm_resid(o.reshape(b * t, -1), lw["w_o_mem"], x)
    hf = _rmsnorm(x2, lw["norm_ffn"], BF16)
    act = _swiglu(hf, lw["w_up"])
    return _mm_resid(act, lw["w_down"], x2, tn_pref=512, tk_pref=5504)


def kernel(x_prompt, mem_prompt, x_sample, cache_k, cache_v, cache_idx_k, cache_mem_k, cache_mem_v, state_rwkv, state_shift, page_table, norm_mix, w_in, rwkv_mu, rwkv_w0, rwkv_w2, rwkv_a0, rwkv_a2, rwkv_g2, rwkv_kk, rwkv_ka, rwkv_rk, rwkv_ln_g, rwkv_ln_b, idx_ln_g, idx_ln_b, w_br_rwkv, w_br_attn, w_out, norm_cross, norm_mem, w_q_mem, w_k_mem, w_v_mem, w_o_mem, norm_ffn, w_up, w_down, final_norm):
    b, s, d = x_prompt.shape
    db, ts, _ = x_sample.shape
    assert ts == 1, "the sample group decodes one token per sequence"
    depth = w_in.shape[0]
    in_proj = w_in.shape[2]
    rp = rwkv_mu.shape[1]
    lwd, rd = rwkv_w2.shape[1:]
    lad = rwkv_a2.shape[1]
    lg = rwkv_g2.shape[1]
    lgp = _round_up(lg, LANE)
    rpp = rp - lg + lgp
    ad = w_br_attn.shape[1]
    nh = ad // HEAD_DIM
    kvh = cache_k.shape[3]
    kvd = kvh * HEAD_DIM
    ih = (in_proj - 2 * d - rp - ad - 2 * kvd - INDEX_DIM) // (INDEX_DIM + 1)
    mh = cache_mem_k.shape[3]
    n_mem = mem_prompt.shape[1]
    past = page_table.shape[1] * cache_k.shape[2]
    assert rp == 3 * rd + lwd + lad + lg and s % Q_BLOCK == 0 and ih % 2 == 0
    dims = (d, rd, lwd, lad, lgp, rp, rpp, nh, kvh, ih)

    xp = x_prompt.reshape(b * s, d)
    xs = x_sample.reshape(db, d)
    pos_p = jnp.arange(s, dtype=jnp.int32)
    pos_s = past + jnp.arange(ts, dtype=jnp.int32)
    outs = [[] for _ in range(12)]
    for l in range(depth):
        wl = w_in[l]
        o_att = 2 * d + rp
        n_att = ad + 2 * kvd + ih * INDEX_DIM + INDEX_DIM
        lw = {
            "norm_mix": norm_mix[l],
            "w_gates": wl[:, :2 * d].astype(BF16),
            "w_rwkv": jnp.pad(wl[:, 2 * d:o_att], ((0, 0), (0, rpp - rp))).astype(BF16),
            "w_attn": jnp.pad(wl[:, o_att:], ((0, 0), (0, LANE - ih))).astype(BF16),
            "mu": jnp.pad(rwkv_mu[l], (0, rpp - rp)).reshape(1, rpp),
            "w0": rwkv_w0[l].reshape(1, rd), "w2": rwkv_w2[l].astype(BF16),
            "a0": rwkv_a0[l].reshape(1, rd), "a2": rwkv_a2[l].astype(BF16),
            "g2": jnp.pad(rwkv_g2[l], ((0, lgp - lg), (0, 0))).astype(BF16),
            "kk": rwkv_kk[l], "ka": rwkv_ka[l], "rk": rwkv_rk[l].reshape(-1),
            "ln_g": rwkv_ln_g[l], "ln_b": rwkv_ln_b[l],
            "idx_ln_g": idx_ln_g[l].reshape(1, INDEX_DIM), "idx_ln_b": idx_ln_b[l].reshape(1, INDEX_DIM),
            "w_br_rwkv": w_br_rwkv[l].astype(BF16), "w_br_attn": w_br_attn[l].astype(BF16),
            "w_out": w_out[l].astype(BF16),
            "norm_cross": norm_cross[l], "norm_ffn": norm_ffn[l],
            "w_q_mem": w_q_mem[l].astype(BF16), "w_o_mem": w_o_mem[l].astype(BF16),
            "w_up": w_up[l].astype(BF16), "w_down": w_down[l].astype(BF16),
        }
        assert n_att + ih == in_proj - o_att
        xp, kp, vp, kip, sfp, shp = _mixer(
            xp, b, s, pos_p, jnp.zeros((b, rp), F32),
            jnp.zeros((b, rd // RWKV_HEAD_DIM, RWKV_HEAD_DIM, RWKV_HEAD_DIM), F32), lw, dims,
            functools.partial(_attend_prompt, b=b, s=s, nh=nh, kvh=kvh, ih=ih))
        mn = _rmsnorm(mem_prompt.reshape(b * n_mem, d), norm_mem[l], BF16)
        mkp = _mm(mn, w_k_mem[l].astype(BF16)).reshape(b, n_mem, mh * HEAD_DIM)
        mvp = _mm(mn, w_v_mem[l].astype(BF16)).reshape(b, n_mem, mh * HEAD_DIM)
        xp = _cross_and_ffn(xp, b, s, mkp, mvp, lw, mh)
        xs, ks_, vs_, kis, sfs, shs = _mixer(
            xs, db, ts, pos_s, state_shift[l], state_rwkv[l], lw, dims,
            functools.partial(_attend_sample, pool_k=cache_k[l], pool_v=cache_v[l], pool_idx=cache_idx_k[l],
                              page_table=page_table, nh=nh, kvh=kvh, ih=ih))
        xs = _cross_and_ffn(xs, db, ts, cache_mem_k[l].reshape(db, n_mem, mh * HEAD_DIM),
                            cache_mem_v[l].reshape(db, n_mem, mh * HEAD_DIM), lw, mh)
        vals = (kp.reshape(b, s, kvh, HEAD_DIM), vp.reshape(b, s, kvh, HEAD_DIM), kip.reshape(b, s, INDEX_DIM),
                mkp.reshape(b, n_mem, mh, HEAD_DIM), mvp.reshape(b, n_mem, mh, HEAD_DIM), sfp, shp,
                ks_.reshape(db, ts, kvh, HEAD_DIM), vs_.reshape(db, ts, kvh, HEAD_DIM),
                kis.reshape(db, ts, INDEX_DIM), sfs, shs)
        for lst, val in zip(outs, vals):
            lst.append(val)
    y_prompt = _rmsnorm(xp, final_norm, F32).reshape(b, s, d)
    y_sample = _rmsnorm(xs, final_norm, F32).reshape(db, ts, d)
    return (y_prompt, y_sample) + tuple(jnp.stack(o) for o in outs)
```

```python
import functools

import jax
import jax.numpy as jnp
from jax import lax
from jax.experimental import pallas as pl
from jax.experimental.pallas import tpu as pltpu

F32 = jnp.float32
BF16 = jnp.bfloat16
LANE = 128
SUBLANE = 8
VMEM_LIMIT = 56 * 1024 * 1024

RWKV_HEAD_DIM = 64
HEAD_DIM = 128
INDEX_DIM = 128
TOPK_MAX = 256
Q_BLOCK = 128
ROPE_THETA = 10000.0
NORM_EPS = 1e-6
RWKV_LN_EPS = 64e-5
IDX_LN_EPS = 1e-6
PREFIX_CHUNK = 256
MAX_CAUSAL_BUCKETS = 4
NT_DIMS = (((1,), (1,)), ((), ()))


def _cparams(sem):
    return pltpu.CompilerParams(dimension_semantics=sem, vmem_limit_bytes=VMEM_LIMIT)


def _round_up(n, m):
    return (n + m - 1) // m * m


def _tile(n, pref, unit):
    best = None
    t = unit
    while t <= min(n, pref):
        if n % t == 0:
            best = t
        t += unit
    return best if best is not None else n


def _rmsnorm_kernel(x_ref, g_ref, o_ref):
    x = x_ref[...]
    ms = jnp.mean(x * x, axis=-1, keepdims=True)
    o_ref[...] = (x * lax.rsqrt(ms + NORM_EPS) * g_ref[...]).astype(o_ref.dtype)


def _rmsnorm(x, g, out_dtype):
    m, d = x.shape
    tm = _tile(m, 256, SUBLANE)
    return pl.pallas_call(
        _rmsnorm_kernel,
        grid=(m // tm,),
        in_specs=[pl.BlockSpec((tm, d), lambda i: (i, 0)), pl.BlockSpec((1, d), lambda i: (0, 0))],
        out_specs=pl.BlockSpec((tm, d), lambda i: (i, 0)),
        out_shape=jax.ShapeDtypeStruct((m, d), out_dtype),
        compiler_params=_cparams(("parallel",)),
        name="rmsnorm",
    )(x, g.reshape(1, d))


def _mm_kernel(x_ref, w_ref, o_ref):
    o_ref[...] = jnp.dot(x_ref[...], w_ref[...], preferred_element_type=F32).astype(o_ref.dtype)


def _mm(x, w, out_dtype=F32, tm_pref=1024, tn_pref=512):
    m, k = x.shape
    n = w.shape[1]
    tm = _tile(m, tm_pref, SUBLANE)
    tn = _tile(n, tn_pref, LANE)
    return pl.pallas_call(
        _mm_kernel,
        grid=(m // tm, n // tn),
        in_specs=[pl.BlockSpec((tm, k), lambda i, j: (i, 0)), pl.BlockSpec((k, tn), lambda i, j: (0, j))],
        out_specs=pl.BlockSpec((tm, tn), lambda i, j: (i, j)),
        out_shape=jax.ShapeDtypeStruct((m, n), out_dtype),
        compiler_params=_cparams(("parallel", "parallel")),
        name="matmul",
    )(x, w)


def _mm_resid_kernel(x_ref, w_ref, r_ref, o_ref, acc_ref, *, nk):
    kk = pl.program_id(2)

    @pl.when(kk == 0)
    def _():
        acc_ref[...] = r_ref[...]

    acc_ref[...] += jnp.dot(x_ref[...], w_ref[...], preferred_element_type=F32)

    @pl.when(kk == nk - 1)
    def _():
        o_ref[...] = acc_ref[...]


def _mm_resid(x, w, resid, tm_pref=1024, tn_pref=1024, tk_pref=2048):
    m, k = x.shape
    n = w.shape[1]
    tm = _tile(m, tm_pref, SUBLANE)
    tn = _tile(n, tn_pref, LANE)
    tk = _tile(k, tk_pref, LANE)
    nk = k // tk
    return pl.pallas_call(
        functools.partial(_mm_resid_kernel, nk=nk),
        grid=(m // tm, n // tn, nk),
        in_specs=[pl.BlockSpec((tm, tk), lambda i, j, kk: (i, kk)),
                  pl.BlockSpec((tk, tn), lambda i, j, kk: (kk, j)),
                  pl.BlockSpec((tm, tn), lambda i, j, kk: (i, j))],
        out_specs=pl.BlockSpec((tm, tn), lambda i, j, kk: (i, j)),
        out_shape=jax.ShapeDtypeStruct((m, n), F32),
        scratch_shapes=[pltpu.VMEM((tm, tn), F32)],
        compiler_params=_cparams(("parallel", "parallel", "arbitrary")),
        name="matmul_resid",
    )(x, w, resid)


def _swiglu_kernel(x_ref, wg_ref, wu_ref, o_ref):
    x = x_ref[...]
    g = jnp.dot(x, wg_ref[...], preferred_element_type=F32)
    u = jnp.dot(x, wu_ref[...], preferred_element_type=F32)
    o_ref[...] = (g * jax.nn.sigmoid(g) * u).astype(o_ref.dtype)


def _swiglu(x, w_up, tm_pref=1024, tn_pref=256):
    m, k = x.shape
    f = w_up.shape[1] // 2
    tm = _tile(m, tm_pref, SUBLANE)
    tn = _tile(f, tn_pref, LANE)
    nb = f // tn
    return pl.pallas_call(
        _swiglu_kernel,
        grid=(m // tm, nb),
        in_specs=[pl.BlockSpec((tm, k), lambda i, j: (i, 0)),
                  pl.BlockSpec((k, tn), lambda i, j: (0, j)),
                  pl.BlockSpec((k, tn), lambda i, j: (0, j + nb))],
        out_specs=pl.BlockSpec((tm, tn), lambda i, j: (i, j)),
        out_shape=jax.ShapeDtypeStruct((m, f), BF16),
        compiler_params=_cparams(("parallel", "parallel")),
        name="swiglu",
    )(x, w_up, w_up)


def _gate_mul_kernel(y_ref, g_ref, o_ref):
    o_ref[...] = (y_ref[...] * g_ref[...]).astype(o_ref.dtype)


def _gate_mul(y, g):
    m, n = y.shape
    tm = _tile(m, 256, SUBLANE)
    blk = pl.BlockSpec((tm, n), lambda i: (i, 0))
    return pl.pallas_call(
        _gate_mul_kernel,
        grid=(m // tm,),
        in_specs=[blk, blk],
        out_specs=blk,
        out_shape=jax.ShapeDtypeStruct((m, n), BF16),
        compiler_params=_cparams(("parallel",)),
        name="gate_mul",
    )(y, g)


def _merge_kernel(yr_ref, ya_ref, gr_ref, ga_ref, wr_ref, wa_ref, o_ref):
    br = jnp.dot(yr_ref[...], wr_ref[...], preferred_element_type=F32)
    ba = jnp.dot(ya_ref[...], wa_ref[...], preferred_element_type=F32)
    o_ref[...] = (jax.nn.sigmoid(gr_ref[...]) * br + jax.nn.sigmoid(ga_ref[...]) * ba).astype(o_ref.dtype)


def _merge(yr, ya, gates, w_br_r, w_br_a, tm_pref=1024, tn_pref=512):
    m, rd = yr.shape
    ad = ya.shape[1]
    d = w_br_r.shape[1]
    tm = _tile(m, tm_pref, SUBLANE)
    tn = _tile(d, tn_pref, LANE)
    nb = d // tn
    return pl.pallas_call(
        _merge_kernel,
        grid=(m // tm, nb),
        in_specs=[pl.BlockSpec((tm, rd), lambda i, j: (i, 0)),
                  pl.BlockSpec((tm, ad), lambda i, j: (i, 0)),
                  pl.BlockSpec((tm, tn), lambda i, j: (i, j)),
                  pl.BlockSpec((tm, tn), lambda i, j: (i, j + nb)),
                  pl.BlockSpec((rd, tn), lambda i, j: (0, j)),
                  pl.BlockSpec((ad, tn), lambda i, j: (0, j))],
        out_specs=pl.BlockSpec((tm, tn), lambda i, j: (i, j)),
        out_shape=jax.ShapeDtypeStruct((m, d), BF16),
        compiler_params=_cparams(("parallel", "parallel")),
        name="merge",
    )(yr, ya, gates, gates, w_br_r, w_br_a)


def _rwkv_prep_kernel(p_ref, sh_ref, *rest, rd, lw, la, lgp, seq_blocks):
    if seq_blocks is not None:
        prev_ref, rest = rest[0], rest[1:]
    mu_ref, w0_ref, w2_ref, a0_ref, a2_ref, g2_ref, r_ref, w_ref, k_ref, v_ref, a_ref, g_ref = rest
    p = p_ref[...]
    if seq_blocks is None:
        shifted = sh_ref[...]
    else:
        first = pl.program_id(0) % seq_blocks == 0
        above = jnp.where(first, prev_ref[0], sh_ref[SUBLANE - 1:SUBLANE, :])
        row0 = lax.broadcasted_iota(jnp.int32, p.shape, 0) == 0
        shifted = jnp.where(row0, above, pltpu.roll(p, 1, axis=0))
    xm = p + (shifted - p) * mu_ref[...]
    o = 0
    xr = xm[:, o:o + rd]; o += rd
    xw = xm[:, o:o + lw]; o += lw
    xk = xm[:, o:o + rd]; o += rd
    xv = xm[:, o:o + rd]; o += rd
    xa = xm[:, o:o + la]; o += la
    xg = xm[:, o:o + lgp]
    z = w0_ref[...] + jnp.dot(jnp.tanh(xw).astype(BF16), w2_ref[...], preferred_element_type=F32)
    sp = jnp.maximum(-z, 0.0) + jnp.log(1.0 + jnp.exp(-jnp.abs(z)))
    w_raw = -sp - 0.5
    r_ref[...] = xr
    w_ref[...] = jnp.exp(-jnp.exp(w_raw))
    k_ref[...] = xk
    v_ref[...] = xv
    a_ref[...] = jax.nn.sigmoid(
        a0_ref[...] + jnp.dot(xa.astype(BF16), a2_ref[...], preferred_element_type=F32))
    g_ref[...] = jnp.dot(jax.nn.sigmoid(xg).astype(BF16), g2_ref[...], preferred_element_type=F32)


def _rwkv_prep(p, prev, t, mu, w0, w2, a0, a2, g2, rd, lw, la, lgp):
    m, rpp = p.shape
    row = lambda i: (i, 0)
    fix = lambda i: (0, 0)
    if t == 1:
        tm = _tile(m, 128, SUBLANE)
        seq_blocks = None
        shift_args = (prev,)
        shift_specs = [pl.BlockSpec((tm, rpp), row)]
    else:
        tm = _tile(t, 128, SUBLANE)
        seq_blocks = t // tm
        shift_args = (p, prev.reshape(-1, 1, rpp))
        shift_specs = [pl.BlockSpec((SUBLANE, rpp), lambda i: (jnp.maximum(i * (tm // SUBLANE) - 1, 0), 0)),
                       pl.BlockSpec((1, 1, rpp), lambda i: (i // seq_blocks, 0, 0))]
    out = jax.ShapeDtypeStruct((m, rd), F32)
    return pl.pallas_call(
        functools.partial(_rwkv_prep_kernel, rd=rd, lw=lw, la=la, lgp=lgp, seq_blocks=seq_blocks),
        grid=(m // tm,),
        in_specs=[pl.BlockSpec((tm, rpp), row)] + shift_specs + [
            pl.BlockSpec((1, rpp), fix), pl.BlockSpec((1, rd), fix), pl.BlockSpec((lw, rd), fix),
            pl.BlockSpec((1, rd), fix), pl.BlockSpec((la, rd), fix), pl.BlockSpec((lgp, rd), fix)],
        out_specs=[pl.BlockSpec((tm, rd), row)] * 6,
        out_shape=[out] * 6,
        compiler_params=_cparams(("parallel",)),
        name="rwkv_prep",
    )(p, *shift_args, mu, w0, w2, a0, a2, g2)


def _rwkv_scan_kernel(r_ref, w_ref, k_ref, v_ref, a_ref, kkw_ref, ka_ref, rk_ref, lng_ref, lnb_ref,
                      s0_ref, y_ref, s_ref, *, tb):
    n = RWKV_HEAD_DIM

    @pl.when(pl.program_id(1) == 0)
    def _():
        s_ref[...] = s0_ref[...]

    kkw = kkw_ref[...]
    ka = ka_ref[...]
    rk = rk_ref[...]
    lng = lng_ref[...]
    lnb = lnb_ref[...]

    def step(t, carry):
        r = r_ref[t]
        w = w_ref[t]
        k = k_ref[t]
        a = a_ref[t]
        kk = k * kkw
        nrm = jnp.sqrt(jnp.sum(kk * kk, axis=0, keepdims=True))
        kk = kk / jnp.maximum(nrm, 1e-12)
        kp = k * (1.0 + (a - 1.0) * ka)
        nb = -(kk * a)

        def rows(i8, c):
            base = pl.multiple_of(i8 * SUBLANE, SUBLANE)
            v8 = v_ref[t, pl.ds(base, SUBLANE), :]
            ys = []
            for ii in range(SUBLANE):
                s = s_ref[base + ii]
                sk = jnp.sum(s * kk, axis=0, keepdims=True)
                sn = s * w + sk * nb + v8[ii:ii + 1, :] * kp
                s_ref[base + ii] = sn
                ys.append(jnp.sum(sn * r, axis=0, keepdims=True))
            y_ref[t, pl.ds(base, SUBLANE), :] = jnp.concatenate(ys, axis=0)
            return c

        lax.fori_loop(0, n // SUBLANE, rows, 0)
        y = y_ref[t]
        mu = jnp.mean(y, axis=0, keepdims=True)
        yc = y - mu
        var = jnp.mean(yc * yc, axis=0, keepdims=True)
        yn = yc * lax.rsqrt(var + RWKV_LN_EPS) * lng + lnb
        bonus = jnp.sum(r * kp * rk, axis=0, keepdims=True) * v_ref[t]
        y_ref[t] = yn + bonus
        return carry

    lax.fori_loop(0, tb, step, 0)


def _rwkv_scan(r, w, k, v, a, kkw, ka, rk, lng, lnb, s0):
    t, n, nc = r.shape
    tb = _tile(t, 32, 1)
    seq = pl.BlockSpec((tb, n, LANE), lambda c, tt: (tt, 0, c))
    par = pl.BlockSpec((n, LANE), lambda c, tt: (0, c))
    st = pl.BlockSpec((n, n, LANE), lambda c, tt: (0, 0, c))
    return pl.pallas_call(
        functools.partial(_rwkv_scan_kernel, tb=tb),
        grid=(nc // LANE, t // tb),
        in_specs=[seq] * 5 + [par] * 5 + [st],
        out_specs=[seq, st],
        out_shape=[jax.ShapeDtypeStruct((t, n, nc), F32), jax.ShapeDtypeStruct((n, n, nc), F32)],
        compiler_params=_cparams(("parallel", "arbitrary")),
        name="rwkv_scan",
    )(r, w, k, v, a, kkw, ka, rk, lng, lnb, s0)


def _rope(x, cos2, sin2):
    return x * cos2 + pltpu.roll(x, HEAD_DIM // 2, axis=1) * sin2


def _attn_prep_kernel(a_ref, cos_ref, sin_ref, lng_ref, lnb_ref,
                      q_ref, k_ref, v_ref, qi_ref, ki_ref, wi_ref, *, nh, kvh, ih, wscale):
    cos2 = cos_ref[...]
    sin2 = sin_ref[...]
    o = 0
    for h in range(nh):
        q_ref[:, h * HEAD_DIM:(h + 1) * HEAD_DIM] = _rope(a_ref[:, o:o + HEAD_DIM], cos2, sin2).astype(BF16)
        o += HEAD_DIM
    for h in range(kvh):
        k_ref[:, h * HEAD_DIM:(h + 1) * HEAD_DIM] = _rope(a_ref[:, o:o + HEAD_DIM], cos2, sin2)
        o += HEAD_DIM
    v_ref[...] = a_ref[:, o:o + kvh * HEAD_DIM]
    o += kvh * HEAD_DIM
    for h in range(ih):
        qi_ref[:, h * INDEX_DIM:(h + 1) * INDEX_DIM] = _rope(a_ref[:, o:o + INDEX_DIM], cos2, sin2).astype(BF16)
        o += INDEX_DIM
    x = a_ref[:, o:o + INDEX_DIM]
    o += INDEX_DIM
    mu = jnp.mean(x, axis=-1, keepdims=True)
    xc = x - mu
    var = jnp.mean(xc * xc, axis=-1, keepdims=True)
    kin = xc * lax.rsqrt(var + IDX_LN_EPS) * lng_ref[...] + lnb_ref[...]
    ki_ref[...] = _rope(kin, cos2, sin2)
    wi_ref[...] = a_ref[:, o:o + LANE] * wscale


def _attn_prep(a, cos2, sin2, lng, lnb, nh, kvh, ih):
    m, na = a.shape
    tm = _tile(m, 128, SUBLANE)
    row = lambda i: (i, 0)
    fix = lambda i: (0, 0)
    ad, kvd, idd = nh * HEAD_DIM, kvh * HEAD_DIM, ih * INDEX_DIM
    wscale = float(ih) ** -0.5 * float(INDEX_DIM) ** -0.5
    return pl.pallas_call(
        functools.partial(_attn_prep_kernel, nh=nh, kvh=kvh, ih=ih, wscale=wscale),
        grid=(m // tm,),
        in_specs=[pl.BlockSpec((tm, na), row), pl.BlockSpec((tm, LANE), row), pl.BlockSpec((tm, LANE), row),
                  pl.BlockSpec((1, LANE), fix), pl.BlockSpec((1, LANE), fix)],
        out_specs=[pl.BlockSpec((tm, ad), row), pl.BlockSpec((tm, kvd), row), pl.BlockSpec((tm, kvd), row),
                   pl.BlockSpec((tm, idd), row),
                   pl.BlockSpec((tm, INDEX_DIM), row), pl.BlockSpec((tm, LANE), row)],
        out_shape=[jax.ShapeDtypeStruct((m, ad), BF16), jax.ShapeDtypeStruct((m, kvd), F32),
                   jax.ShapeDtypeStruct((m, kvd), F32), jax.ShapeDtypeStruct((m, idd), BF16),
                   jax.ShapeDtypeStruct((m, INDEX_DIM), F32),
                   jax.ShapeDtypeStruct((m, LANE), F32)],
        compiler_params=_cparams(("parallel",)),
        name="attn_prep",
    )(a, cos2, sin2, lng, lnb)


def _topk_mask(score, key_ref, n_sel):
    l = score.shape[0]
    bits = lax.bitcast_convert_type(score, jnp.int32)
    key_ref[...] = jnp.where(bits < 0, bits ^ jnp.int32(0x7FFFFFFF), bits)
    sign = jnp.int32(-2 ** 31)
    nf = jnp.float32(n_sel)

    def body(i, ans):
        cand = ans | lax.shift_left(jnp.int32(1), 31 - i)
        cnt = jnp.sum(jnp.where(key_ref[...] >= (cand ^ sign), 1.0, 0.0), axis=0, keepdims=True)
        return jnp.where(cnt >= nf, cand, ans)

    ans = lax.fori_loop(0, 32, body, jnp.zeros((1, LANE), jnp.int32))
    thr = ans ^ sign
    need = nf - jnp.sum(jnp.where(key_ref[...] > thr, 1.0, 0.0), axis=0, keepdims=True)
    ch = PREFIX_CHUNK
    tri = jnp.where(lax.broadcasted_iota(jnp.int32, (ch, ch), 0) > lax.broadcasted_iota(jnp.int32, (ch, ch), 1),
                    1.0, 0.0).astype(BF16)
    run = jnp.zeros((1, LANE), F32)
    ranks = []
    for c in range(l // ch):
        e = jnp.where(key_ref[c * ch:(c + 1) * ch, :] == thr, 1.0, 0.0)
        ranks.append(jnp.dot(tri, e.astype(BF16), preferred_element_type=F32) + run)
        run = run + jnp.sum(e, axis=0, keepdims=True)
    rank = jnp.concatenate(ranks, axis=0)
    key = key_ref[...]
    return jnp.logical_or(key > thr, jnp.logical_and(key == thr, rank < need))


def _dsa_prompt_kernel(qi_ref, wit_ref, ki_ref, q_ref, k_ref, v_ref, o_ref, sc_ref, key_ref,
                       *, n_sel, ih, nh, kvh, qb0):
    l = ki_ref.shape[1]
    qb = qb0 + pl.program_id(1)
    lhs = ki_ref[0].astype(BF16)
    for hp in range(ih // 2):
        c0 = 2 * hp * INDEX_DIM
        c1 = c0 + INDEX_DIM
        rhs = jnp.concatenate([qi_ref[:, c0:c1], qi_ref[:, c1:c1 + INDEX_DIM]], axis=0)
        rel = jnp.maximum(lax.dot_general(lhs, rhs, NT_DIMS, preferred_element_type=F32), 0.0)
        w0 = wit_ref[0, 0, 2 * hp:2 * hp + 1, :]
        w1 = wit_ref[0, 0, 2 * hp + 1:2 * hp + 2, :]
        part = rel[:, :Q_BLOCK] * w0 + rel[:, Q_BLOCK:] * w1
        if hp == 0:
            sc_ref[...] = part
        else:
            sc_ref[...] += part
    s_idx = lax.broadcasted_iota(jnp.int32, (l, Q_BLOCK), 0)
    q_pos = qb * Q_BLOCK + lax.broadcasted_iota(jnp.int32, (l, Q_BLOCK), 1)
    causal = s_idx <= q_pos
    sel = _topk_mask(jnp.where(causal, sc_ref[...], -jnp.inf), key_ref, n_sel)
    bias = jnp.where(jnp.logical_and(sel, causal), 0.0, -jnp.inf).T
    c_exp = float(HEAD_DIM) ** -0.5 * 1.4426950408889634
    rep = nh // kvh
    for g in range(kvh):
        kg = k_ref[0, :, g * HEAD_DIM:(g + 1) * HEAD_DIM].astype(BF16)
        vg = v_ref[0, :, g * HEAD_DIM:(g + 1) * HEAD_DIM].astype(BF16)
        h0 = g * rep
        qg = jnp.concatenate([q_ref[:, (h0 + r) * HEAD_DIM:(h0 + r + 1) * HEAD_DIM] for r in range(rep)], axis=0)
        lg = lax.dot_general(qg, kg, NT_DIMS, preferred_element_type=F32)
        ps, dens = [], []
        for r in range(rep):
            x = lg[r * Q_BLOCK:(r + 1) * Q_BLOCK] + bias
            p = jnp.exp2((x - jnp.max(x, axis=-1, keepdims=True)) * c_exp)
            dens.append(jnp.sum(p, axis=-1, keepdims=True))
            ps.append(p.astype(BF16))
        og = jnp.dot(jnp.concatenate(ps, axis=0), vg, preferred_element_type=F32)
        for r in range(rep):
            o = og[r * Q_BLOCK:(r + 1) * Q_BLOCK] / dens[r]
            o_ref[0, :, (h0 + r) * HEAD_DIM:(h0 + r + 1) * HEAD_DIM] = o.astype(o_ref.dtype)


def _causal_buckets(nq):
    for n in range(MAX_CAUSAL_BUCKETS, 0, -1):
        if nq % n == 0 and (nq // n * Q_BLOCK) % PREFIX_CHUNK == 0:
            return n
    return 1


def _dsa_prompt(qi, wit, ki, q, k, v, b, s, n_sel, ih, nh, kvh):
    nq = s // Q_BLOCK
    nbk = _causal_buckets(nq)
    per = nq // nbk
    per_b = lambda bb, qq: (bb, 0, 0)
    outs = []
    for u in range(nbk):
        le = (u + 1) * per * Q_BLOCK
        row = functools.partial(lambda bb, qq, u: (bb * nq + u * per + qq, 0), u=u)
        wrow = functools.partial(lambda bb, qq, u: (bb, u * per + qq, 0, 0), u=u)
        outs.append(pl.pallas_call(
            functools.partial(_dsa_prompt_kernel, n_sel=n_sel, ih=ih, nh=nh, kvh=kvh, qb0=u * per),
            grid=(b, per),
            in_specs=[pl.BlockSpec((Q_BLOCK, ih * INDEX_DIM), row),
                      pl.BlockSpec((1, 1, ih, Q_BLOCK), wrow),
                      pl.BlockSpec((1, le, INDEX_DIM), per_b),
                      pl.BlockSpec((Q_BLOCK, nh * HEAD_DIM), row),
                      pl.BlockSpec((1, le, kvh * HEAD_DIM), per_b), pl.BlockSpec((1, le, kvh * HEAD_DIM), per_b)],
            out_specs=pl.BlockSpec((1, Q_BLOCK, nh * HEAD_DIM), lambda bb, qq: (bb, qq, 0)),
            out_shape=jax.ShapeDtypeStruct((b, per * Q_BLOCK, nh * HEAD_DIM), BF16),
            scratch_shapes=[pltpu.VMEM((le, Q_BLOCK), F32), pltpu.VMEM((le, Q_BLOCK), jnp.int32)],
            compiler_params=_cparams(("parallel", "arbitrary")),
            name="dsa_prompt",
        )(qi, wit, ki, q, k, v))
    return jnp.concatenate(outs, axis=1).reshape(b * s, nh * HEAD_DIM)


def _dsa_score_kernel(pt_ref, qi_ref, wi_ref, kin_ref, *rest, n_pages):
    pages = rest[:n_pages]
    past_ref, new_ref = rest[n_pages:]
    qi = qi_ref[0]
    wcol = wi_ref[0]
    for p in range(n_pages):
        rel = lax.dot_general(qi, pages[p][0].astype(BF16), NT_DIMS, preferred_element_type=F32)
        past_ref[0, p] = jnp.sum(jnp.maximum(rel, 0.0) * wcol, axis=0, keepdims=True)
    kin = kin_ref[0].astype(BF16).astype(F32)
    rel_new = jnp.maximum(jnp.sum(qi.astype(F32) * kin, axis=-1, keepdims=True), 0.0)
    new = jnp.sum(rel_new * wcol, axis=0, keepdims=True)
    new_ref[0] = jnp.broadcast_to(new, (1, LANE))


def _dsa_scores(page_table, qi, wi, ki_new, pool_idx):
    db, n_pages = page_table.shape
    ih = qi.shape[1]
    psz = pool_idx.shape[1]
    per_b = lambda bb, pt: (bb, 0, 0)
    page_specs = [pl.BlockSpec((1, psz, INDEX_DIM), functools.partial(lambda bb, pt, p: (pt[bb, p], 0, 0), p=p))
                  for p in range(n_pages)]
    grid_spec = pltpu.PrefetchScalarGridSpec(
        num_scalar_prefetch=1,
        grid=(db,),
        in_specs=[pl.BlockSpec((1, ih, INDEX_DIM), per_b),
                  pl.BlockSpec((1, ih, 1), per_b), pl.BlockSpec((1, 1, INDEX_DIM), per_b)] + page_specs,
        out_specs=[pl.BlockSpec((1, n_pages, 1, psz), lambda bb, pt: (bb, 0, 0, 0)),
                   pl.BlockSpec((1, 1, LANE), per_b)],
    )
    return pl.pallas_call(
        functools.partial(_dsa_score_kernel, n_pages=n_pages),
        grid_spec=grid_spec,
        out_shape=[jax.ShapeDtypeStruct((db, n_pages, 1, psz), F32), jax.ShapeDtypeStruct((db, 1, LANE), F32)],
        compiler_params=_cparams(("arbitrary",)),
        name="dsa_sample_scores",
    )(page_table, qi, wi, ki_new, *([pool_idx] * n_pages))


def _mask_kernel(sc_ref, m_ref, key_ref, *, n_sel):
    m_ref[...] = jnp.where(_topk_mask(sc_ref[...], key_ref, n_sel), 1.0, 0.0)


def _select_mask(score_t, n_sel):
    l, n = score_t.shape
    blk = pl.BlockSpec((l, LANE), lambda i: (0, i))
    return pl.pallas_call(
        functools.partial(_mask_kernel, n_sel=n_sel),
        grid=(n // LANE,),
        in_specs=[blk],
        out_specs=blk,
        out_shape=jax.ShapeDtypeStruct((l, n), F32),
        scratch_shapes=[pltpu.VMEM((l, LANE), jnp.int32)],
        compiler_params=_cparams(("parallel",)),
        name="select_mask",
    )(score_t)


def _dsa_sample_kernel(pt_ref, q_ref, bias_ref, kn_ref, vn_ref, *rest, n_pages, nh, kvh):
    kpages = rest[:n_pages]
    vpages = rest[n_pages:2 * n_pages]
    o_ref = rest[2 * n_pages]
    psz = kpages[0].shape[1]
    kvd = kvh * HEAD_DIM
    rep = nh // kvh
    scale = float(HEAD_DIM) ** -0.5
    q = q_ref[0]
    grp = lax.broadcasted_iota(jnp.int32, (nh, kvd), 1) // HEAD_DIM
    own = grp == lax.broadcasted_iota(jnp.int32, (nh, kvd), 0) // rep
    qbd = jnp.where(own, jnp.concatenate([q] * kvh, axis=1), jnp.zeros((), BF16))
    lgs = []
    for p in range(n_pages):
        lg = lax.dot_general(qbd, kpages[p][0].astype(BF16), NT_DIMS, preferred_element_type=F32)
        lgs.append(lg * scale + bias_ref[0, :, p * psz:(p + 1) * psz])
    first = lax.broadcasted_iota(jnp.int32, (LANE, kvd), 0) == 0
    kn8 = jnp.where(first, jnp.broadcast_to(kn_ref[0], (LANE, kvd)), 0.0).astype(BF16)
    vn8 = jnp.where(first, jnp.broadcast_to(vn_ref[0], (LANE, kvd)), 0.0).astype(BF16)
    lg_new = lax.dot_general(qbd, kn8, NT_DIMS, preferred_element_type=F32) * scale
    lg_new = lg_new + bias_ref[0, :, n_pages * psz:n_pages * psz + LANE]
    m = jnp.max(lg_new, axis=-1, keepdims=True)
    for lg in lgs:
        m = jnp.maximum(m, jnp.max(lg, axis=-1, keepdims=True))
    p_new = jnp.exp(lg_new - m)
    den = jnp.sum(p_new, axis=-1, keepdims=True)
    acc = jnp.dot(p_new.astype(BF16), vn8, preferred_element_type=F32)
    for p in range(n_pages):
        pp = jnp.exp(lgs[p] - m)
        den = den + jnp.sum(pp, axis=-1, keepdims=True)
        acc = acc + jnp.dot(pp.astype(BF16), vpages[p][0].astype(BF16), preferred_element_type=F32)
    acc = jnp.where(own, acc, 0.0) / den
    out = acc[:, 0:HEAD_DIM]
    for g in range(1, kvh):
        out = out + acc[:, g * HEAD_DIM:(g + 1) * HEAD_DIM]
    o_ref[0] = out.astype(o_ref.dtype)


def _dsa_sample(page_table, q, bias, k_new, v_new, pool_k, pool_v, nh, kvh):
    db, n_pages = page_table.shape
    psz, kvd = pool_k.shape[1], pool_k.shape[2]
    lp = bias.shape[2]
    per_b = lambda bb, pt: (bb, 0, 0)
    page_specs = [pl.BlockSpec((1, psz, kvd), functools.partial(lambda bb, pt, p: (pt[bb, p], 0, 0), p=p))
                  for p in range(n_pages)]
    grid_spec = pltpu.PrefetchScalarGridSpec(
        num_scalar_prefetch=1,
        grid=(db,),
        in_specs=[pl.BlockSpec((1, nh, HEAD_DIM), per_b), pl.BlockSpec((1, 1, lp), per_b),
                  pl.BlockSpec((1, 1, kvd), per_b), pl.BlockSpec((1, 1, kvd), per_b)] + page_specs + page_specs,
        out_specs=pl.BlockSpec((1, nh, HEAD_DIM), per_b),
    )
    return pl.pallas_call(
        functools.partial(_dsa_sample_kernel, n_pages=n_pages, nh=nh, kvh=kvh),
        grid_spec=grid_spec,
        out_shape=jax.ShapeDtypeStruct((db, nh, HEAD_DIM), BF16),
        compiler_params=_cparams(("arbitrary",)),
        name="dsa_sample_attend",
    )(page_table, q, bias, k_new, v_new, *([pool_k] * n_pages), *([pool_v] * n_pages))


def _cross_kernel(q_ref, mk_ref, mv_ref, o_ref, *, mh, tq):
    scale = float(HEAD_DIM) ** -0.5
    for h in range(mh):
        cs = slice(h * HEAD_DIM, (h + 1) * HEAD_DIM)
        qh = q_ref[0, :, cs].astype(BF16)
        if tq < SUBLANE:
            qh = jnp.broadcast_to(qh[0:1], (SUBLANE, HEAD_DIM))
        kh = mk_ref[0, :, cs].astype(BF16)
        vh = mv_ref[0, :, cs].astype(BF16)
        lg = lax.dot_general(qh, kh, NT_DIMS, preferred_element_type=F32) * scale
        p = jnp.exp(lg - jnp.max(lg, axis=-1, keepdims=True))
        den = jnp.sum(p, axis=-1, keepdims=True)
        o = jnp.dot(p.astype(BF16), vh, preferred_element_type=F32) / den
        o_ref[0, :, cs] = o[0:tq].astype(o_ref.dtype)


def _cross_attend(q, mk, mv, mh):
    b, t, md = q.shape
    nm = mk.shape[1]
    tq = _tile(t, 512, SUBLANE)
    return pl.pallas_call(
        functools.partial(_cross_kernel, mh=mh, tq=tq),
        grid=(b, t // tq),
        in_specs=[pl.BlockSpec((1, tq, md), lambda bb, i: (bb, i, 0)),
                  pl.BlockSpec((1, nm, md), lambda bb, i: (bb, 0, 0)),
                  pl.BlockSpec((1, nm, md), lambda bb, i: (bb, 0, 0))],
        out_specs=pl.BlockSpec((1, tq, md), lambda bb, i: (bb, i, 0)),
        out_shape=jax.ShapeDtypeStruct((b, t, md), BF16),
        compiler_params=_cparams(("parallel", "parallel")),
        name="cross_attend",
    )(q, mk, mv)


def _to_chains(x, b, t, h):
    nc = b * h
    y = x.reshape(b, t, h, RWKV_HEAD_DIM).transpose(1, 3, 0, 2).reshape(t, RWKV_HEAD_DIM, nc)
    return jnp.pad(y, ((0, 0), (0, 0), (0, _round_up(nc, LANE) - nc)))


def _param_chains(p, b, h):
    nc = b * h
    y = jnp.tile(p.reshape(h, RWKV_HEAD_DIM).T, (1, b))
    return jnp.pad(y, ((0, 0), (0, _round_up(nc, LANE) - nc)))


def _rope_tables(pos):
    half = HEAD_DIM // 2
    inv = jnp.power(ROPE_THETA, -jnp.arange(half, dtype=F32) / half)
    ang = pos.astype(F32)[:, None] * inv[None, :]
    cos, sin = jnp.cos(ang), jnp.sin(ang)
    return jnp.concatenate([cos, cos], axis=-1), jnp.concatenate([-sin, sin], axis=-1)


def _mixer(x, b, t, pos, shift_prev, s0, lw, dims, attend):
    d, rd, lwd, lad, lgp, rp, rpp, nh, kvh, ih = dims
    h = rd // RWKV_HEAD_DIM
    m = b * t
    hn = _rmsnorm(x, lw["norm_mix"], BF16)
    gates = _mm(hn, lw["w_gates"])
    p = _mm(hn, lw["w_rwkv"])
    att = _mm(hn, lw["w_attn"])

    p3 = p.reshape(b, t, rpp)
    prev = jnp.pad(shift_prev, ((0, 0), (0, rpp - rp)))
    r, w, k, v, a, g = _rwkv_prep(p, prev, t, lw["mu"], lw["w0"], lw["w2"], lw["a0"], lw["a2"], lw["g2"],
                                  rd, lwd, lad, lgp)
    nc = b * h
    s0c = jnp.pad(s0.reshape(nc, RWKV_HEAD_DIM, RWKV_HEAD_DIM).transpose(1, 2, 0),
                  ((0, 0), (0, 0), (0, _round_up(nc, LANE) - nc)))
    yc, sc = _rwkv_scan(*[_to_chains(z, b, t, h) for z in (r, w, k, v, a)],
                        _param_chains(lw["kk"], b, h), _param_chains(lw["ka"], b, h),
                        _param_chains(lw["rk"], b, h), _param_chains(lw["ln_g"], b, h),
                        _param_chains(lw["ln_b"], b, h), s0c)
    yr = yc[:, :, :nc].reshape(t, RWKV_HEAD_DIM, b, h).transpose(2, 0, 3, 1).reshape(m, rd)
    s_fin = sc[:, :, :nc].transpose(2, 0, 1).reshape(b, h, RWKV_HEAD_DIM, RWKV_HEAD_DIM)
    shift_new = p3[:, -1, :rp]

    cos2, sin2 = _rope_tables(pos)
    cos2 = jnp.broadcast_to(cos2[None], (b, t, LANE)).reshape(m, LANE)
    sin2 = jnp.broadcast_to(sin2[None], (b, t, LANE)).reshape(m, LANE)
    q, kr, vv, qi, ki, wi = _attn_prep(att, cos2, sin2, lw["idx_ln_g"], lw["idx_ln_b"], nh, kvh, ih)
    ya = attend(q, qi, wi, kr, vv, ki)

    merged = _merge(_gate_mul(yr, g), ya, gates, lw["w_br_rwkv"], lw["w_br_attn"])
    x1 = _mm_resid(merged, lw["w_out"], x)
    return x1, kr, vv, ki, s_fin, shift_new


def _attend_prompt(q, qi, wi, kr, vv, ki, *, b, s, nh, kvh, ih):
    n_sel = min(TOPK_MAX, s // 4)
    nq = s // Q_BLOCK
    wit = wi[:, :ih].reshape(b, nq, Q_BLOCK, ih).transpose(0, 1, 3, 2)
    return _dsa_prompt(qi, wit, ki.reshape(b, s, INDEX_DIM), q,
                       kr.reshape(b, s, kvh * HEAD_DIM), vv.reshape(b, s, kvh * HEAD_DIM),
                       b, s, n_sel, ih, nh, kvh)


def _attend_sample(q, qi, wi, kr, vv, ki, *, pool_k, pool_v, pool_idx, page_table, nh, kvh, ih):
    db, n_pages = page_table.shape
    psz = pool_idx.shape[1]
    past = n_pages * psz
    n_sel = min(TOPK_MAX, (past + 1) // 4)
    kvd = kvh * HEAD_DIM
    sc_past, sc_new = _dsa_scores(page_table, qi.reshape(db, ih, INDEX_DIM),
                                  wi[:, :ih].reshape(db, ih, 1), ki.reshape(db, 1, INDEX_DIM), pool_idx)
    lp = _round_up(past + 1, PREFIX_CHUNK)
    dbp = _round_up(db, LANE)
    score = jnp.concatenate([sc_past.reshape(db, past), sc_new[:, 0, :1]], axis=1)
    score_t = jnp.pad(score.T, ((0, lp - past - 1), (0, dbp - db)), constant_values=-jnp.inf)
    mask = _select_mask(score_t, n_sel)[:, :db].T
    bias = jnp.where(mask > 0.5, 0.0, -jnp.inf).astype(F32).reshape(db, 1, lp)
    o = _dsa_sample(page_table, q.reshape(db, nh, HEAD_DIM), bias, kr.reshape(db, 1, kvd),
                    vv.reshape(db, 1, kvd), pool_k.reshape(-1, psz, kvd), pool_v.reshape(-1, psz, kvd), nh, kvh)
    return o.reshape(db, nh * HEAD_DIM)


def _cross_and_ffn(x, b, t, mk, mv, lw, mh):
    hc = _rmsnorm(x, lw["norm_cross"], BF16)
    qm = _mm(hc, lw["w_q_mem"])
    o = _cross_attend(qm.reshape(b, t, -1), mk, mv, mh)
    x2 = _mm_resid(o.reshape(b * t, -1), lw["w_o_mem"], x)
    hf = _rmsnorm(x2, lw["norm_ffn"], BF16)
    act = _swiglu(hf, lw["w_up"])
    return _mm_resid(act, lw["w_down"], x2, tn_pref=512, tk_pref=5504)


def kernel(x_prompt, mem_prompt, x_sample, cache_k, cache_v, cache_idx_k, cache_mem_k, cache_mem_v, state_rwkv, state_shift, page_table, norm_mix, w_in, rwkv_mu, rwkv_w0, rwkv_w2, rwkv_a0, rwkv_a2, rwkv_g2, rwkv_kk, rwkv_ka, rwkv_rk, rwkv_ln_g, rwkv_ln_b, idx_ln_g, idx_ln_b, w_br_rwkv, w_br_attn, w_out, norm_cross, norm_mem, w_q_mem, w_k_mem, w_v_mem, w_o_mem, norm_ffn, w_up, w_down, final_norm):
    b, s, d = x_prompt.shape
    db, ts, _ = x_sample.shape
    assert ts == 1, "the sample group decodes one token per sequence"
    depth = w_in.shape[0]
    in_proj = w_in.shape[2]
    rp = rwkv_mu.shape[1]
    lwd, rd = rwkv_w2.shape[1:]
    lad = rwkv_a2.shape[1]
    lg = rwkv_g2.shape[1]
    lgp = _round_up(lg, LANE)
    rpp = rp - lg + lgp
    ad = w_br_attn.shape[1]
    nh = ad // HEAD_DIM
    kvh = cache_k.shape[3]
    kvd = kvh * HEAD_DIM
    ih = (in_proj - 2 * d - rp - ad - 2 * kvd - INDEX_DIM) // (INDEX_DIM + 1)
    mh = cache_mem_k.shape[3]
    n_mem = mem_prompt.shape[1]
    past = page_table.shape[1] * cache_k.shape[2]
    assert rp == 3 * rd + lwd + lad + lg and s % Q_BLOCK == 0 and ih % 2 == 0
    dims = (d, rd, lwd, lad, lgp, rp, rpp, nh, kvh, ih)
    n_pool, psz = cache_k.shape[1:3]
    pool_k = cache_k.reshape(depth * n_pool, psz, kvd)
    pool_v = cache_v.reshape(depth * n_pool, psz, kvd)
    pool_idx = cache_idx_k.reshape(depth * n_pool, psz, INDEX_DIM)
    mem_k = cache_mem_k.reshape(depth * db, n_mem, mh * HEAD_DIM)
    mem_v = cache_mem_v.reshape(depth * db, n_mem, mh * HEAD_DIM)

    xp = x_prompt.reshape(b * s, d)
    xs = x_sample.reshape(db, d)
    pos_p = jnp.arange(s, dtype=jnp.int32)
    pos_s = past + jnp.arange(ts, dtype=jnp.int32)
    outs = [[] for _ in range(12)]
    for l in range(depth):
        wl = w_in[l]
        o_att = 2 * d + rp
        n_att = ad + 2 * kvd + ih * INDEX_DIM + INDEX_DIM
        lw = {
            "norm_mix": norm_mix[l],
            "w_gates": wl[:, :2 * d].astype(BF16),
            "w_rwkv": jnp.pad(wl[:, 2 * d:o_att], ((0, 0), (0, rpp - rp))).astype(BF16),
            "w_attn": jnp.pad(wl[:, o_att:], ((0, 0), (0, LANE - ih))).astype(BF16),
            "mu": jnp.pad(rwkv_mu[l], (0, rpp - rp)).reshape(1, rpp),
            "w0": rwkv_w0[l].reshape(1, rd), "w2": rwkv_w2[l].astype(BF16),
            "a0": rwkv_a0[l].reshape(1, rd), "a2": rwkv_a2[l].astype(BF16),
            "g2": jnp.pad(rwkv_g2[l], ((0, lgp - lg), (0, 0))).astype(BF16),
            "kk": rwkv_kk[l], "ka": rwkv_ka[l], "rk": rwkv_rk[l].reshape(-1),
            "ln_g": rwkv_ln_g[l], "ln_b": rwkv_ln_b[l],
            "idx_ln_g": idx_ln_g[l].reshape(1, INDEX_DIM), "idx_ln_b": idx_ln_b[l].reshape(1, INDEX_DIM),
            "w_br_rwkv": w_br_rwkv[l].astype(BF16), "w_br_attn": w_br_attn[l].astype(BF16),
            "w_out": w_out[l].astype(BF16),
            "norm_cross": norm_cross[l], "norm_ffn": norm_ffn[l],
            "w_q_mem": w_q_mem[l].astype(BF16), "w_o_mem": w_o_mem[l].astype(BF16),
            "w_up": w_up[l].astype(BF16), "w_down": w_down[l].astype(BF16),
        }
        assert n_att + ih == in_proj - o_att
        xp, kp, vp, kip, sfp, shp = _mixer(
            xp, b, s, pos_p, jnp.zeros((b, rp), F32),
            jnp.zeros((b, rd // RWKV_HEAD_DIM, RWKV_HEAD_DIM, RWKV_HEAD_DIM), F32), lw, dims,
            functools.partial(_attend_prompt, b=b, s=s, nh=nh, kvh=kvh, ih=ih))
        mn = _rmsnorm(mem_prompt.reshape(b * n_mem, d), norm_mem[l], BF16)
        mkp = _mm(mn, w_k_mem[l].astype(BF16)).reshape(b, n_mem, mh * HEAD_DIM)
        mvp = _mm(mn, w_v_mem[l].astype(BF16)).reshape(b, n_mem, mh * HEAD_DIM)
        xp = _cross_and_ffn(xp, b, s, mkp, mvp, lw, mh)
        xs, ks_, vs_, kis, sfs, shs = _mixer(
            xs, db, ts, pos_s, state_shift[l], state_rwkv[l], lw, dims,
            functools.partial(_attend_sample, pool_k=pool_k, pool_v=pool_v, pool_idx=pool_idx,
                              page_table=page_table + l * n_pool, nh=nh, kvh=kvh, ih=ih))
        xs = _cross_and_ffn(xs, db, ts, mem_k[l * db:(l + 1) * db], mem_v[l * db:(l + 1) * db], lw, mh)
        vals = (kp.reshape(b, s, kvh, HEAD_DIM), vp.reshape(b, s, kvh, HEAD_DIM), kip.reshape(b, s, INDEX_DIM),
                mkp.reshape(b, n_mem, mh, HEAD_DIM), mvp.reshape(b, n_mem, mh, HEAD_DIM), sfp, shp,
                ks_.reshape(db, ts, kvh, HEAD_DIM), vs_.reshape(db, ts, kvh, HEAD_DIM),
                kis.reshape(db, ts, INDEX_DIM), sfs, shs)
        for lst, val in zip(outs, vals):
            lst.append(val)
    y_prompt = _rmsnorm(xp, final_norm, F32).reshape(b, s, d)
    y_sample = _rmsnorm(xs, final_norm, F32).reshape(db, ts, d)
    return (y_prompt, y_sample) + tuple(jnp.stack(o) for o in outs)
```

```python
import functools

import jax
import jax.numpy as jnp
from jax import lax
from jax.experimental import pallas as pl
from jax.experimental.pallas import tpu as pltpu

F32 = jnp.float32
BF16 = jnp.bfloat16
LANE = 128
SUBLANE = 8
VMEM_LIMIT = 56 * 1024 * 1024

RWKV_HEAD_DIM = 64
HEAD_DIM = 128
INDEX_DIM = 128
TOPK_MAX = 256
Q_BLOCK = 128
ROPE_THETA = 10000.0
NORM_EPS = 1e-6
RWKV_LN_EPS = 64e-5
IDX_LN_EPS = 1e-6
PREFIX_CHUNK = 256
MAX_CAUSAL_BUCKETS = 4
PREP_STEPS = 32
NT_DIMS = (((1,), (1,)), ((), ()))


def _cparams(sem):
    return pltpu.CompilerParams(dimension_semantics=sem, vmem_limit_bytes=VMEM_LIMIT)


def _round_up(n, m):
    return (n + m - 1) // m * m


def _tile(n, pref, unit):
    best = None
    t = unit
    while t <= min(n, pref):
        if n % t == 0:
            best = t
        t += unit
    return best if best is not None else n


def _rmsnorm_kernel(x_ref, g_ref, o_ref):
    x = x_ref[...]
    ms = jnp.mean(x * x, axis=-1, keepdims=True)
    o_ref[...] = (x * lax.rsqrt(ms + NORM_EPS) * g_ref[...]).astype(o_ref.dtype)


def _rmsnorm(x, g, out_dtype):
    m, d = x.shape
    tm = _tile(m, 256, SUBLANE)
    return pl.pallas_call(
        _rmsnorm_kernel,
        grid=(m // tm,),
        in_specs=[pl.BlockSpec((tm, d), lambda i: (i, 0)), pl.BlockSpec((1, d), lambda i: (0, 0))],
        out_specs=pl.BlockSpec((tm, d), lambda i: (i, 0)),
        out_shape=jax.ShapeDtypeStruct((m, d), out_dtype),
        compiler_params=_cparams(("parallel",)),
        name="rmsnorm",
    )(x, g.reshape(1, d))


def _mm_kernel(x_ref, w_ref, o_ref):
    o_ref[...] = jnp.dot(x_ref[...], w_ref[...], preferred_element_type=F32).astype(o_ref.dtype)


def _mm(x, w, out_dtype=F32, tm_pref=1024, tn_pref=512):
    m, k = x.shape
    n = w.shape[1]
    tm = _tile(m, tm_pref, SUBLANE)
    tn = _tile(n, tn_pref, LANE)
    return pl.pallas_call(
        _mm_kernel,
        grid=(m // tm, n // tn),
        in_specs=[pl.BlockSpec((tm, k), lambda i, j: (i, 0)), pl.BlockSpec((k, tn), lambda i, j: (0, j))],
        out_specs=pl.BlockSpec((tm, tn), lambda i, j: (i, j)),
        out_shape=jax.ShapeDtypeStruct((m, n), out_dtype),
        compiler_params=_cparams(("parallel", "parallel")),
        name="matmul",
    )(x, w)


def _mm_resid_kernel(x_ref, w_ref, r_ref, o_ref, acc_ref, *, nk):
    kk = pl.program_id(2)

    @pl.when(kk == 0)
    def _():
        acc_ref[...] = r_ref[...]

    acc_ref[...] += jnp.dot(x_ref[...], w_ref[...], preferred_element_type=F32)

    @pl.when(kk == nk - 1)
    def _():
        o_ref[...] = acc_ref[...]


def _mm_resid(x, w, resid, tm_pref=1024, tn_pref=1024, tk_pref=2048):
    m, k = x.shape
    n = w.shape[1]
    tm = _tile(m, tm_pref, SUBLANE)
    tn = _tile(n, tn_pref, LANE)
    tk = _tile(k, tk_pref, LANE)
    nk = k // tk
    return pl.pallas_call(
        functools.partial(_mm_resid_kernel, nk=nk),
        grid=(m // tm, n // tn, nk),
        in_specs=[pl.BlockSpec((tm, tk), lambda i, j, kk: (i, kk)),
                  pl.BlockSpec((tk, tn), lambda i, j, kk: (kk, j)),
                  pl.BlockSpec((tm, tn), lambda i, j, kk: (i, j))],
        out_specs=pl.BlockSpec((tm, tn), lambda i, j, kk: (i, j)),
        out_shape=jax.ShapeDtypeStruct((m, n), F32),
        scratch_shapes=[pltpu.VMEM((tm, tn), F32)],
        compiler_params=_cparams(("parallel", "parallel", "arbitrary")),
        name="matmul_resid",
    )(x, w, resid)


def _swiglu_kernel(x_ref, wg_ref, wu_ref, o_ref):
    x = x_ref[...]
    g = jnp.dot(x, wg_ref[...], preferred_element_type=F32)
    u = jnp.dot(x, wu_ref[...], preferred_element_type=F32)
    o_ref[...] = (g * jax.nn.sigmoid(g) * u).astype(o_ref.dtype)


def _swiglu(x, w_up, tm_pref=1024, tn_pref=256):
    m, k = x.shape
    f = w_up.shape[1] // 2
    tm = _tile(m, tm_pref, SUBLANE)
    tn = _tile(f, tn_pref, LANE)
    nb = f // tn
    return pl.pallas_call(
        _swiglu_kernel,
        grid=(m // tm, nb),
        in_specs=[pl.BlockSpec((tm, k), lambda i, j: (i, 0)),
                  pl.BlockSpec((k, tn), lambda i, j: (0, j)),
                  pl.BlockSpec((k, tn), lambda i, j: (0, j + nb))],
        out_specs=pl.BlockSpec((tm, tn), lambda i, j: (i, j)),
        out_shape=jax.ShapeDtypeStruct((m, f), BF16),
        compiler_params=_cparams(("parallel", "parallel")),
        name="swiglu",
    )(x, w_up, w_up)


def _gate_mul_kernel(y_ref, g_ref, o_ref):
    o_ref[...] = (y_ref[...] * g_ref[...]).astype(o_ref.dtype)


def _gate_mul(y, g):
    m, n = y.shape
    tm = _tile(m, 256, SUBLANE)
    blk = pl.BlockSpec((tm, n), lambda i: (i, 0))
    return pl.pallas_call(
        _gate_mul_kernel,
        grid=(m // tm,),
        in_specs=[blk, blk],
        out_specs=blk,
        out_shape=jax.ShapeDtypeStruct((m, n), BF16),
        compiler_params=_cparams(("parallel",)),
        name="gate_mul",
    )(y, g)


def _merge_kernel(yr_ref, ya_ref, gr_ref, ga_ref, wr_ref, wa_ref, o_ref):
    br = jnp.dot(yr_ref[...], wr_ref[...], preferred_element_type=F32)
    ba = jnp.dot(ya_ref[...], wa_ref[...], preferred_element_type=F32)
    o_ref[...] = (jax.nn.sigmoid(gr_ref[...]) * br + jax.nn.sigmoid(ga_ref[...]) * ba).astype(o_ref.dtype)


def _merge(yr, ya, gates, w_br_r, w_br_a, tm_pref=1024, tn_pref=512):
    m, rd = yr.shape
    ad = ya.shape[1]
    d = w_br_r.shape[1]
    tm = _tile(m, tm_pref, SUBLANE)
    tn = _tile(d, tn_pref, LANE)
    nb = d // tn
    return pl.pallas_call(
        _merge_kernel,
        grid=(m // tm, nb),
        in_specs=[pl.BlockSpec((tm, rd), lambda i, j: (i, 0)),
                  pl.BlockSpec((tm, ad), lambda i, j: (i, 0)),
                  pl.BlockSpec((tm, tn), lambda i, j: (i, j)),
                  pl.BlockSpec((tm, tn), lambda i, j: (i, j + nb)),
                  pl.BlockSpec((rd, tn), lambda i, j: (0, j)),
                  pl.BlockSpec((ad, tn), lambda i, j: (0, j))],
        out_specs=pl.BlockSpec((tm, tn), lambda i, j: (i, j)),
        out_shape=jax.ShapeDtypeStruct((m, d), BF16),
        compiler_params=_cparams(("parallel", "parallel")),
        name="merge",
    )(yr, ya, gates, gates, w_br_r, w_br_a)


def _lane_segment(h):
    return lax.broadcasted_iota(jnp.int32, (SUBLANE, LANE), 1) // h


def _to_chain_tiles(x, out_ref, nbatch, tt, h):
    per = LANE // h
    seg = _lane_segment(h)
    for c in range(nbatch // per):
        for g in range(tt // SUBLANE):
            for col in range(RWKV_HEAD_DIM // per):
                tiles = [x[(c * per + q) * tt + g * SUBLANE:(c * per + q) * tt + (g + 1) * SUBLANE,
                           col * LANE:(col + 1) * LANE] for q in range(per)]
                for jj in range(per):
                    acc = None
                    for q in range(per):
                        sh = ((q - jj) % per) * h
                        t = tiles[q] if sh == 0 else pltpu.roll(tiles[q], sh, axis=1)
                        acc = t if acc is None else jnp.where(seg == q, t, acc)
                    n = col * per + jj
                    out_ref[g, n * SUBLANE:(n + 1) * SUBLANE, c * LANE:(c + 1) * LANE] = acc


def _rwkv_prep_kernel(p_ref, sh_ref, *rest, rd, lw, la, lgp, chain):
    if chain is not None:
        prev_ref, rest = rest[0], rest[1:]
    mu_ref, w0_ref, w2_ref, a0_ref, a2_ref, g2_ref, r_ref, w_ref, k_ref, v_ref, a_ref, g_ref = rest
    if chain is None:
        p = p_ref[...]
        shifted = sh_ref[...]
    else:
        nbatch, tt, h = chain
        first = pl.program_id(0) == 0
        row0 = lax.broadcasted_iota(jnp.int32, (tt, p_ref.shape[2]), 0) == 0
        ps, shs = [], []
        for bb in range(nbatch):
            pb = p_ref[bb]
            above = jnp.where(first, prev_ref[bb], sh_ref[bb, SUBLANE - 1:SUBLANE, :])
            ps.append(pb)
            shs.append(jnp.where(row0, above, pltpu.roll(pb, 1, axis=0)))
        p = jnp.concatenate(ps, axis=0)
        shifted = jnp.concatenate(shs, axis=0)
    xm = p + (shifted - p) * mu_ref[...]
    o = 0
    xr = xm[:, o:o + rd]; o += rd
    xw = xm[:, o:o + lw]; o += lw
    xk = xm[:, o:o + rd]; o += rd
    xv = xm[:, o:o + rd]; o += rd
    xa = xm[:, o:o + la]; o += la
    xg = xm[:, o:o + lgp]
    z = w0_ref[...] + jnp.dot(jnp.tanh(xw).astype(BF16), w2_ref[...], preferred_element_type=F32)
    sp = jnp.maximum(-z, 0.0) + jnp.log(1.0 + jnp.exp(-jnp.abs(z)))
    w_raw = -sp - 0.5
    decay = jnp.exp(-jnp.exp(w_raw))
    a = jax.nn.sigmoid(a0_ref[...] + jnp.dot(xa.astype(BF16), a2_ref[...], preferred_element_type=F32))
    g = jnp.dot(jax.nn.sigmoid(xg).astype(BF16), g2_ref[...], preferred_element_type=F32)
    if chain is None:
        r_ref[...] = xr
        w_ref[...] = decay
        k_ref[...] = xk
        v_ref[...] = xv
        a_ref[...] = a
        g_ref[...] = g
    else:
        for x, ref in ((xr, r_ref), (decay, w_ref), (xk, k_ref), (xv, v_ref), (a, a_ref)):
            _to_chain_tiles(x, ref, nbatch, tt, h)
        for bb in range(nbatch):
            g_ref[bb] = g[bb * tt:(bb + 1) * tt]


def _chain_kernels_fit(b, t, h):
    return LANE % h == 0 and b % (LANE // h) == 0 and t % PREP_STEPS == 0


def _rwkv_prep(p, prev, b, t, mu, w0, w2, a0, a2, g2, rd, lw, la, lgp):
    m, rpp = p.shape
    h = rd // RWKV_HEAD_DIM
    fix = lambda i: (0, 0)
    par_specs = [pl.BlockSpec((1, rpp), fix), pl.BlockSpec((1, rd), fix), pl.BlockSpec((lw, rd), fix),
                 pl.BlockSpec((1, rd), fix), pl.BlockSpec((la, rd), fix), pl.BlockSpec((lgp, rd), fix)]
    if not _chain_kernels_fit(b, t, h):
        shifted = jnp.concatenate([prev[:, None, :], p.reshape(b, t, rpp)[:, :-1]], axis=1).reshape(m, rpp)
        tm = _tile(m, 128, SUBLANE)
        row = lambda i: (i, 0)
        out = jax.ShapeDtypeStruct((m, rd), F32)
        return pl.pallas_call(
            functools.partial(_rwkv_prep_kernel, rd=rd, lw=lw, la=la, lgp=lgp, chain=None),
            grid=(m // tm,),
            in_specs=[pl.BlockSpec((tm, rpp), row), pl.BlockSpec((tm, rpp), row)] + par_specs,
            out_specs=[pl.BlockSpec((tm, rd), row)] * 6,
            out_shape=[out] * 6,
            compiler_params=_cparams(("parallel",)),
            name="rwkv_prep",
        )(p, shifted, mu, w0, w2, a0, a2, g2)
    tt = PREP_STEPS
    nc = b * h
    p3 = p.reshape(b, t, rpp)
    step_blk = lambda i: (0, i, 0)
    chain_out = jax.ShapeDtypeStruct((t // SUBLANE, RWKV_HEAD_DIM * SUBLANE, nc), F32)
    chain_spec = pl.BlockSpec((tt // SUBLANE, RWKV_HEAD_DIM * SUBLANE, nc), lambda i: (i, 0, 0))
    outs = pl.pallas_call(
        functools.partial(_rwkv_prep_kernel, rd=rd, lw=lw, la=la, lgp=lgp, chain=(b, tt, h)),
        grid=(t // tt,),
        in_specs=[pl.BlockSpec((b, tt, rpp), step_blk),
                  pl.BlockSpec((b, SUBLANE, rpp), lambda i: (0, jnp.maximum(i * (tt // SUBLANE) - 1, 0), 0)),
                  pl.BlockSpec((b, 1, rpp), lambda i: (0, 0, 0))] + par_specs,
        out_specs=[chain_spec] * 5 + [pl.BlockSpec((b, tt, rd), step_blk)],
        out_shape=[chain_out] * 5 + [jax.ShapeDtypeStruct((b, t, rd), F32)],
        compiler_params=_cparams(("parallel",)),
        name="rwkv_prep",
    )(p3, p3, prev.reshape(b, 1, rpp), mu, w0, w2, a0, a2, g2)
    return list(outs[:5]) + [outs[5].reshape(m, rd)]


def _rwkv_scan_kernel(r_ref, w_ref, k_ref, v_ref, a_ref, kkw_ref, ka_ref, rk_ref, lng_ref, lnb_ref,
                      s0_ref, y_ref, s_ref, kk_s, nb_s, kp_s, *, ts):
    n = RWKV_HEAD_DIM

    @pl.when(pl.program_id(1) == 0)
    def _():
        s_ref[...] = s0_ref[...]

    def tile(j):
        return slice(j * ts, (j + 1) * ts)

    ss = None
    for j in range(n):
        kk = k_ref[0, tile(j), :] * kkw_ref[j:j + 1, :]
        kk_s[tile(j), :] = kk
        ss = kk * kk if ss is None else ss + kk * kk
    inv = 1.0 / jnp.maximum(jnp.sqrt(ss), 1e-12)
    bonus = None
    for j in range(n):
        kj = k_ref[0, tile(j), :]
        aj = a_ref[0, tile(j), :]
        kk = kk_s[tile(j), :] * inv
        kk_s[tile(j), :] = kk
        nb_s[tile(j), :] = -(kk * aj)
        kp = kj * (1.0 + (aj - 1.0) * ka_ref[j:j + 1, :])
        kp_s[tile(j), :] = kp
        term = r_ref[0, tile(j), :] * kp * rk_ref[j:j + 1, :]
        bonus = term if bonus is None else bonus + term

    def row(ref, j, tt):
        return ref[j * ts + tt:j * ts + tt + 1, :]

    def row3(ref, j, tt):
        return ref[0, j * ts + tt:j * ts + tt + 1, :]

    lng = lng_ref[...]
    lnb = lnb_ref[...]
    sk = None
    for j in range(n):
        term = s_ref[j] * row(kk_s, j, 0)
        sk = term if sk is None else sk + term
    for tt in range(ts):
        v = v_ref[0, pl.ds(tt, n, stride=ts), :]
        y = None
        sk_next = None
        for j in range(n):
            sn = s_ref[j] * row3(w_ref, j, tt) + sk * row(nb_s, j, tt) + v * row(kp_s, j, tt)
            s_ref[j] = sn
            yt = sn * row3(r_ref, j, tt)
            y = yt if y is None else y + yt
            if tt + 1 < ts:
                st = sn * row(kk_s, j, tt + 1)
                sk_next = st if sk_next is None else sk_next + st
        sk = sk_next
        mu = jnp.mean(y, axis=0, keepdims=True)
        yc = y - mu
        var = jnp.mean(yc * yc, axis=0, keepdims=True)
        yn = yc * lax.rsqrt(var + RWKV_LN_EPS) * lng + lnb
        y_ref[0, pl.ds(tt, n, stride=ts), :] = yn + bonus[tt:tt + 1, :] * v


def _rwkv_scan(r, w, k, v, a, kkw, ka, rk, lng, lnb, s0):
    nt, rows, nc = r.shape
    n = RWKV_HEAD_DIM
    ts = rows // n
    seq = pl.BlockSpec((1, rows, LANE), lambda c, tt: (tt, 0, c))
    par = pl.BlockSpec((n, LANE), lambda c, tt: (0, c))
    st = pl.BlockSpec((n, n, LANE), lambda c, tt: (0, 0, c))
    return pl.pallas_call(
        functools.partial(_rwkv_scan_kernel, ts=ts),
        grid=(nc // LANE, nt),
        in_specs=[seq] * 5 + [par] * 5 + [st],
        out_specs=[seq, st],
        out_shape=[jax.ShapeDtypeStruct((nt, rows, nc), F32), jax.ShapeDtypeStruct((n, n, nc), F32)],
        scratch_shapes=[pltpu.VMEM((rows, LANE), F32)] * 3,
        compiler_params=_cparams(("parallel", "arbitrary")),
        name="rwkv_scan",
    )(r, w, k, v, a, kkw, ka, rk, lng, lnb, s0)


def _rwkv_post_kernel(y_ref, g_ref, o_ref, *, nbatch, tt, h):
    per = LANE // h
    seg = _lane_segment(h)
    pair = 2 * SUBLANE
    for c in range(nbatch // per):
        for q in range(per):
            bb = c * per + q
            for g2 in range(tt // pair):
                for col in range(RWKV_HEAD_DIM // per):
                    halves = []
                    for g in (2 * g2, 2 * g2 + 1):
                        acc = None
                        for ii in range(per):
                            i = col * per + ii
                            tile = y_ref[g, i * SUBLANE:(i + 1) * SUBLANE, c * LANE:(c + 1) * LANE]
                            sh = ((ii - q) % per) * h
                            t = tile if sh == 0 else pltpu.roll(tile, sh, axis=1)
                            acc = t if acc is None else jnp.where(seg == ii, t, acc)
                        halves.append(acc)
                    rows = slice(g2 * pair, (g2 + 1) * pair)
                    cols = slice(col * LANE, (col + 1) * LANE)
                    o_ref[bb, rows, cols] = (jnp.concatenate(halves, axis=0) * g_ref[bb, rows, cols]).astype(o_ref.dtype)


def _rwkv_post(y, g, b, t, h):
    rd = RWKV_HEAD_DIM * h
    tt = PREP_STEPS
    step_blk = lambda i: (0, i, 0)
    return pl.pallas_call(
        functools.partial(_rwkv_post_kernel, nbatch=b, tt=tt, h=h),
        grid=(t // tt,),
        in_specs=[pl.BlockSpec((tt // SUBLANE, RWKV_HEAD_DIM * SUBLANE, b * h), lambda i: (i, 0, 0)),
                  pl.BlockSpec((b, tt, rd), step_blk)],
        out_specs=pl.BlockSpec((b, tt, rd), step_blk),
        out_shape=jax.ShapeDtypeStruct((b, t, rd), BF16),
        compiler_params=_cparams(("parallel",)),
        name="rwkv_post",
    )(y, g.reshape(b, t, rd)).reshape(b * t, rd)


def _rope(x, cos2, sin2):
    return x * cos2 + pltpu.roll(x, HEAD_DIM // 2, axis=1) * sin2


def _attn_prep_kernel(a_ref, cos_ref, sin_ref, lng_ref, lnb_ref,
                      q_ref, k_ref, v_ref, qi_ref, ki_ref, wi_ref, *, nh, kvh, ih, wscale):
    cos2 = cos_ref[...]
    sin2 = sin_ref[...]
    o = 0
    for h in range(nh):
        q_ref[:, h * HEAD_DIM:(h + 1) * HEAD_DIM] = _rope(a_ref[:, o:o + HEAD_DIM], cos2, sin2).astype(BF16)
        o += HEAD_DIM
    for h in range(kvh):
        k_ref[:, h * HEAD_DIM:(h + 1) * HEAD_DIM] = _rope(a_ref[:, o:o + HEAD_DIM], cos2, sin2)
        o += HEAD_DIM
    v_ref[...] = a_ref[:, o:o + kvh * HEAD_DIM]
    o += kvh * HEAD_DIM
    for h in range(ih):
        qi_ref[:, h * INDEX_DIM:(h + 1) * INDEX_DIM] = _rope(a_ref[:, o:o + INDEX_DIM], cos2, sin2).astype(BF16)
        o += INDEX_DIM
    x = a_ref[:, o:o + INDEX_DIM]
    o += INDEX_DIM
    mu = jnp.mean(x, axis=-1, keepdims=True)
    xc = x - mu
    var = jnp.mean(xc * xc, axis=-1, keepdims=True)
    kin = xc * lax.rsqrt(var + IDX_LN_EPS) * lng_ref[...] + lnb_ref[...]
    ki_ref[...] = _rope(kin, cos2, sin2)
    wi_ref[...] = a_ref[:, o:o + LANE] * wscale


def _attn_prep(a, cos2, sin2, lng, lnb, nh, kvh, ih):
    m, na = a.shape
    tm = _tile(m, 128, SUBLANE)
    row = lambda i: (i, 0)
    fix = lambda i: (0, 0)
    ad, kvd, idd = nh * HEAD_DIM, kvh * HEAD_DIM, ih * INDEX_DIM
    wscale = float(ih) ** -0.5 * float(INDEX_DIM) ** -0.5
    return pl.pallas_call(
        functools.partial(_attn_prep_kernel, nh=nh, kvh=kvh, ih=ih, wscale=wscale),
        grid=(m // tm,),
        in_specs=[pl.BlockSpec((tm, na), row), pl.BlockSpec((tm, LANE), row), pl.BlockSpec((tm, LANE), row),
                  pl.BlockSpec((1, LANE), fix), pl.BlockSpec((1, LANE), fix)],
        out_specs=[pl.BlockSpec((tm, ad), row), pl.BlockSpec((tm, kvd), row), pl.BlockSpec((tm, kvd), row),
                   pl.BlockSpec((tm, idd), row),
                   pl.BlockSpec((tm, INDEX_DIM), row), pl.BlockSpec((tm, LANE), row)],
        out_shape=[jax.ShapeDtypeStruct((m, ad), BF16), jax.ShapeDtypeStruct((m, kvd), F32),
                   jax.ShapeDtypeStruct((m, kvd), F32), jax.ShapeDtypeStruct((m, idd), BF16),
                   jax.ShapeDtypeStruct((m, INDEX_DIM), F32),
                   jax.ShapeDtypeStruct((m, LANE), F32)],
        compiler_params=_cparams(("parallel",)),
        name="attn_prep",
    )(a, cos2, sin2, lng, lnb)


def _topk_mask(score, key_ref, n_sel):
    l = score.shape[0]
    bits = lax.bitcast_convert_type(score, jnp.int32)
    key_ref[...] = jnp.where(bits < 0, bits ^ jnp.int32(0x7FFFFFFF), bits)
    sign = jnp.int32(-2 ** 31)
    nf = jnp.float32(n_sel)

    def body(i, ans):
        cand = ans | lax.shift_left(jnp.int32(1), 31 - i)
        cnt = jnp.sum(jnp.where(key_ref[...] >= (cand ^ sign), 1.0, 0.0), axis=0, keepdims=True)
        return jnp.where(cnt >= nf, cand, ans)

    ans = lax.fori_loop(0, 32, body, jnp.zeros((1, LANE), jnp.int32))
    thr = ans ^ sign
    need = nf - jnp.sum(jnp.where(key_ref[...] > thr, 1.0, 0.0), axis=0, keepdims=True)
    ch = PREFIX_CHUNK
    tri = jnp.where(lax.broadcasted_iota(jnp.int32, (ch, ch), 0) > lax.broadcasted_iota(jnp.int32, (ch, ch), 1),
                    1.0, 0.0).astype(BF16)
    run = jnp.zeros((1, LANE), F32)
    ranks = []
    for c in range(l // ch):
        e = jnp.where(key_ref[c * ch:(c + 1) * ch, :] == thr, 1.0, 0.0)
        ranks.append(jnp.dot(tri, e.astype(BF16), preferred_element_type=F32) + run)
        run = run + jnp.sum(e, axis=0, keepdims=True)
    rank = jnp.concatenate(ranks, axis=0)
    key = key_ref[...]
    return jnp.logical_or(key > thr, jnp.logical_and(key == thr, rank < need))


def _dsa_prompt_kernel(qi_ref, wit_ref, ki_ref, q_ref, k_ref, v_ref, o_ref, sc_ref, key_ref,
                       *, n_sel, ih, nh, kvh, qb0):
    l = ki_ref.shape[1]
    qb = qb0 + pl.program_id(1)
    lhs = ki_ref[0].astype(BF16)
    for hp in range(ih // 2):
        c0 = 2 * hp * INDEX_DIM
        c1 = c0 + INDEX_DIM
        rhs = jnp.concatenate([qi_ref[:, c0:c1], qi_ref[:, c1:c1 + INDEX_DIM]], axis=0)
        rel = jnp.maximum(lax.dot_general(lhs, rhs, NT_DIMS, preferred_element_type=F32), 0.0)
        w0 = wit_ref[0, 0, 2 * hp:2 * hp + 1, :]
        w1 = wit_ref[0, 0, 2 * hp + 1:2 * hp + 2, :]
        part = rel[:, :Q_BLOCK] * w0 + rel[:, Q_BLOCK:] * w1
        if hp == 0:
            sc_ref[...] = part
        else:
            sc_ref[...] += part
    s_idx = lax.broadcasted_iota(jnp.int32, (l, Q_BLOCK), 0)
    q_pos = qb * Q_BLOCK + lax.broadcasted_iota(jnp.int32, (l, Q_BLOCK), 1)
    causal = s_idx <= q_pos
    sel = _topk_mask(jnp.where(causal, sc_ref[...], -jnp.inf), key_ref, n_sel)
    bias = jnp.where(jnp.logical_and(sel, causal), 0.0, -jnp.inf).T
    c_exp = float(HEAD_DIM) ** -0.5 * 1.4426950408889634
    rep = nh // kvh
    for g in range(kvh):
        kg = k_ref[0, :, g * HEAD_DIM:(g + 1) * HEAD_DIM].astype(BF16)
        vg = v_ref[0, :, g * HEAD_DIM:(g + 1) * HEAD_DIM].astype(BF16)
        h0 = g * rep
        qg = jnp.concatenate([q_ref[:, (h0 + r) * HEAD_DIM:(h0 + r + 1) * HEAD_DIM] for r in range(rep)], axis=0)
        lg = lax.dot_general(qg, kg, NT_DIMS, preferred_element_type=F32)
        ps, dens = [], []
        for r in range(rep):
            x = lg[r * Q_BLOCK:(r + 1) * Q_BLOCK] + bias
            p = jnp.exp2((x - jnp.max(x, axis=-1, keepdims=True)) * c_exp)
            dens.append(jnp.sum(p, axis=-1, keepdims=True))
            ps.append(p.astype(BF16))
        og = jnp.dot(jnp.concatenate(ps, axis=0), vg, preferred_element_type=F32)
        for r in range(rep):
            o = og[r * Q_BLOCK:(r + 1) * Q_BLOCK] / dens[r]
            o_ref[0, :, (h0 + r) * HEAD_DIM:(h0 + r + 1) * HEAD_DIM] = o.astype(o_ref.dtype)


def _causal_buckets(nq):
    for n in range(MAX_CAUSAL_BUCKETS, 0, -1):
        if nq % n == 0 and (nq // n * Q_BLOCK) % PREFIX_CHUNK == 0:
            return n
    return 1


def _dsa_prompt(qi, wit, ki, q, k, v, b, s, n_sel, ih, nh, kvh):
    nq = s // Q_BLOCK
    nbk = _causal_buckets(nq)
    per = nq // nbk
    per_b = lambda bb, qq: (bb, 0, 0)
    outs = []
    for u in range(nbk):
        le = (u + 1) * per * Q_BLOCK
        row = functools.partial(lambda bb, qq, u: (bb * nq + u * per + qq, 0), u=u)
        wrow = functools.partial(lambda bb, qq, u: (bb, u * per + qq, 0, 0), u=u)
        outs.append(pl.pallas_call(
            functools.partial(_dsa_prompt_kernel, n_sel=n_sel, ih=ih, nh=nh, kvh=kvh, qb0=u * per),
            grid=(b, per),
            in_specs=[pl.BlockSpec((Q_BLOCK, ih * INDEX_DIM), row),
                      pl.BlockSpec((1, 1, ih, Q_BLOCK), wrow),
                      pl.BlockSpec((1, le, INDEX_DIM), per_b),
                      pl.BlockSpec((Q_BLOCK, nh * HEAD_DIM), row),
                      pl.BlockSpec((1, le, kvh * HEAD_DIM), per_b), pl.BlockSpec((1, le, kvh * HEAD_DIM), per_b)],
            out_specs=pl.BlockSpec((1, Q_BLOCK, nh * HEAD_DIM), lambda bb, qq: (bb, qq, 0)),
            out_shape=jax.ShapeDtypeStruct((b, per * Q_BLOCK, nh * HEAD_DIM), BF16),
            scratch_shapes=[pltpu.VMEM((le, Q_BLOCK), F32), pltpu.VMEM((le, Q_BLOCK), jnp.int32)],
            compiler_params=_cparams(("parallel", "arbitrary")),
            name="dsa_prompt",
        )(qi, wit, ki, q, k, v))
    return jnp.concatenate(outs, axis=1).reshape(b * s, nh * HEAD_DIM)


def _dsa_score_kernel(pt_ref, qi_ref, wi_ref, kin_ref, *rest, n_pages):
    pages = rest[:n_pages]
    past_ref, new_ref = rest[n_pages:]
    qi = qi_ref[0]
    wcol = wi_ref[0]
    for p in range(n_pages):
        rel = lax.dot_general(qi, pages[p][0].astype(BF16), NT_DIMS, preferred_element_type=F32)
        past_ref[0, p] = jnp.sum(jnp.maximum(rel, 0.0) * wcol, axis=0, keepdims=True)
    kin = kin_ref[0].astype(BF16).astype(F32)
    rel_new = jnp.maximum(jnp.sum(qi.astype(F32) * kin, axis=-1, keepdims=True), 0.0)
    new = jnp.sum(rel_new * wcol, axis=0, keepdims=True)
    new_ref[0] = jnp.broadcast_to(new, (1, LANE))


def _dsa_scores(page_table, qi, wi, ki_new, pool_idx):
    db, n_pages = page_table.shape
    ih = qi.shape[1]
    psz = pool_idx.shape[1]
    per_b = lambda bb, pt: (bb, 0, 0)
    page_specs = [pl.BlockSpec((1, psz, INDEX_DIM), functools.partial(lambda bb, pt, p: (pt[bb, p], 0, 0), p=p))
                  for p in range(n_pages)]
    grid_spec = pltpu.PrefetchScalarGridSpec(
        num_scalar_prefetch=1,
        grid=(db,),
        in_specs=[pl.BlockSpec((1, ih, INDEX_DIM), per_b),
                  pl.BlockSpec((1, ih, 1), per_b), pl.BlockSpec((1, 1, INDEX_DIM), per_b)] + page_specs,
        out_specs=[pl.BlockSpec((1, n_pages, 1, psz), lambda bb, pt: (bb, 0, 0, 0)),
                   pl.BlockSpec((1, 1, LANE), per_b)],
    )
    return pl.pallas_call(
        functools.partial(_dsa_score_kernel, n_pages=n_pages),
        grid_spec=grid_spec,
        out_shape=[jax.ShapeDtypeStruct((db, n_pages, 1, psz), F32), jax.ShapeDtypeStruct((db, 1, LANE), F32)],
        compiler_params=_cparams(("arbitrary",)),
        name="dsa_sample_scores",
    )(page_table, qi, wi, ki_new, *([pool_idx] * n_pages))


def _mask_kernel(sc_ref, m_ref, key_ref, *, n_sel):
    m_ref[...] = jnp.where(_topk_mask(sc_ref[...], key_ref, n_sel), 1.0, 0.0)


def _select_mask(score_t, n_sel):
    l, n = score_t.shape
    blk = pl.BlockSpec((l, LANE), lambda i: (0, i))
    return pl.pallas_call(
        functools.partial(_mask_kernel, n_sel=n_sel),
        grid=(n // LANE,),
        in_specs=[blk],
        out_specs=blk,
        out_shape=jax.ShapeDtypeStruct((l, n), F32),
        scratch_shapes=[pltpu.VMEM((l, LANE), jnp.int32)],
        compiler_params=_cparams(("parallel",)),
        name="select_mask",
    )(score_t)


def _dsa_sample_kernel(pt_ref, q_ref, bias_ref, kn_ref, vn_ref, *rest, n_pages, nh, kvh):
    kpages = rest[:n_pages]
    vpages = rest[n_pages:2 * n_pages]
    o_ref = rest[2 * n_pages]
    rows = kpages[0].shape[0]
    rep = nh // kvh
    scale = float(HEAD_DIM) ** -0.5
    q = q_ref[0]

    def own(width):
        col = lax.broadcasted_iota(jnp.int32, (nh, width), 1)
        head = lax.broadcasted_iota(jnp.int32, (nh, width), 0)
        return col % kvh == head // rep

    own_page = own(rows)
    lgs = []
    for p in range(n_pages):
        lg = lax.dot_general(q, kpages[p][...].astype(BF16), NT_DIMS, preferred_element_type=F32)
        lgs.append(jnp.where(own_page, lg * scale + bias_ref[0, :, p * rows:(p + 1) * rows], -jnp.inf))
    lg_new = lax.dot_general(q, kn_ref[0].astype(BF16), NT_DIMS, preferred_element_type=F32)
    lg_new = jnp.where(own(LANE), lg_new * scale + bias_ref[0, :, n_pages * rows:n_pages * rows + LANE], -jnp.inf)
    m = jnp.max(lg_new, axis=-1, keepdims=True)
    for lg in lgs:
        m = jnp.maximum(m, jnp.max(lg, axis=-1, keepdims=True))
    p_new = jnp.exp(lg_new - m)
    den = jnp.sum(p_new, axis=-1, keepdims=True)
    acc = jnp.dot(p_new.astype(BF16), vn_ref[0].astype(BF16), preferred_element_type=F32)
    for p in range(n_pages):
        pp = jnp.exp(lgs[p] - m)
        den = den + jnp.sum(pp, axis=-1, keepdims=True)
        acc = acc + jnp.dot(pp.astype(BF16), vpages[p][...].astype(BF16), preferred_element_type=F32)
    o_ref[0] = (acc / den).astype(o_ref.dtype)


def _dsa_sample(page_table, q, bias, k_new, v_new, pool_k, pool_v, nh, kvh, psz):
    db, n_pages = page_table.shape
    rows = psz * kvh
    lb = bias.shape[2]
    per_b = lambda bb, pt: (bb, 0, 0)
    page_specs = [pl.BlockSpec((rows, HEAD_DIM), functools.partial(lambda bb, pt, p: (pt[bb, p], 0), p=p))
                  for p in range(n_pages)]
    grid_spec = pltpu.PrefetchScalarGridSpec(
        num_scalar_prefetch=1,
        grid=(db,),
        in_specs=[pl.BlockSpec((1, nh, HEAD_DIM), per_b), pl.BlockSpec((1, 1, lb), per_b),
                  pl.BlockSpec((1, LANE, HEAD_DIM), per_b), pl.BlockSpec((1, LANE, HEAD_DIM), per_b)]
        + page_specs + page_specs,
        out_specs=pl.BlockSpec((1, nh, HEAD_DIM), per_b),
    )
    return pl.pallas_call(
        functools.partial(_dsa_sample_kernel, n_pages=n_pages, nh=nh, kvh=kvh),
        grid_spec=grid_spec,
        out_shape=jax.ShapeDtypeStruct((db, nh, HEAD_DIM), BF16),
        compiler_params=_cparams(("arbitrary",)),
        name="dsa_sample_attend",
    )(page_table, q, bias, k_new, v_new, *([pool_k] * n_pages), *([pool_v] * n_pages))


def _cross_kernel(q_ref, mk_ref, mv_ref, o_ref, *, mh, tq):
    scale = float(HEAD_DIM) ** -0.5
    for h in range(mh):
        cs = slice(h * HEAD_DIM, (h + 1) * HEAD_DIM)
        qh = q_ref[0, :, cs].astype(BF16)
        if tq < SUBLANE:
            qh = jnp.broadcast_to(qh[0:1], (SUBLANE, HEAD_DIM))
        kh = mk_ref[0, :, cs].astype(BF16)
        vh = mv_ref[0, :, cs].astype(BF16)
        lg = lax.dot_general(qh, kh, NT_DIMS, preferred_element_type=F32) * scale
        p = jnp.exp(lg - jnp.max(lg, axis=-1, keepdims=True))
        den = jnp.sum(p, axis=-1, keepdims=True)
        o = jnp.dot(p.astype(BF16), vh, preferred_element_type=F32) / den
        o_ref[0, :, cs] = o[0:tq].astype(o_ref.dtype)


def _cross_attend(q, mk, mv, mh):
    b, t, md = q.shape
    nm = mk.shape[1]
    tq = _tile(t, 512, SUBLANE)
    return pl.pallas_call(
        functools.partial(_cross_kernel, mh=mh, tq=tq),
        grid=(b, t // tq),
        in_specs=[pl.BlockSpec((1, tq, md), lambda bb, i: (bb, i, 0)),
                  pl.BlockSpec((1, nm, md), lambda bb, i: (bb, 0, 0)),
                  pl.BlockSpec((1, nm, md), lambda bb, i: (bb, 0, 0))],
        out_specs=pl.BlockSpec((1, tq, md), lambda bb, i: (bb, i, 0)),
        out_shape=jax.ShapeDtypeStruct((b, t, md), BF16),
        compiler_params=_cparams(("parallel", "parallel")),
        name="cross_attend",
    )(q, mk, mv)


def _cross_rows_kernel(q_ref, mk_ref, mv_ref, o_ref, *, mh):
    scale = float(HEAD_DIM) ** -0.5
    rows = mk_ref.shape[0]
    q = q_ref[0].astype(BF16)
    lg = lax.dot_general(q, mk_ref[...].astype(BF16), NT_DIMS, preferred_element_type=F32) * scale
    col = lax.broadcasted_iota(jnp.int32, (SUBLANE, rows), 1)
    head = lax.broadcasted_iota(jnp.int32, (SUBLANE, rows), 0)
    lg = jnp.where(col % mh == head % mh, lg, -jnp.inf)
    p = jnp.exp(lg - jnp.max(lg, axis=-1, keepdims=True))
    den = jnp.sum(p, axis=-1, keepdims=True)
    o = jnp.dot(p.astype(BF16), mv_ref[...].astype(BF16), preferred_element_type=F32) / den
    o_ref[0] = o.astype(o_ref.dtype)


def _cross_attend_rows(q, mk, mv, mh, n_mem):
    b = q.shape[0]
    assert mh <= SUBLANE
    q8 = jnp.pad(q.reshape(b, mh, HEAD_DIM), ((0, 0), (0, SUBLANE - mh), (0, 0)))
    rows = n_mem * mh
    o = pl.pallas_call(
        functools.partial(_cross_rows_kernel, mh=mh),
        grid=(b,),
        in_specs=[pl.BlockSpec((1, SUBLANE, HEAD_DIM), lambda bb: (bb, 0, 0)),
                  pl.BlockSpec((rows, HEAD_DIM), lambda bb: (bb, 0)),
                  pl.BlockSpec((rows, HEAD_DIM), lambda bb: (bb, 0))],
        out_specs=pl.BlockSpec((1, SUBLANE, HEAD_DIM), lambda bb: (bb, 0, 0)),
        out_shape=jax.ShapeDtypeStruct((b, SUBLANE, HEAD_DIM), BF16),
        compiler_params=_cparams(("parallel",)),
        name="cross_attend_rows",
    )(q8, mk, mv)
    return o[:, :mh].reshape(b, mh * HEAD_DIM)


def _head_major_to_channel_major(x, h):
    lead = x.shape[:-1]
    return x.reshape(lead + (h, RWKV_HEAD_DIM)).swapaxes(-1, -2).reshape(x.shape)


def _channel_major_to_head_major(x, h):
    lead = x.shape[:-1]
    return x.reshape(lead + (RWKV_HEAD_DIM, h)).swapaxes(-1, -2).reshape(x.shape)


def _rwkv_cols(x, h, rd, lwd, lad, fn):
    o_k = rd + lwd
    o_a = o_k + 2 * rd
    return jnp.concatenate([fn(x[..., :rd], h), x[..., rd:o_k], fn(x[..., o_k:o_k + rd], h),
                            fn(x[..., o_k + rd:o_a], h), x[..., o_a:]], axis=-1)


def _to_chains(x, b, t, h, ts):
    nc = b * h
    y = x.reshape(b, t // ts, ts, RWKV_HEAD_DIM, h).transpose(1, 3, 2, 0, 4)
    y = y.reshape(t // ts, RWKV_HEAD_DIM * ts, nc)
    return jnp.pad(y, ((0, 0), (0, 0), (0, _round_up(nc, LANE) - nc)))


def _from_chains(y, b, t, h, ts):
    nc = b * h
    y = y[:, :, :nc].reshape(t // ts, RWKV_HEAD_DIM, ts, b, h).transpose(3, 0, 2, 1, 4)
    return y.reshape(b * t, RWKV_HEAD_DIM * h)


def _param_chains(p, b, h):
    nc = b * h
    y = jnp.tile(p.reshape(h, RWKV_HEAD_DIM).T, (1, b))
    return jnp.pad(y, ((0, 0), (0, _round_up(nc, LANE) - nc)))


def _rope_tables(pos):
    half = HEAD_DIM // 2
    inv = jnp.power(ROPE_THETA, -jnp.arange(half, dtype=F32) / half)
    ang = pos.astype(F32)[:, None] * inv[None, :]
    cos, sin = jnp.cos(ang), jnp.sin(ang)
    return jnp.concatenate([cos, cos], axis=-1), jnp.concatenate([-sin, sin], axis=-1)


def _mixer(x, b, t, pos, shift_prev, s0, lw, dims, attend):
    d, rd, lwd, lad, lgp, rp, rpp, nh, kvh, ih = dims
    h = rd // RWKV_HEAD_DIM
    m = b * t
    hn = _rmsnorm(x, lw["norm_mix"], BF16)
    gates = _mm(hn, lw["w_gates"])
    p = _mm(hn, lw["w_rwkv"])
    att = _mm(hn, lw["w_attn"])

    p3 = p.reshape(b, t, rpp)
    prev = jnp.pad(_rwkv_cols(shift_prev, h, rd, lwd, lad, _head_major_to_channel_major), ((0, 0), (0, rpp - rp)))
    r, w, k, v, a, g = _rwkv_prep(p, prev, b, t, lw["mu"], lw["w0"], lw["w2"], lw["a0"], lw["a2"], lw["g2"],
                                  rd, lwd, lad, lgp)
    nc = b * h
    in_kernel = _chain_kernels_fit(b, t, h)
    ts = SUBLANE if t % SUBLANE == 0 else 1
    if not in_kernel:
        r, w, k, v, a = [_to_chains(z, b, t, h, ts) for z in (r, w, k, v, a)]
    s0c = jnp.pad(s0.reshape(nc, RWKV_HEAD_DIM, RWKV_HEAD_DIM).transpose(2, 1, 0),
                  ((0, 0), (0, 0), (0, _round_up(nc, LANE) - nc)))
    yc, sc = _rwkv_scan(r, w, k, v, a,
                        _param_chains(lw["kk"], b, h), _param_chains(lw["ka"], b, h),
                        _param_chains(lw["rk"], b, h), _param_chains(lw["ln_g"], b, h),
                        _param_chains(lw["ln_b"], b, h), s0c)
    if in_kernel:
        yg = _rwkv_post(yc, g, b, t, h)
    else:
        yg = _gate_mul(_from_chains(yc, b, t, h, ts), g)
    s_fin = sc[:, :, :nc].transpose(2, 1, 0).reshape(b, h, RWKV_HEAD_DIM, RWKV_HEAD_DIM)
    shift_new = _rwkv_cols(p3[:, -1, :rp], h, rd, lwd, lad, _channel_major_to_head_major)

    cos2, sin2 = _rope_tables(pos)
    cos2 = jnp.broadcast_to(cos2[None], (b, t, LANE)).reshape(m, LANE)
    sin2 = jnp.broadcast_to(sin2[None], (b, t, LANE)).reshape(m, LANE)
    q, kr, vv, qi, ki, wi = _attn_prep(att, cos2, sin2, lw["idx_ln_g"], lw["idx_ln_b"], nh, kvh, ih)
    ya = attend(q, qi, wi, kr, vv, ki)

    merged = _merge(yg, ya, gates, lw["w_br_rwkv"], lw["w_br_attn"])
    x1 = _mm_resid(merged, lw["w_out"], x)
    return x1, kr, vv, ki, s_fin, shift_new


def _attend_prompt(q, qi, wi, kr, vv, ki, *, b, s, nh, kvh, ih):
    n_sel = min(TOPK_MAX, s // 4)
    nq = s // Q_BLOCK
    wit = wi[:, :ih].reshape(b, nq, Q_BLOCK, ih).transpose(0, 1, 3, 2)
    return _dsa_prompt(qi, wit, ki.reshape(b, s, INDEX_DIM), q,
                       kr.reshape(b, s, kvh * HEAD_DIM), vv.reshape(b, s, kvh * HEAD_DIM),
                       b, s, n_sel, ih, nh, kvh)


def _attend_sample(q, qi, wi, kr, vv, ki, *, pool_k, pool_v, pool_idx, page_table, nh, kvh, ih):
    db, n_pages = page_table.shape
    psz = pool_idx.shape[1]
    past = n_pages * psz
    n_sel = min(TOPK_MAX, (past + 1) // 4)
    kvd = kvh * HEAD_DIM
    sc_past, sc_new = _dsa_scores(page_table, qi.reshape(db, ih, INDEX_DIM),
                                  wi[:, :ih].reshape(db, ih, 1), ki.reshape(db, 1, INDEX_DIM), pool_idx)
    lp = _round_up(past + 1, PREFIX_CHUNK)
    dbp = _round_up(db, LANE)
    assert (lp - past) * kvh >= LANE
    score = jnp.concatenate([sc_past.reshape(db, past), sc_new[:, 0, :1]], axis=1)
    score_t = jnp.pad(score.T, ((0, lp - past - 1), (0, dbp - db)), constant_values=-jnp.inf)
    mask = _select_mask(score_t, n_sel)[:, :db].T
    bias = jnp.repeat(jnp.where(mask > 0.5, 0.0, -jnp.inf).astype(F32), kvh, axis=1).reshape(db, 1, lp * kvh)
    new_rows = lambda x: jnp.pad(x.reshape(db, kvh, HEAD_DIM), ((0, 0), (0, LANE - kvh), (0, 0)))
    o = _dsa_sample(page_table, q.reshape(db, nh, HEAD_DIM), bias, new_rows(kr), new_rows(vv),
                    pool_k, pool_v, nh, kvh, psz)
    return o.reshape(db, nh * HEAD_DIM)


def _cross_and_ffn(x, b, t, mk, mv, lw, mh, n_mem):
    hc = _rmsnorm(x, lw["norm_cross"], BF16)
    qm = _mm(hc, lw["w_q_mem"])
    if mk.ndim == 2:
        o = _cross_attend_rows(qm, mk, mv, mh, n_mem)
    else:
        o = _cross_attend(qm.reshape(b, t, -1), mk, mv, mh).reshape(b * t, -1)
    x2 = _mm_resid(o, lw["w_o_mem"], x)
    hf = _rmsnorm(x2, lw["norm_ffn"], BF16)
    act = _swiglu(hf, lw["w_up"])
    return _mm_resid(act, lw["w_down"], x2, tn_pref=512, tk_pref=5504)


def kernel(x_prompt, mem_prompt, x_sample, cache_k, cache_v, cache_idx_k, cache_mem_k, cache_mem_v, state_rwkv, state_shift, page_table, norm_mix, w_in, rwkv_mu, rwkv_w0, rwkv_w2, rwkv_a0, rwkv_a2, rwkv_g2, rwkv_kk, rwkv_ka, rwkv_rk, rwkv_ln_g, rwkv_ln_b, idx_ln_g, idx_ln_b, w_br_rwkv, w_br_attn, w_out, norm_cross, norm_mem, w_q_mem, w_k_mem, w_v_mem, w_o_mem, norm_ffn, w_up, w_down, final_norm):
    b, s, d = x_prompt.shape
    db, ts, _ = x_sample.shape
    assert ts == 1, "the sample group decodes one token per sequence"
    depth = w_in.shape[0]
    in_proj = w_in.shape[2]
    rp = rwkv_mu.shape[1]
    lwd, rd = rwkv_w2.shape[1:]
    lad = rwkv_a2.shape[1]
    lg = rwkv_g2.shape[1]
    lgp = _round_up(lg, LANE)
    rpp = rp - lg + lgp
    ad = w_br_attn.shape[1]
    nh = ad // HEAD_DIM
    kvh = cache_k.shape[3]
    kvd = kvh * HEAD_DIM
    ih = (in_proj - 2 * d - rp - ad - 2 * kvd - INDEX_DIM) // (INDEX_DIM + 1)
    mh = cache_mem_k.shape[3]
    n_mem = mem_prompt.shape[1]
    past = page_table.shape[1] * cache_k.shape[2]
    assert rp == 3 * rd + lwd + lad + lg and s % Q_BLOCK == 0 and ih % 2 == 0
    dims = (d, rd, lwd, lad, lgp, rp, rpp, nh, kvh, ih)
    n_pool, psz = cache_k.shape[1:3]
    pool_k = cache_k.reshape(depth * n_pool * psz * kvh, HEAD_DIM)
    pool_v = cache_v.reshape(depth * n_pool * psz * kvh, HEAD_DIM)
    pool_idx = cache_idx_k.reshape(depth * n_pool, psz, INDEX_DIM)
    mem_rows = db * n_mem * mh
    mem_k = cache_mem_k.reshape(depth * mem_rows, HEAD_DIM)
    mem_v = cache_mem_v.reshape(depth * mem_rows, HEAD_DIM)
    hr = rd // RWKV_HEAD_DIM
    cm = functools.partial(_head_major_to_channel_major, h=hr)

    xp = x_prompt.reshape(b * s, d)
    xs = x_sample.reshape(db, d)
    pos_p = jnp.arange(s, dtype=jnp.int32)
    pos_s = past + jnp.arange(ts, dtype=jnp.int32)
    outs = [[] for _ in range(12)]
    for l in range(depth):
        wl = w_in[l]
        o_att = 2 * d + rp
        n_att = ad + 2 * kvd + ih * INDEX_DIM + INDEX_DIM
        lw = {
            "norm_mix": norm_mix[l],
            "w_gates": wl[:, :2 * d].astype(BF16),
            "w_rwkv": jnp.pad(_rwkv_cols(wl[:, 2 * d:o_att], hr, rd, lwd, lad, _head_major_to_channel_major),
                              ((0, 0), (0, rpp - rp))).astype(BF16),
            "w_attn": jnp.pad(wl[:, o_att:], ((0, 0), (0, LANE - ih))).astype(BF16),
            "mu": jnp.pad(_rwkv_cols(rwkv_mu[l], hr, rd, lwd, lad, _head_major_to_channel_major),
                          (0, rpp - rp)).reshape(1, rpp),
            "w0": cm(rwkv_w0[l]).reshape(1, rd), "w2": cm(rwkv_w2[l]).astype(BF16),
            "a0": cm(rwkv_a0[l]).reshape(1, rd), "a2": cm(rwkv_a2[l]).astype(BF16),
            "g2": jnp.pad(cm(rwkv_g2[l]), ((0, lgp - lg), (0, 0))).astype(BF16),
            "kk": rwkv_kk[l], "ka": rwkv_ka[l], "rk": rwkv_rk[l].reshape(-1),
            "ln_g": rwkv_ln_g[l], "ln_b": rwkv_ln_b[l],
            "idx_ln_g": idx_ln_g[l].reshape(1, INDEX_DIM), "idx_ln_b": idx_ln_b[l].reshape(1, INDEX_DIM),
            "w_br_rwkv": cm(w_br_rwkv[l].T).T.astype(BF16), "w_br_attn": w_br_attn[l].astype(BF16),
            "w_out": w_out[l].astype(BF16),
            "norm_cross": norm_cross[l], "norm_ffn": norm_ffn[l],
            "w_q_mem": w_q_mem[l].astype(BF16), "w_o_mem": w_o_mem[l].astype(BF16),
            "w_up": w_up[l].astype(BF16), "w_down": w_down[l].astype(BF16),
        }
        assert n_att + ih == in_proj - o_att
        xp, kp, vp, kip, sfp, shp = _mixer(
            xp, b, s, pos_p, jnp.zeros((b, rp), F32),
            jnp.zeros((b, rd // RWKV_HEAD_DIM, RWKV_HEAD_DIM, RWKV_HEAD_DIM), F32), lw, dims,
            functools.partial(_attend_prompt, b=b, s=s, nh=nh, kvh=kvh, ih=ih))
        mn = _rmsnorm(mem_prompt.reshape(b * n_mem, d), norm_mem[l], BF16)
        mkp = _mm(mn, w_k_mem[l].astype(BF16)).reshape(b, n_mem, mh * HEAD_DIM)
        mvp = _mm(mn, w_v_mem[l].astype(BF16)).reshape(b, n_mem, mh * HEAD_DIM)
        xp = _cross_and_ffn(xp, b, s, mkp, mvp, lw, mh, n_mem)
        xs, ks_, vs_, kis, sfs, shs = _mixer(
            xs, db, ts, pos_s, state_shift[l], state_rwkv[l], lw, dims,
            functools.partial(_attend_sample, pool_k=pool_k, pool_v=pool_v, pool_idx=pool_idx,
                              page_table=page_table + l * n_pool, nh=nh, kvh=kvh, ih=ih))
        xs = _cross_and_ffn(xs, db, ts, mem_k[l * mem_rows:(l + 1) * mem_rows],
                            mem_v[l * mem_rows:(l + 1) * mem_rows], lw, mh, n_mem)
        vals = (kp.reshape(b, s, kvh, HEAD_DIM), vp.reshape(b, s, kvh, HEAD_DIM), kip.reshape(b, s, INDEX_DIM),
                mkp.reshape(b, n_mem, mh, HEAD_DIM), mvp.reshape(b, n_mem, mh, HEAD_DIM), sfp, shp,
                ks_.reshape(db, ts, kvh, HEAD_DIM), vs_.reshape(db, ts, kvh, HEAD_DIM),
                kis.reshape(db, ts, INDEX_DIM), sfs, shs)
        for lst, val in zip(outs, vals):
            lst.append(val)
    y_prompt = _rmsnorm(xp, final_norm, F32).reshape(b, s, d)
    y_sample = _rmsnorm(xs, final_norm, F32).reshape(db, ts, d)
    return (y_prompt, y_sample) + tuple(jnp.stack(o) for o in outs)
```

```python
import functools

import jax
import jax.numpy as jnp
from jax import lax
from jax.experimental import pallas as pl
from jax.experimental.pallas import tpu as pltpu

F32 = jnp.float32
BF16 = jnp.bfloat16
LANE = 128
SUBLANE = 8
VMEM_LIMIT = 56 * 1024 * 1024

RWKV_HEAD_DIM = 64
HEAD_DIM = 128
INDEX_DIM = 128
TOPK_MAX = 256
Q_BLOCK = 128
ROPE_THETA = 10000.0
NORM_EPS = 1e-6
RWKV_LN_EPS = 64e-5
IDX_LN_EPS = 1e-6
PREFIX_CHUNK = 256
MAX_CAUSAL_BUCKETS = 8
PREP_STEPS = 32
NT_DIMS = (((1,), (1,)), ((), ()))


def _cparams(sem):
    return pltpu.CompilerParams(dimension_semantics=sem, vmem_limit_bytes=VMEM_LIMIT)


def _round_up(n, m):
    return (n + m - 1) // m * m


def _tile(n, pref, unit):
    best = None
    t = unit
    while t <= min(n, pref):
        if n % t == 0:
            best = t
        t += unit
    return best if best is not None else n


def _rmsnorm_kernel(x_ref, g_ref, o_ref):
    x = x_ref[...]
    ms = jnp.mean(x * x, axis=-1, keepdims=True)
    o_ref[...] = (x * lax.rsqrt(ms + NORM_EPS) * g_ref[...]).astype(o_ref.dtype)


def _rmsnorm(x, g, out_dtype):
    m, d = x.shape
    tm = _tile(m, 256, SUBLANE)
    return pl.pallas_call(
        _rmsnorm_kernel,
        grid=(m // tm,),
        in_specs=[pl.BlockSpec((tm, d), lambda i: (i, 0)), pl.BlockSpec((1, d), lambda i: (0, 0))],
        out_specs=pl.BlockSpec((tm, d), lambda i: (i, 0)),
        out_shape=jax.ShapeDtypeStruct((m, d), out_dtype),
        compiler_params=_cparams(("parallel",)),
        name="rmsnorm",
    )(x, g.reshape(1, d))


MXU_COLS = 256


def _col_tile(n, pref):
    return _tile(n, pref, MXU_COLS if n % MXU_COLS == 0 else LANE)


def _mm_kernel(x_ref, w_ref, o_ref):
    o_ref[...] = jnp.dot(x_ref[...], w_ref[...].astype(BF16), preferred_element_type=F32).astype(o_ref.dtype)


def _mm(x, w, n=None, out_dtype=F32, tm_pref=1024, tn_pref=768):
    m, k = x.shape
    n = w.shape[1] if n is None else n
    tm = _tile(m, tm_pref, SUBLANE)
    tn = _col_tile(n, tn_pref)
    return pl.pallas_call(
        _mm_kernel,
        grid=(m // tm, n // tn),
        in_specs=[pl.BlockSpec((tm, k), lambda i, j: (i, 0)), pl.BlockSpec((k, tn), lambda i, j: (0, j))],
        out_specs=pl.BlockSpec((tm, tn), lambda i, j: (i, j)),
        out_shape=jax.ShapeDtypeStruct((m, n), out_dtype),
        compiler_params=_cparams(("parallel", "parallel")),
        name="matmul",
    )(x, w)


def _mm_resid_kernel(x_ref, w_ref, r_ref, o_ref, acc_ref, *, nk):
    kk = pl.program_id(2)

    @pl.when(kk == 0)
    def _():
        acc_ref[...] = r_ref[...]

    acc_ref[...] += jnp.dot(x_ref[...], w_ref[...].astype(BF16), preferred_element_type=F32)

    @pl.when(kk == nk - 1)
    def _():
        o_ref[...] = acc_ref[...]


def _mm_resid(x, w, resid, tm_pref=1024, tn_pref=1024, tk_pref=2048):
    m, k = x.shape
    n = w.shape[1]
    tm = _tile(m, tm_pref, SUBLANE)
    tn = _tile(n, tn_pref, LANE)
    tk = _tile(k, tk_pref, LANE)
    nk = k // tk
    return pl.pallas_call(
        functools.partial(_mm_resid_kernel, nk=nk),
        grid=(m // tm, n // tn, nk),
        in_specs=[pl.BlockSpec((tm, tk), lambda i, j, kk: (i, kk)),
                  pl.BlockSpec((tk, tn), lambda i, j, kk: (kk, j)),
                  pl.BlockSpec((tm, tn), lambda i, j, kk: (i, j))],
        out_specs=pl.BlockSpec((tm, tn), lambda i, j, kk: (i, j)),
        out_shape=jax.ShapeDtypeStruct((m, n), F32),
        scratch_shapes=[pltpu.VMEM((tm, tn), F32)],
        compiler_params=_cparams(("parallel", "parallel", "arbitrary")),
        name="matmul_resid",
    )(x, w, resid)


def _swiglu_kernel(x_ref, wg_ref, wu_ref, o_ref):
    x = x_ref[...]
    g = jnp.dot(x, wg_ref[...].astype(BF16), preferred_element_type=F32)
    u = jnp.dot(x, wu_ref[...].astype(BF16), preferred_element_type=F32)
    o_ref[...] = (g * jax.nn.sigmoid(g) * u).astype(o_ref.dtype)


def _swiglu(x, w_up, tm_pref=1024, tn_pref=256):
    m, k = x.shape
    f = w_up.shape[1] // 2
    tm = _tile(m, tm_pref, SUBLANE)
    tn = _tile(f, tn_pref, LANE)
    nb = f // tn
    return pl.pallas_call(
        _swiglu_kernel,
        grid=(m // tm, nb),
        in_specs=[pl.BlockSpec((tm, k), lambda i, j: (i, 0)),
                  pl.BlockSpec((k, tn), lambda i, j: (0, j)),
                  pl.BlockSpec((k, tn), lambda i, j: (0, j + nb))],
        out_specs=pl.BlockSpec((tm, tn), lambda i, j: (i, j)),
        out_shape=jax.ShapeDtypeStruct((m, f), BF16),
        compiler_params=_cparams(("parallel", "parallel")),
        name="swiglu",
    )(x, w_up, w_up)


def _gate_mul_kernel(y_ref, g_ref, o_ref):
    o_ref[...] = (y_ref[...] * g_ref[...]).astype(o_ref.dtype)


def _gate_mul(y, g):
    m, n = y.shape
    tm = _tile(m, 256, SUBLANE)
    blk = pl.BlockSpec((tm, n), lambda i: (i, 0))
    return pl.pallas_call(
        _gate_mul_kernel,
        grid=(m // tm,),
        in_specs=[blk, blk],
        out_specs=blk,
        out_shape=jax.ShapeDtypeStruct((m, n), BF16),
        compiler_params=_cparams(("parallel",)),
        name="gate_mul",
    )(y, g)


def _merge_kernel(yr_ref, ya_ref, gr_ref, ga_ref, wr_ref, wa_ref, o_ref):
    br = jnp.dot(yr_ref[...], wr_ref[...].astype(BF16), preferred_element_type=F32)
    ba = jnp.dot(ya_ref[...], wa_ref[...].astype(BF16), preferred_element_type=F32)
    o_ref[...] = (jax.nn.sigmoid(gr_ref[...]) * br + jax.nn.sigmoid(ga_ref[...]) * ba).astype(o_ref.dtype)


def _merge(yr, ya, gates, w_br_r, w_br_a, tm_pref=1024, tn_pref=512):
    m, rd = yr.shape
    ad = ya.shape[1]
    d = w_br_r.shape[1]
    tm = _tile(m, tm_pref, SUBLANE)
    tn = _tile(d, tn_pref, LANE)
    nb = d // tn
    return pl.pallas_call(
        _merge_kernel,
        grid=(m // tm, nb),
        in_specs=[pl.BlockSpec((tm, rd), lambda i, j: (i, 0)),
                  pl.BlockSpec((tm, ad), lambda i, j: (i, 0)),
                  pl.BlockSpec((tm, tn), lambda i, j: (i, j)),
                  pl.BlockSpec((tm, tn), lambda i, j: (i, j + nb)),
                  pl.BlockSpec((rd, tn), lambda i, j: (0, j)),
                  pl.BlockSpec((ad, tn), lambda i, j: (0, j))],
        out_specs=pl.BlockSpec((tm, tn), lambda i, j: (i, j)),
        out_shape=jax.ShapeDtypeStruct((m, d), BF16),
        compiler_params=_cparams(("parallel", "parallel")),
        name="merge",
    )(yr, ya, gates, gates, w_br_r, w_br_a)


def _lane_segment(h):
    return lax.broadcasted_iota(jnp.int32, (SUBLANE, LANE), 1) // h


def _to_chain_tiles(x, out_ref, nbatch, tt, h):
    per = LANE // h
    seg = _lane_segment(h)
    for c in range(nbatch // per):
        for g in range(tt // SUBLANE):
            for col in range(RWKV_HEAD_DIM // per):
                tiles = [x[(c * per + q) * tt + g * SUBLANE:(c * per + q) * tt + (g + 1) * SUBLANE,
                           col * LANE:(col + 1) * LANE] for q in range(per)]
                for jj in range(per):
                    acc = None
                    for q in range(per):
                        sh = ((q - jj) % per) * h
                        t = tiles[q] if sh == 0 else pltpu.roll(tiles[q], sh, axis=1)
                        acc = t if acc is None else jnp.where(seg == q, t, acc)
                    n = col * per + jj
                    out_ref[g, n * SUBLANE:(n + 1) * SUBLANE, c * LANE:(c + 1) * LANE] = acc


def _rwkv_prep_kernel(p_ref, sh_ref, *rest, rd, lw, la, lgp, chain):
    if chain is not None:
        prev_ref, rest = rest[0], rest[1:]
    mu_ref, w0_ref, w2_ref, a0_ref, a2_ref, g2_ref, r_ref, w_ref, k_ref, v_ref, a_ref, g_ref = rest
    if chain is None:
        p = p_ref[...]
        shifted = sh_ref[...]
    else:
        nbatch, tt, h = chain
        first = pl.program_id(0) == 0
        row0 = lax.broadcasted_iota(jnp.int32, (tt, p_ref.shape[2]), 0) == 0
        ps, shs = [], []
        for bb in range(nbatch):
            pb = p_ref[bb]
            above = jnp.where(first, prev_ref[bb], sh_ref[bb, SUBLANE - 1:SUBLANE, :])
            ps.append(pb)
            shs.append(jnp.where(row0, above, pltpu.roll(pb, 1, axis=0)))
        p = jnp.concatenate(ps, axis=0)
        shifted = jnp.concatenate(shs, axis=0)
    xm = p + (shifted - p) * mu_ref[...]
    o = 0
    xr = xm[:, o:o + rd]; o += rd
    xw = xm[:, o:o + lw]; o += lw
    xk = xm[:, o:o + rd]; o += rd
    xv = xm[:, o:o + rd]; o += rd
    xa = xm[:, o:o + la]; o += la
    xg = xm[:, o:o + lgp]
    z = w0_ref[...] + jnp.dot(jnp.tanh(xw).astype(BF16), w2_ref[...], preferred_element_type=F32)
    sp = jnp.maximum(-z, 0.0) + jnp.log(1.0 + jnp.exp(-jnp.abs(z)))
    w_raw = -sp - 0.5
    decay = jnp.exp(-jnp.exp(w_raw))
    a = jax.nn.sigmoid(a0_ref[...] + jnp.dot(xa.astype(BF16), a2_ref[...], preferred_element_type=F32))
    g = jnp.dot(jax.nn.sigmoid(xg).astype(BF16), g2_ref[...], preferred_element_type=F32)
    if chain is None:
        r_ref[...] = xr
        w_ref[...] = decay
        k_ref[...] = xk
        v_ref[...] = xv
        a_ref[...] = a
        g_ref[...] = g
    else:
        for x, ref in ((xr, r_ref), (decay, w_ref), (xk, k_ref), (xv, v_ref), (a, a_ref)):
            _to_chain_tiles(x, ref, nbatch, tt, h)
        for bb in range(nbatch):
            g_ref[bb] = g[bb * tt:(bb + 1) * tt]


def _chain_kernels_fit(b, t, h):
    return LANE % h == 0 and b % (LANE // h) == 0 and t % PREP_STEPS == 0


def _rwkv_prep(p, prev, b, t, mu, w0, w2, a0, a2, g2, rd, lw, la, lgp):
    m, rpp = p.shape
    h = rd // RWKV_HEAD_DIM
    fix = lambda i: (0, 0)
    par_specs = [pl.BlockSpec((1, rpp), fix), pl.BlockSpec((1, rd), fix), pl.BlockSpec((lw, rd), fix),
                 pl.BlockSpec((1, rd), fix), pl.BlockSpec((la, rd), fix), pl.BlockSpec((lgp, rd), fix)]
    if not _chain_kernels_fit(b, t, h):
        shifted = jnp.concatenate([prev[:, None, :], p.reshape(b, t, rpp)[:, :-1]], axis=1).reshape(m, rpp)
        tm = _tile(m, 128, SUBLANE)
        row = lambda i: (i, 0)
        out = jax.ShapeDtypeStruct((m, rd), F32)
        return pl.pallas_call(
            functools.partial(_rwkv_prep_kernel, rd=rd, lw=lw, la=la, lgp=lgp, chain=None),
            grid=(m // tm,),
            in_specs=[pl.BlockSpec((tm, rpp), row), pl.BlockSpec((tm, rpp), row)] + par_specs,
            out_specs=[pl.BlockSpec((tm, rd), row)] * 6,
            out_shape=[out] * 6,
            compiler_params=_cparams(("parallel",)),
            name="rwkv_prep",
        )(p, shifted, mu, w0, w2, a0, a2, g2)
    tt = PREP_STEPS
    nc = b * h
    p3 = p.reshape(b, t, rpp)
    step_blk = lambda i: (0, i, 0)
    chain_out = jax.ShapeDtypeStruct((t // SUBLANE, RWKV_HEAD_DIM * SUBLANE, nc), F32)
    chain_spec = pl.BlockSpec((tt // SUBLANE, RWKV_HEAD_DIM * SUBLANE, nc), lambda i: (i, 0, 0))
    outs = pl.pallas_call(
        functools.partial(_rwkv_prep_kernel, rd=rd, lw=lw, la=la, lgp=lgp, chain=(b, tt, h)),
        grid=(t // tt,),
        in_specs=[pl.BlockSpec((b, tt, rpp), step_blk),
                  pl.BlockSpec((b, SUBLANE, rpp), lambda i: (0, jnp.maximum(i * (tt // SUBLANE) - 1, 0), 0)),
                  pl.BlockSpec((b, 1, rpp), lambda i: (0, 0, 0))] + par_specs,
        out_specs=[chain_spec] * 5 + [pl.BlockSpec((b, tt, rd), step_blk)],
        out_shape=[chain_out] * 5 + [jax.ShapeDtypeStruct((b, t, rd), F32)],
        compiler_params=_cparams(("parallel",)),
        name="rwkv_prep",
    )(p3, p3, prev.reshape(b, 1, rpp), mu, w0, w2, a0, a2, g2)
    return list(outs[:5]) + [outs[5].reshape(m, rd)]


def _rwkv_scan_kernel(r_ref, w_ref, k_ref, v_ref, a_ref, kkw_ref, ka_ref, rk_ref, lng_ref, lnb_ref,
                      s0_ref, y_ref, s_ref, kk_s, nb_s, kp_s, *, ts):
    n = RWKV_HEAD_DIM

    @pl.when(pl.program_id(1) == 0)
    def _():
        s_ref[...] = s0_ref[...]

    def tile(j):
        return slice(j * ts, (j + 1) * ts)

    ss = None
    for j in range(n):
        kk = k_ref[0, tile(j), :] * kkw_ref[j:j + 1, :]
        kk_s[tile(j), :] = kk
        ss = kk * kk if ss is None else ss + kk * kk
    inv = 1.0 / jnp.maximum(jnp.sqrt(ss), 1e-12)
    bonus = None
    for j in range(n):
        kj = k_ref[0, tile(j), :]
        aj = a_ref[0, tile(j), :]
        kk = kk_s[tile(j), :] * inv
        kk_s[tile(j), :] = kk
        nb_s[tile(j), :] = -(kk * aj)
        kp = kj * (1.0 + (aj - 1.0) * ka_ref[j:j + 1, :])
        kp_s[tile(j), :] = kp
        term = r_ref[0, tile(j), :] * kp * rk_ref[j:j + 1, :]
        bonus = term if bonus is None else bonus + term

    def row(ref, j, tt):
        return ref[j * ts + tt:j * ts + tt + 1, :]

    def row3(ref, j, tt):
        return ref[0, j * ts + tt:j * ts + tt + 1, :]

    lng = lng_ref[...]
    lnb = lnb_ref[...]
    sk = None
    for j in range(n):
        term = s_ref[j] * row(kk_s, j, 0)
        sk = term if sk is None else sk + term
    for tt in range(ts):
        v = v_ref[0, pl.ds(tt, n, stride=ts), :]
        y = None
        sk_next = None
        for j in range(n):
            sn = s_ref[j] * row3(w_ref, j, tt) + sk * row(nb_s, j, tt) + v * row(kp_s, j, tt)
            s_ref[j] = sn
            yt = sn * row3(r_ref, j, tt)
            y = yt if y is None else y + yt
            if tt + 1 < ts:
                st = sn * row(kk_s, j, tt + 1)
                sk_next = st if sk_next is None else sk_next + st
        sk = sk_next
        mu = jnp.mean(y, axis=0, keepdims=True)
        yc = y - mu
        var = jnp.mean(yc * yc, axis=0, keepdims=True)
        yn = yc * lax.rsqrt(var + RWKV_LN_EPS) * lng + lnb
        y_ref[0, pl.ds(tt, n, stride=ts), :] = yn + bonus[tt:tt + 1, :] * v


def _rwkv_scan(r, w, k, v, a, kkw, ka, rk, lng, lnb, s0):
    nt, rows, nc = r.shape
    n = RWKV_HEAD_DIM
    ts = rows // n
    seq = pl.BlockSpec((1, rows, LANE), lambda c, tt: (tt, 0, c))
    par = pl.BlockSpec((n, LANE), lambda c, tt: (0, c))
    st = pl.BlockSpec((n, n, LANE), lambda c, tt: (0, 0, c))
    return pl.pallas_call(
        functools.partial(_rwkv_scan_kernel, ts=ts),
        grid=(nc // LANE, nt),
        in_specs=[seq] * 5 + [par] * 5 + [st],
        out_specs=[seq, st],
        out_shape=[jax.ShapeDtypeStruct((nt, rows, nc), F32), jax.ShapeDtypeStruct((n, n, nc), F32)],
        scratch_shapes=[pltpu.VMEM((rows, LANE), F32)] * 3,
        compiler_params=_cparams(("parallel", "arbitrary")),
        name="rwkv_scan",
    )(r, w, k, v, a, kkw, ka, rk, lng, lnb, s0)


def _rwkv_post_kernel(y_ref, g_ref, o_ref, *, nbatch, tt, h):
    per = LANE // h
    seg = _lane_segment(h)
    pair = 2 * SUBLANE
    for c in range(nbatch // per):
        for q in range(per):
            bb = c * per + q
            for g2 in range(tt // pair):
                for col in range(RWKV_HEAD_DIM // per):
                    halves = []
                    for g in (2 * g2, 2 * g2 + 1):
                        acc = None
                        for ii in range(per):
                            i = col * per + ii
                            tile = y_ref[g, i * SUBLANE:(i + 1) * SUBLANE, c * LANE:(c + 1) * LANE]
                            sh = ((ii - q) % per) * h
                            t = tile if sh == 0 else pltpu.roll(tile, sh, axis=1)
                            acc = t if acc is None else jnp.where(seg == ii, t, acc)
                        halves.append(acc)
                    rows = slice(g2 * pair, (g2 + 1) * pair)
                    cols = slice(col * LANE, (col + 1) * LANE)
                    o_ref[bb, rows, cols] = (jnp.concatenate(halves, axis=0) * g_ref[bb, rows, cols]).astype(o_ref.dtype)


def _rwkv_post(y, g, b, t, h):
    rd = RWKV_HEAD_DIM * h
    tt = PREP_STEPS
    step_blk = lambda i: (0, i, 0)
    return pl.pallas_call(
        functools.partial(_rwkv_post_kernel, nbatch=b, tt=tt, h=h),
        grid=(t // tt,),
        in_specs=[pl.BlockSpec((tt // SUBLANE, RWKV_HEAD_DIM * SUBLANE, b * h), lambda i: (i, 0, 0)),
                  pl.BlockSpec((b, tt, rd), step_blk)],
        out_specs=pl.BlockSpec((b, tt, rd), step_blk),
        out_shape=jax.ShapeDtypeStruct((b, t, rd), BF16),
        compiler_params=_cparams(("parallel",)),
        name="rwkv_post",
    )(y, g.reshape(b, t, rd)).reshape(b * t, rd)


def _rope(x, cos2, sin2):
    return x * cos2 + pltpu.roll(x, HEAD_DIM // 2, axis=1) * sin2


def _attn_prep_kernel(a_ref, cos_ref, sin_ref, lng_ref, lnb_ref,
                      q_ref, k_ref, v_ref, qi_ref, ki_ref, wi_ref, *, nh, kvh, ih, wscale):
    cos2 = cos_ref[...]
    sin2 = sin_ref[...]
    o = 0
    for h in range(nh):
        q_ref[:, h * HEAD_DIM:(h + 1) * HEAD_DIM] = _rope(a_ref[:, o:o + HEAD_DIM], cos2, sin2).astype(BF16)
        o += HEAD_DIM
    for h in range(kvh):
        k_ref[:, h * HEAD_DIM:(h + 1) * HEAD_DIM] = _rope(a_ref[:, o:o + HEAD_DIM], cos2, sin2)
        o += HEAD_DIM
    v_ref[...] = a_ref[:, o:o + kvh * HEAD_DIM]
    o += kvh * HEAD_DIM
    for h in range(ih):
        qi_ref[:, h * INDEX_DIM:(h + 1) * INDEX_DIM] = _rope(a_ref[:, o:o + INDEX_DIM], cos2, sin2).astype(BF16)
        o += INDEX_DIM
    x = a_ref[:, o:o + INDEX_DIM]
    o += INDEX_DIM
    mu = jnp.mean(x, axis=-1, keepdims=True)
    xc = x - mu
    var = jnp.mean(xc * xc, axis=-1, keepdims=True)
    kin = xc * lax.rsqrt(var + IDX_LN_EPS) * lng_ref[...] + lnb_ref[...]
    ki_ref[...] = _rope(kin, cos2, sin2)
    wi_ref[...] = a_ref[:, o:o + LANE] * wscale


def _attn_prep(a, cos2, sin2, lng, lnb, nh, kvh, ih):
    m, na = a.shape
    tm = _tile(m, 128, SUBLANE)
    row = lambda i: (i, 0)
    fix = lambda i: (0, 0)
    ad, kvd, idd = nh * HEAD_DIM, kvh * HEAD_DIM, ih * INDEX_DIM
    wscale = float(ih) ** -0.5 * float(INDEX_DIM) ** -0.5
    return pl.pallas_call(
        functools.partial(_attn_prep_kernel, nh=nh, kvh=kvh, ih=ih, wscale=wscale),
        grid=(m // tm,),
        in_specs=[pl.BlockSpec((tm, na), row), pl.BlockSpec((tm, LANE), row), pl.BlockSpec((tm, LANE), row),
                  pl.BlockSpec((1, LANE), fix), pl.BlockSpec((1, LANE), fix)],
        out_specs=[pl.BlockSpec((tm, ad), row), pl.BlockSpec((tm, kvd), row), pl.BlockSpec((tm, kvd), row),
                   pl.BlockSpec((tm, idd), row),
                   pl.BlockSpec((tm, INDEX_DIM), row), pl.BlockSpec((tm, LANE), row)],
        out_shape=[jax.ShapeDtypeStruct((m, ad), BF16), jax.ShapeDtypeStruct((m, kvd), F32),
                   jax.ShapeDtypeStruct((m, kvd), F32), jax.ShapeDtypeStruct((m, idd), BF16),
                   jax.ShapeDtypeStruct((m, INDEX_DIM), F32),
                   jax.ShapeDtypeStruct((m, LANE), F32)],
        compiler_params=_cparams(("parallel",)),
        name="attn_prep",
    )(a, cos2, sin2, lng, lnb)


def _topk_mask(score, key_ref, n_sel):
    l = score.shape[0]
    bits = lax.bitcast_convert_type(score, jnp.int32)
    key_ref[...] = jnp.where(bits < 0, bits ^ jnp.int32(0x7FFFFFFF), bits)
    sign = jnp.int32(-2 ** 31)
    nf = jnp.float32(n_sel)

    def body(i, ans):
        cand = ans | lax.shift_left(jnp.int32(1), 31 - i)
        cnt = jnp.sum(jnp.where(key_ref[...] >= (cand ^ sign), 1.0, 0.0), axis=0, keepdims=True)
        return jnp.where(cnt >= nf, cand, ans)

    ans = lax.fori_loop(0, 32, body, jnp.zeros((1, LANE), jnp.int32))
    thr = ans ^ sign
    need = nf - jnp.sum(jnp.where(key_ref[...] > thr, 1.0, 0.0), axis=0, keepdims=True)
    ch = PREFIX_CHUNK
    tri = jnp.where(lax.broadcasted_iota(jnp.int32, (ch, ch), 0) > lax.broadcasted_iota(jnp.int32, (ch, ch), 1),
                    1.0, 0.0).astype(BF16)
    run = jnp.zeros((1, LANE), F32)
    ranks = []
    for c in range(l // ch):
        e = jnp.where(key_ref[c * ch:(c + 1) * ch, :] == thr, 1.0, 0.0)
        ranks.append(jnp.dot(tri, e.astype(BF16), preferred_element_type=F32) + run)
        run = run + jnp.sum(e, axis=0, keepdims=True)
    rank = jnp.concatenate(ranks, axis=0)
    key = key_ref[...]
    return jnp.logical_or(key > thr, jnp.logical_and(key == thr, rank < need))


def _dsa_prompt_kernel(qi_ref, wit_ref, ki_ref, q_ref, k_ref, v_ref, o_ref, sc_ref, key_ref,
                       *, n_sel, ih, nh, kvh, qb0):
    l = ki_ref.shape[1]
    qb = qb0 + pl.program_id(1)
    lhs = ki_ref[0].astype(BF16)
    for hp in range(ih // 2):
        c0 = 2 * hp * INDEX_DIM
        c1 = c0 + INDEX_DIM
        rhs = jnp.concatenate([qi_ref[:, c0:c1], qi_ref[:, c1:c1 + INDEX_DIM]], axis=0)
        rel = jnp.maximum(lax.dot_general(lhs, rhs, NT_DIMS, preferred_element_type=F32), 0.0)
        w0 = wit_ref[0, 0, 2 * hp:2 * hp + 1, :]
        w1 = wit_ref[0, 0, 2 * hp + 1:2 * hp + 2, :]
        part = rel[:, :Q_BLOCK] * w0 + rel[:, Q_BLOCK:] * w1
        if hp == 0:
            sc_ref[...] = part
        else:
            sc_ref[...] += part
    s_idx = lax.broadcasted_iota(jnp.int32, (l, Q_BLOCK), 0)
    q_pos = qb * Q_BLOCK + lax.broadcasted_iota(jnp.int32, (l, Q_BLOCK), 1)
    causal = s_idx <= q_pos
    sel = _topk_mask(jnp.where(causal, sc_ref[...], -jnp.inf), key_ref, n_sel)
    bias = jnp.where(jnp.logical_and(sel, causal), 0.0, -jnp.inf).T
    c_exp = float(HEAD_DIM) ** -0.5 * 1.4426950408889634
    rep = nh // kvh
    for g in range(kvh):
        kg = k_ref[0, :, g * HEAD_DIM:(g + 1) * HEAD_DIM].astype(BF16)
        vg = v_ref[0, :, g * HEAD_DIM:(g + 1) * HEAD_DIM].astype(BF16)
        h0 = g * rep
        qg = jnp.concatenate([q_ref[:, (h0 + r) * HEAD_DIM:(h0 + r + 1) * HEAD_DIM] for r in range(rep)], axis=0)
        lg = lax.dot_general(qg, kg, NT_DIMS, preferred_element_type=F32)
        ps, dens = [], []
        for r in range(rep):
            x = lg[r * Q_BLOCK:(r + 1) * Q_BLOCK] + bias
            p = jnp.exp2((x - jnp.max(x, axis=-1, keepdims=True)) * c_exp)
            dens.append(jnp.sum(p, axis=-1, keepdims=True))
            ps.append(p.astype(BF16))
        og = jnp.dot(jnp.concatenate(ps, axis=0), vg, preferred_element_type=F32)
        for r in range(rep):
            o = og[r * Q_BLOCK:(r + 1) * Q_BLOCK] / dens[r]
            o_ref[0, :, (h0 + r) * HEAD_DIM:(h0 + r + 1) * HEAD_DIM] = o.astype(o_ref.dtype)


def _causal_buckets(nq):
    for n in range(MAX_CAUSAL_BUCKETS, 0, -1):
        if nq % n == 0 and (nq // n * Q_BLOCK) % PREFIX_CHUNK == 0:
            return n
    return 1


def _dsa_prompt(qi, wit, ki, q, k, v, b, s, n_sel, ih, nh, kvh):
    nq = s // Q_BLOCK
    nbk = _causal_buckets(nq)
    per = nq // nbk
    per_b = lambda bb, qq: (bb, 0, 0)
    outs = []
    for u in range(nbk):
        le = (u + 1) * per * Q_BLOCK
        row = functools.partial(lambda bb, qq, u: (bb * nq + u * per + qq, 0), u=u)
        wrow = functools.partial(lambda bb, qq, u: (bb, u * per + qq, 0, 0), u=u)
        outs.append(pl.pallas_call(
            functools.partial(_dsa_prompt_kernel, n_sel=n_sel, ih=ih, nh=nh, kvh=kvh, qb0=u * per),
            grid=(b, per),
            in_specs=[pl.BlockSpec((Q_BLOCK, ih * INDEX_DIM), row),
                      pl.BlockSpec((1, 1, ih, Q_BLOCK), wrow),
                      pl.BlockSpec((1, le, INDEX_DIM), per_b),
                      pl.BlockSpec((Q_BLOCK, nh * HEAD_DIM), row),
                      pl.BlockSpec((1, le, kvh * HEAD_DIM), per_b), pl.BlockSpec((1, le, kvh * HEAD_DIM), per_b)],
            out_specs=pl.BlockSpec((1, Q_BLOCK, nh * HEAD_DIM), lambda bb, qq: (bb, qq, 0)),
            out_shape=jax.ShapeDtypeStruct((b, per * Q_BLOCK, nh * HEAD_DIM), BF16),
            scratch_shapes=[pltpu.VMEM((le, Q_BLOCK), F32), pltpu.VMEM((le, Q_BLOCK), jnp.int32)],
            compiler_params=_cparams(("parallel", "arbitrary")),
            name="dsa_prompt",
        )(qi, wit, ki, q, k, v))
    return jnp.concatenate(outs, axis=1).reshape(b * s, nh * HEAD_DIM)


def _dsa_score_kernel(pt_ref, qi_ref, wi_ref, kin_ref, *rest, n_pages):
    pages = rest[:n_pages]
    past_ref, new_ref = rest[n_pages:]
    qi = qi_ref[0]
    wcol = wi_ref[0]
    for p in range(n_pages):
        rel = lax.dot_general(qi, pages[p][0].astype(BF16), NT_DIMS, preferred_element_type=F32)
        past_ref[0, p] = jnp.sum(jnp.maximum(rel, 0.0) * wcol, axis=0, keepdims=True)
    kin = kin_ref[0].astype(BF16).astype(F32)
    rel_new = jnp.maximum(jnp.sum(qi.astype(F32) * kin, axis=-1, keepdims=True), 0.0)
    new = jnp.sum(rel_new * wcol, axis=0, keepdims=True)
    new_ref[0] = jnp.broadcast_to(new, (1, LANE))


def _dsa_scores(page_table, qi, wi, ki_new, pool_idx):
    db, n_pages = page_table.shape
    ih = qi.shape[1]
    psz = pool_idx.shape[1]
    per_b = lambda bb, pt: (bb, 0, 0)
    page_specs = [pl.BlockSpec((1, psz, INDEX_DIM), functools.partial(lambda bb, pt, p: (pt[bb, p], 0, 0), p=p))
                  for p in range(n_pages)]
    grid_spec = pltpu.PrefetchScalarGridSpec(
        num_scalar_prefetch=1,
        grid=(db,),
        in_specs=[pl.BlockSpec((1, ih, INDEX_DIM), per_b),
                  pl.BlockSpec((1, ih, 1), per_b), pl.BlockSpec((1, 1, INDEX_DIM), per_b)] + page_specs,
        out_specs=[pl.BlockSpec((1, n_pages, 1, psz), lambda bb, pt: (bb, 0, 0, 0)),
                   pl.BlockSpec((1, 1, LANE), per_b)],
    )
    return pl.pallas_call(
        functools.partial(_dsa_score_kernel, n_pages=n_pages),
        grid_spec=grid_spec,
        out_shape=[jax.ShapeDtypeStruct((db, n_pages, 1, psz), F32), jax.ShapeDtypeStruct((db, 1, LANE), F32)],
        compiler_params=_cparams(("arbitrary",)),
        name="dsa_sample_scores",
    )(page_table, qi, wi, ki_new, *([pool_idx] * n_pages))


def _mask_kernel(sc_ref, m_ref, key_ref, *, n_sel):
    m_ref[...] = jnp.where(_topk_mask(sc_ref[...], key_ref, n_sel), 1.0, 0.0)


def _select_mask(score_t, n_sel):
    l, n = score_t.shape
    blk = pl.BlockSpec((l, LANE), lambda i: (0, i))
    return pl.pallas_call(
        functools.partial(_mask_kernel, n_sel=n_sel),
        grid=(n // LANE,),
        in_specs=[blk],
        out_specs=blk,
        out_shape=jax.ShapeDtypeStruct((l, n), F32),
        scratch_shapes=[pltpu.VMEM((l, LANE), jnp.int32)],
        compiler_params=_cparams(("parallel",)),
        name="select_mask",
    )(score_t)


def _dsa_sample_kernel(pt_ref, q_ref, bias_ref, kn_ref, vn_ref, *rest, n_pages, nh, kvh):
    kpages = rest[:n_pages]
    vpages = rest[n_pages:2 * n_pages]
    o_ref = rest[2 * n_pages]
    rows = kpages[0].shape[0]
    rep = nh // kvh
    scale = float(HEAD_DIM) ** -0.5
    q = q_ref[0]

    def own(width):
        col = lax.broadcasted_iota(jnp.int32, (nh, width), 1)
        head = lax.broadcasted_iota(jnp.int32, (nh, width), 0)
        return col % kvh == head // rep

    own_page = own(rows)
    lgs = []
    for p in range(n_pages):
        lg = lax.dot_general(q, kpages[p][...].astype(BF16), NT_DIMS, preferred_element_type=F32)
        lgs.append(jnp.where(own_page, lg * scale + bias_ref[0, :, p * rows:(p + 1) * rows], -jnp.inf))
    lg_new = lax.dot_general(q, kn_ref[0].astype(BF16), NT_DIMS, preferred_element_type=F32)
    lg_new = jnp.where(own(LANE), lg_new * scale + bias_ref[0, :, n_pages * rows:n_pages * rows + LANE], -jnp.inf)
    m = jnp.max(lg_new, axis=-1, keepdims=True)
    for lg in lgs:
        m = jnp.maximum(m, jnp.max(lg, axis=-1, keepdims=True))
    p_new = jnp.exp(lg_new - m)
    den = jnp.sum(p_new, axis=-1, keepdims=True)
    acc = jnp.dot(p_new.astype(BF16), vn_ref[0].astype(BF16), preferred_element_type=F32)
    for p in range(n_pages):
        pp = jnp.exp(lgs[p] - m)
        den = den + jnp.sum(pp, axis=-1, keepdims=True)
        acc = acc + jnp.dot(pp.astype(BF16), vpages[p][...].astype(BF16), preferred_element_type=F32)
    o_ref[0] = (acc / den).astype(o_ref.dtype)


def _dsa_sample(page_table, q, bias, k_new, v_new, pool_k, pool_v, nh, kvh, psz):
    db, n_pages = page_table.shape
    rows = psz * kvh
    lb = bias.shape[2]
    per_b = lambda bb, pt: (bb, 0, 0)
    page_specs = [pl.BlockSpec((rows, HEAD_DIM), functools.partial(lambda bb, pt, p: (pt[bb, p], 0), p=p))
                  for p in range(n_pages)]
    grid_spec = pltpu.PrefetchScalarGridSpec(
        num_scalar_prefetch=1,
        grid=(db,),
        in_specs=[pl.BlockSpec((1, nh, HEAD_DIM), per_b), pl.BlockSpec((1, 1, lb), per_b),
                  pl.BlockSpec((1, LANE, HEAD_DIM), per_b), pl.BlockSpec((1, LANE, HEAD_DIM), per_b)]
        + page_specs + page_specs,
        out_specs=pl.BlockSpec((1, nh, HEAD_DIM), per_b),
    )
    return pl.pallas_call(
        functools.partial(_dsa_sample_kernel, n_pages=n_pages, nh=nh, kvh=kvh),
        grid_spec=grid_spec,
        out_shape=jax.ShapeDtypeStruct((db, nh, HEAD_DIM), BF16),
        compiler_params=_cparams(("arbitrary",)),
        name="dsa_sample_attend",
    )(page_table, q, bias, k_new, v_new, *([pool_k] * n_pages), *([pool_v] * n_pages))


def _cross_kernel(q_ref, mk_ref, mv_ref, o_ref, *, mh, tq):
    scale = float(HEAD_DIM) ** -0.5
    for h in range(mh):
        cs = slice(h * HEAD_DIM, (h + 1) * HEAD_DIM)
        qh = q_ref[0, :, cs].astype(BF16)
        if tq < SUBLANE:
            qh = jnp.broadcast_to(qh[0:1], (SUBLANE, HEAD_DIM))
        kh = mk_ref[0, :, cs].astype(BF16)
        vh = mv_ref[0, :, cs].astype(BF16)
        lg = lax.dot_general(qh, kh, NT_DIMS, preferred_element_type=F32) * scale
        p = jnp.exp(lg - jnp.max(lg, axis=-1, keepdims=True))
        den = jnp.sum(p, axis=-1, keepdims=True)
        o = jnp.dot(p.astype(BF16), vh, preferred_element_type=F32) / den
        o_ref[0, :, cs] = o[0:tq].astype(o_ref.dtype)


def _cross_attend(q, mk, mv, mh):
    b, t, md = q.shape
    nm = mk.shape[1]
    tq = _tile(t, 512, SUBLANE)
    return pl.pallas_call(
        functools.partial(_cross_kernel, mh=mh, tq=tq),
        grid=(b, t // tq),
        in_specs=[pl.BlockSpec((1, tq, md), lambda bb, i: (bb, i, 0)),
                  pl.BlockSpec((1, nm, md), lambda bb, i: (bb, 0, 0)),
                  pl.BlockSpec((1, nm, md), lambda bb, i: (bb, 0, 0))],
        out_specs=pl.BlockSpec((1, tq, md), lambda bb, i: (bb, i, 0)),
        out_shape=jax.ShapeDtypeStruct((b, t, md), BF16),
        compiler_params=_cparams(("parallel", "parallel")),
        name="cross_attend",
    )(q, mk, mv)


def _cross_rows_kernel(q_ref, mk_ref, mv_ref, o_ref, *, mh):
    scale = float(HEAD_DIM) ** -0.5
    rows = mk_ref.shape[0]
    q = q_ref[0].astype(BF16)
    lg = lax.dot_general(q, mk_ref[...].astype(BF16), NT_DIMS, preferred_element_type=F32) * scale
    col = lax.broadcasted_iota(jnp.int32, (SUBLANE, rows), 1)
    head = lax.broadcasted_iota(jnp.int32, (SUBLANE, rows), 0)
    lg = jnp.where(col % mh == head % mh, lg, -jnp.inf)
    p = jnp.exp(lg - jnp.max(lg, axis=-1, keepdims=True))
    den = jnp.sum(p, axis=-1, keepdims=True)
    o = jnp.dot(p.astype(BF16), mv_ref[...].astype(BF16), preferred_element_type=F32) / den
    o_ref[0] = o.astype(o_ref.dtype)


def _cross_attend_rows(q, mk, mv, mh, n_mem):
    b = q.shape[0]
    assert mh <= SUBLANE
    q8 = jnp.pad(q.reshape(b, mh, HEAD_DIM), ((0, 0), (0, SUBLANE - mh), (0, 0)))
    rows = n_mem * mh
    o = pl.pallas_call(
        functools.partial(_cross_rows_kernel, mh=mh),
        grid=(b,),
        in_specs=[pl.BlockSpec((1, SUBLANE, HEAD_DIM), lambda bb: (bb, 0, 0)),
                  pl.BlockSpec((rows, HEAD_DIM), lambda bb: (bb, 0)),
                  pl.BlockSpec((rows, HEAD_DIM), lambda bb: (bb, 0))],
        out_specs=pl.BlockSpec((1, SUBLANE, HEAD_DIM), lambda bb: (bb, 0, 0)),
        out_shape=jax.ShapeDtypeStruct((b, SUBLANE, HEAD_DIM), BF16),
        compiler_params=_cparams(("parallel",)),
        name="cross_attend_rows",
    )(q8, mk, mv)
    return o[:, :mh].reshape(b, mh * HEAD_DIM)


def _head_major_to_channel_major(x, h):
    lead = x.shape[:-1]
    return x.reshape(lead + (h, RWKV_HEAD_DIM)).swapaxes(-1, -2).reshape(x.shape)


def _channel_major_to_head_major(x, h):
    lead = x.shape[:-1]
    return x.reshape(lead + (RWKV_HEAD_DIM, h)).swapaxes(-1, -2).reshape(x.shape)


def _rwkv_cols(x, h, rd, lwd, lad, fn):
    o_k = rd + lwd
    o_a = o_k + 2 * rd
    return jnp.concatenate([fn(x[..., :rd], h), x[..., rd:o_k], fn(x[..., o_k:o_k + rd], h),
                            fn(x[..., o_k + rd:o_a], h), x[..., o_a:]], axis=-1)


def _to_chains(x, b, t, h, ts):
    nc = b * h
    y = x.reshape(b, t // ts, ts, RWKV_HEAD_DIM, h).transpose(1, 3, 2, 0, 4)
    y = y.reshape(t // ts, RWKV_HEAD_DIM * ts, nc)
    return jnp.pad(y, ((0, 0), (0, 0), (0, _round_up(nc, LANE) - nc)))


def _from_chains(y, b, t, h, ts):
    nc = b * h
    y = y[:, :, :nc].reshape(t // ts, RWKV_HEAD_DIM, ts, b, h).transpose(3, 0, 2, 1, 4)
    return y.reshape(b * t, RWKV_HEAD_DIM * h)


def _param_chains(p, b, h):
    nc = b * h
    y = jnp.tile(p.reshape(h, RWKV_HEAD_DIM).T, (1, b))
    return jnp.pad(y, ((0, 0), (0, _round_up(nc, LANE) - nc)))


def _rope_tables(pos):
    half = HEAD_DIM // 2
    inv = jnp.power(ROPE_THETA, -jnp.arange(half, dtype=F32) / half)
    ang = pos.astype(F32)[:, None] * inv[None, :]
    cos, sin = jnp.cos(ang), jnp.sin(ang)
    return jnp.concatenate([cos, cos], axis=-1), jnp.concatenate([-sin, sin], axis=-1)


def _mixer(x, b, t, pos, shift_prev, s0, lw, dims, attend):
    d, rd, lwd, lad, lgp, rp, rpp, nh, kvh, ih = dims
    h = rd // RWKV_HEAD_DIM
    m = b * t
    hn = _rmsnorm(x, lw["norm_mix"], BF16)
    gates = _mm(hn, lw["w_in"], n=2 * d, tn_pref=512)
    p = _mm(hn, lw["w_rwkv"])
    att = _mm(hn, lw["w_attn"])

    p3 = p.reshape(b, t, rpp)
    prev = jnp.pad(_rwkv_cols(shift_prev, h, rd, lwd, lad, _head_major_to_channel_major), ((0, 0), (0, rpp - rp)))
    r, w, k, v, a, g = _rwkv_prep(p, prev, b, t, lw["mu"], lw["w0"], lw["w2"], lw["a0"], lw["a2"], lw["g2"],
                                  rd, lwd, lad, lgp)
    nc = b * h
    in_kernel = _chain_kernels_fit(b, t, h)
    ts = SUBLANE if t % SUBLANE == 0 else 1
    if not in_kernel:
        r, w, k, v, a = [_to_chains(z, b, t, h, ts) for z in (r, w, k, v, a)]
    s0c = jnp.pad(s0.reshape(nc, RWKV_HEAD_DIM, RWKV_HEAD_DIM).transpose(2, 1, 0),
                  ((0, 0), (0, 0), (0, _round_up(nc, LANE) - nc)))
    yc, sc = _rwkv_scan(r, w, k, v, a,
                        _param_chains(lw["kk"], b, h), _param_chains(lw["ka"], b, h),
                        _param_chains(lw["rk"], b, h), _param_chains(lw["ln_g"], b, h),
                        _param_chains(lw["ln_b"], b, h), s0c)
    if in_kernel:
        yg = _rwkv_post(yc, g, b, t, h)
    else:
        yg = _gate_mul(_from_chains(yc, b, t, h, ts), g)
    s_fin = sc[:, :, :nc].transpose(2, 1, 0).reshape(b, h, RWKV_HEAD_DIM, RWKV_HEAD_DIM)
    shift_new = _rwkv_cols(p3[:, -1, :rp], h, rd, lwd, lad, _channel_major_to_head_major)

    cos2, sin2 = _rope_tables(pos)
    cos2 = jnp.broadcast_to(cos2[None], (b, t, LANE)).reshape(m, LANE)
    sin2 = jnp.broadcast_to(sin2[None], (b, t, LANE)).reshape(m, LANE)
    q, kr, vv, qi, ki, wi = _attn_prep(att, cos2, sin2, lw["idx_ln_g"], lw["idx_ln_b"], nh, kvh, ih)
    ya = attend(q, qi, wi, kr, vv, ki)

    merged = _merge(yg, ya, gates, lw["w_br_rwkv"], lw["w_br_attn"])
    x1 = _mm_resid(merged, lw["w_out"], x)
    return x1, kr, vv, ki, s_fin, shift_new


def _attend_prompt(q, qi, wi, kr, vv, ki, *, b, s, nh, kvh, ih):
    n_sel = min(TOPK_MAX, s // 4)
    nq = s // Q_BLOCK
    wit = wi[:, :ih].reshape(b, nq, Q_BLOCK, ih).transpose(0, 1, 3, 2)
    return _dsa_prompt(qi, wit, ki.reshape(b, s, INDEX_DIM), q,
                       kr.reshape(b, s, kvh * HEAD_DIM), vv.reshape(b, s, kvh * HEAD_DIM),
                       b, s, n_sel, ih, nh, kvh)


def _attend_sample(q, qi, wi, kr, vv, ki, *, pool_k, pool_v, pool_idx, page_table, nh, kvh, ih):
    db, n_pages = page_table.shape
    psz = pool_idx.shape[1]
    past = n_pages * psz
    n_sel = min(TOPK_MAX, (past + 1) // 4)
    kvd = kvh * HEAD_DIM
    sc_past, sc_new = _dsa_scores(page_table, qi.reshape(db, ih, INDEX_DIM),
                                  wi[:, :ih].reshape(db, ih, 1), ki.reshape(db, 1, INDEX_DIM), pool_idx)
    lp = _round_up(past + 1, PREFIX_CHUNK)
    dbp = _round_up(db, LANE)
    assert (lp - past) * kvh >= LANE
    score = jnp.concatenate([sc_past.reshape(db, past), sc_new[:, 0, :1]], axis=1)
    score_t = jnp.pad(score.T, ((0, lp - past - 1), (0, dbp - db)), constant_values=-jnp.inf)
    mask = _select_mask(score_t, n_sel)[:, :db].T
    bias = jnp.repeat(jnp.where(mask > 0.5, 0.0, -jnp.inf).astype(F32), kvh, axis=1).reshape(db, 1, lp * kvh)
    new_rows = lambda x: jnp.pad(x.reshape(db, kvh, HEAD_DIM), ((0, 0), (0, LANE - kvh), (0, 0)))
    o = _dsa_sample(page_table, q.reshape(db, nh, HEAD_DIM), bias, new_rows(kr), new_rows(vv),
                    pool_k, pool_v, nh, kvh, psz)
    return o.reshape(db, nh * HEAD_DIM)


def _cross_and_ffn(x, b, t, mk, mv, lw, mh, n_mem):
    hc = _rmsnorm(x, lw["norm_cross"], BF16)
    qm = _mm(hc, lw["w_q_mem"])
    if mk.ndim == 2:
        o = _cross_attend_rows(qm, mk, mv, mh, n_mem)
    else:
        o = _cross_attend(qm.reshape(b, t, -1), mk, mv, mh).reshape(b * t, -1)
    x2 = _mm_resid(o, lw["w_o_mem"], x)
    hf = _rmsnorm(x2, lw["norm_ffn"], BF16)
    act = _swiglu(hf, lw["w_up"])
    return _mm_resid(act, lw["w_down"], x2, tn_pref=512, tk_pref=5504)


def kernel(x_prompt, mem_prompt, x_sample, cache_k, cache_v, cache_idx_k, cache_mem_k, cache_mem_v, state_rwkv, state_shift, page_table, norm_mix, w_in, rwkv_mu, rwkv_w0, rwkv_w2, rwkv_a0, rwkv_a2, rwkv_g2, rwkv_kk, rwkv_ka, rwkv_rk, rwkv_ln_g, rwkv_ln_b, idx_ln_g, idx_ln_b, w_br_rwkv, w_br_attn, w_out, norm_cross, norm_mem, w_q_mem, w_k_mem, w_v_mem, w_o_mem, norm_ffn, w_up, w_down, final_norm):
    b, s, d = x_prompt.shape
    db, ts, _ = x_sample.shape
    assert ts == 1, "the sample group decodes one token per sequence"
    depth = w_in.shape[0]
    in_proj = w_in.shape[2]
    rp = rwkv_mu.shape[1]
    lwd, rd = rwkv_w2.shape[1:]
    lad = rwkv_a2.shape[1]
    lg = rwkv_g2.shape[1]
    lgp = _round_up(lg, LANE)
    rpp = rp - lg + lgp
    ad = w_br_attn.shape[1]
    nh = ad // HEAD_DIM
    kvh = cache_k.shape[3]
    kvd = kvh * HEAD_DIM
    ih = (in_proj - 2 * d - rp - ad - 2 * kvd - INDEX_DIM) // (INDEX_DIM + 1)
    mh = cache_mem_k.shape[3]
    n_mem = mem_prompt.shape[1]
    past = page_table.shape[1] * cache_k.shape[2]
    assert rp == 3 * rd + lwd + lad + lg and s % Q_BLOCK == 0 and ih % 2 == 0
    dims = (d, rd, lwd, lad, lgp, rp, rpp, nh, kvh, ih)
    n_pool, psz = cache_k.shape[1:3]
    pool_k = cache_k.reshape(depth * n_pool * psz * kvh, HEAD_DIM)
    pool_v = cache_v.reshape(depth * n_pool * psz * kvh, HEAD_DIM)
    pool_idx = cache_idx_k.reshape(depth * n_pool, psz, INDEX_DIM)
    mem_rows = db * n_mem * mh
    mem_k = cache_mem_k.reshape(depth * mem_rows, HEAD_DIM)
    mem_v = cache_mem_v.reshape(depth * mem_rows, HEAD_DIM)
    hr = rd // RWKV_HEAD_DIM
    cm = functools.partial(_head_major_to_channel_major, h=hr)

    xp = x_prompt.reshape(b * s, d)
    xs = x_sample.reshape(db, d)
    pos_p = jnp.arange(s, dtype=jnp.int32)
    pos_s = past + jnp.arange(ts, dtype=jnp.int32)
    outs = [[] for _ in range(12)]
    for l in range(depth):
        wl = w_in[l]
        o_att = 2 * d + rp
        n_att = ad + 2 * kvd + ih * INDEX_DIM + INDEX_DIM
        lw = {
            "norm_mix": norm_mix[l],
            "w_in": wl,
            "w_rwkv": jnp.pad(_rwkv_cols(wl[:, 2 * d:o_att], hr, rd, lwd, lad, _head_major_to_channel_major),
                              ((0, 0), (0, rpp - rp))).astype(BF16),
            "w_attn": jnp.pad(wl[:, o_att:], ((0, 0), (0, LANE - ih))).astype(BF16),
            "mu": jnp.pad(_rwkv_cols(rwkv_mu[l], hr, rd, lwd, lad, _head_major_to_channel_major),
                          (0, rpp - rp)).reshape(1, rpp),
            "w0": cm(rwkv_w0[l]).reshape(1, rd), "w2": cm(rwkv_w2[l]).astype(BF16),
            "a0": cm(rwkv_a0[l]).reshape(1, rd), "a2": cm(rwkv_a2[l]).astype(BF16),
            "g2": jnp.pad(cm(rwkv_g2[l]), ((0, lgp - lg), (0, 0))).astype(BF16),
            "kk": rwkv_kk[l], "ka": rwkv_ka[l], "rk": rwkv_rk[l].reshape(-1),
            "ln_g": rwkv_ln_g[l], "ln_b": rwkv_ln_b[l],
            "idx_ln_g": idx_ln_g[l].reshape(1, INDEX_DIM), "idx_ln_b": idx_ln_b[l].reshape(1, INDEX_DIM),
            "w_br_rwkv": cm(w_br_rwkv[l].T).T.astype(BF16), "w_br_attn": w_br_attn[l],
            "w_out": w_out[l],
            "norm_cross": norm_cross[l], "norm_ffn": norm_ffn[l],
            "w_q_mem": w_q_mem[l], "w_o_mem": w_o_mem[l],
            "w_up": w_up[l], "w_down": w_down[l].astype(BF16),
        }
        assert n_att + ih == in_proj - o_att
        xp, kp, vp, kip, sfp, shp = _mixer(
            xp, b, s, pos_p, jnp.zeros((b, rp), F32),
            jnp.zeros((b, rd // RWKV_HEAD_DIM, RWKV_HEAD_DIM, RWKV_HEAD_DIM), F32), lw, dims,
            functools.partial(_attend_prompt, b=b, s=s, nh=nh, kvh=kvh, ih=ih))
        mn = _rmsnorm(mem_prompt.reshape(b * n_mem, d), norm_mem[l], BF16)
        mkp = _mm(mn, w_k_mem[l]).reshape(b, n_mem, mh * HEAD_DIM)
        mvp = _mm(mn, w_v_mem[l]).reshape(b, n_mem, mh * HEAD_DIM)
        xp = _cross_and_ffn(xp, b, s, mkp, mvp, lw, mh, n_mem)
        xs, ks_, vs_, kis, sfs, shs = _mixer(
            xs, db, ts, pos_s, state_shift[l], state_rwkv[l], lw, dims,
            functools.partial(_attend_sample, pool_k=pool_k, pool_v=pool_v, pool_idx=pool_idx,
                              page_table=page_table + l * n_pool, nh=nh, kvh=kvh, ih=ih))
        xs = _cross_and_ffn(xs, db, ts, mem_k[l * mem_rows:(l + 1) * mem_rows],
                            mem_v[l * mem_rows:(l + 1) * mem_rows], lw, mh, n_mem)
        vals = (kp.reshape(b, s, kvh, HEAD_DIM), vp.reshape(b, s, kvh, HEAD_DIM), kip.reshape(b, s, INDEX_DIM),
                mkp.reshape(b, n_mem, mh, HEAD_DIM), mvp.reshape(b, n_mem, mh, HEAD_DIM), sfp, shp,
                ks_.reshape(db, ts, kvh, HEAD_DIM), vs_.reshape(db, ts, kvh, HEAD_DIM),
                kis.reshape(db, ts, INDEX_DIM), sfs, shs)
        for lst, val in zip(outs, vals):
            lst.append(val)
    y_prompt = _rmsnorm(xp, final_norm, F32).reshape(b, s, d)
    y_sample = _rmsnorm(xs, final_norm, F32).reshape(db, ts, d)
    return (y_prompt, y_sample) + tuple(jnp.stack(o) for o in outs)
```

```python
import functools

import jax
import jax.numpy as jnp
from jax import lax
from jax.experimental import pallas as pl
from jax.experimental.pallas import tpu as pltpu

F32 = jnp.float32
BF16 = jnp.bfloat16
LANE = 128
SUBLANE = 8
VMEM_LIMIT = 56 * 1024 * 1024

RWKV_HEAD_DIM = 64
HEAD_DIM = 128
INDEX_DIM = 128
TOPK_MAX = 256
Q_BLOCK = 128
ROPE_THETA = 10000.0
NORM_EPS = 1e-6
RWKV_LN_EPS = 64e-5
IDX_LN_EPS = 1e-6
PREFIX_CHUNK = 256
COUNT_ROWS = 64
MAX_CAUSAL_BUCKETS = 8
PREP_STEPS = 32
NT_DIMS = (((1,), (1,)), ((), ()))


def _cparams(sem):
    return pltpu.CompilerParams(dimension_semantics=sem, vmem_limit_bytes=VMEM_LIMIT)


def _round_up(n, m):
    return (n + m - 1) // m * m


def _tile(n, pref, unit):
    best = None
    t = unit
    while t <= min(n, pref):
        if n % t == 0:
            best = t
        t += unit
    return best if best is not None else n


def _rmsnorm_kernel(x_ref, g_ref, o_ref):
    x = x_ref[...]
    ms = jnp.mean(x * x, axis=-1, keepdims=True)
    o_ref[...] = (x * lax.rsqrt(ms + NORM_EPS) * g_ref[...]).astype(o_ref.dtype)


def _rmsnorm(x, g, out_dtype):
    m, d = x.shape
    tm = _tile(m, 256, SUBLANE)
    return pl.pallas_call(
        _rmsnorm_kernel,
        grid=(m // tm,),
        in_specs=[pl.BlockSpec((tm, d), lambda i: (i, 0)), pl.BlockSpec((1, d), lambda i: (0, 0))],
        out_specs=pl.BlockSpec((tm, d), lambda i: (i, 0)),
        out_shape=jax.ShapeDtypeStruct((m, d), out_dtype),
        compiler_params=_cparams(("parallel",)),
        name="rmsnorm",
    )(x, g.reshape(1, d))


MXU_COLS = 256


def _col_tile(n, pref):
    return _tile(n, pref, MXU_COLS if n % MXU_COLS == 0 else LANE)


def _mm_kernel(x_ref, w_ref, o_ref):
    o_ref[...] = jnp.dot(x_ref[...], w_ref[...].astype(BF16), preferred_element_type=F32).astype(o_ref.dtype)


def _mm(x, w, n=None, out_dtype=F32, tm_pref=1024, tn_pref=768):
    m, k = x.shape
    n = w.shape[1] if n is None else n
    tm = _tile(m, tm_pref, SUBLANE)
    tn = _col_tile(n, tn_pref)
    return pl.pallas_call(
        _mm_kernel,
        grid=(m // tm, n // tn),
        in_specs=[pl.BlockSpec((tm, k), lambda i, j: (i, 0)), pl.BlockSpec((k, tn), lambda i, j: (0, j))],
        out_specs=pl.BlockSpec((tm, tn), lambda i, j: (i, j)),
        out_shape=jax.ShapeDtypeStruct((m, n), out_dtype),
        compiler_params=_cparams(("parallel", "parallel")),
        name="matmul",
    )(x, w)


def _mm_resid_kernel(x_ref, w_ref, r_ref, o_ref, acc_ref, *, nk):
    kk = pl.program_id(2)

    @pl.when(kk == 0)
    def _():
        acc_ref[...] = r_ref[...]

    acc_ref[...] += jnp.dot(x_ref[...], w_ref[...].astype(BF16), preferred_element_type=F32)

    @pl.when(kk == nk - 1)
    def _():
        o_ref[...] = acc_ref[...]


def _mm_resid(x, w, resid, tm_pref=1024, tn_pref=1024, tk_pref=2048):
    m, k = x.shape
    n = w.shape[1]
    tm = _tile(m, tm_pref, SUBLANE)
    tn = _tile(n, tn_pref, LANE)
    tk = _tile(k, tk_pref, LANE)
    nk = k // tk
    return pl.pallas_call(
        functools.partial(_mm_resid_kernel, nk=nk),
        grid=(m // tm, n // tn, nk),
        in_specs=[pl.BlockSpec((tm, tk), lambda i, j, kk: (i, kk)),
                  pl.BlockSpec((tk, tn), lambda i, j, kk: (kk, j)),
                  pl.BlockSpec((tm, tn), lambda i, j, kk: (i, j))],
        out_specs=pl.BlockSpec((tm, tn), lambda i, j, kk: (i, j)),
        out_shape=jax.ShapeDtypeStruct((m, n), F32),
        scratch_shapes=[pltpu.VMEM((tm, tn), F32)],
        compiler_params=_cparams(("parallel", "parallel", "arbitrary")),
        name="matmul_resid",
    )(x, w, resid)


def _swiglu_kernel(x_ref, wg_ref, wu_ref, o_ref):
    x = x_ref[...]
    g = jnp.dot(x, wg_ref[...].astype(BF16), preferred_element_type=F32)
    u = jnp.dot(x, wu_ref[...].astype(BF16), preferred_element_type=F32)
    o_ref[...] = (g * jax.nn.sigmoid(g) * u).astype(o_ref.dtype)


def _swiglu(x, w_up, tm_pref=1024, tn_pref=256):
    m, k = x.shape
    f = w_up.shape[1] // 2
    tm = _tile(m, tm_pref, SUBLANE)
    tn = _tile(f, tn_pref, LANE)
    nb = f // tn
    return pl.pallas_call(
        _swiglu_kernel,
        grid=(m // tm, nb),
        in_specs=[pl.BlockSpec((tm, k), lambda i, j: (i, 0)),
                  pl.BlockSpec((k, tn), lambda i, j: (0, j)),
                  pl.BlockSpec((k, tn), lambda i, j: (0, j + nb))],
        out_specs=pl.BlockSpec((tm, tn), lambda i, j: (i, j)),
        out_shape=jax.ShapeDtypeStruct((m, f), BF16),
        compiler_params=_cparams(("parallel", "parallel")),
        name="swiglu",
    )(x, w_up, w_up)


def _gate_mul_kernel(y_ref, g_ref, o_ref):
    o_ref[...] = (y_ref[...] * g_ref[...]).astype(o_ref.dtype)


def _gate_mul(y, g):
    m, n = y.shape
    tm = _tile(m, 256, SUBLANE)
    blk = pl.BlockSpec((tm, n), lambda i: (i, 0))
    return pl.pallas_call(
        _gate_mul_kernel,
        grid=(m // tm,),
        in_specs=[blk, blk],
        out_specs=blk,
        out_shape=jax.ShapeDtypeStruct((m, n), BF16),
        compiler_params=_cparams(("parallel",)),
        name="gate_mul",
    )(y, g)


def _merge_kernel(yr_ref, ya_ref, gr_ref, ga_ref, wr_ref, wa_ref, o_ref):
    br = jnp.dot(yr_ref[...], wr_ref[...].astype(BF16), preferred_element_type=F32)
    ba = jnp.dot(ya_ref[...], wa_ref[...].astype(BF16), preferred_element_type=F32)
    o_ref[...] = (jax.nn.sigmoid(gr_ref[...]) * br + jax.nn.sigmoid(ga_ref[...]) * ba).astype(o_ref.dtype)


def _merge(yr, ya, gates, w_br_r, w_br_a, tm_pref=1024, tn_pref=512):
    m, rd = yr.shape
    ad = ya.shape[1]
    d = w_br_r.shape[1]
    tm = _tile(m, tm_pref, SUBLANE)
    tn = _tile(d, tn_pref, LANE)
    nb = d // tn
    return pl.pallas_call(
        _merge_kernel,
        grid=(m // tm, nb),
        in_specs=[pl.BlockSpec((tm, rd), lambda i, j: (i, 0)),
                  pl.BlockSpec((tm, ad), lambda i, j: (i, 0)),
                  pl.BlockSpec((tm, tn), lambda i, j: (i, j)),
                  pl.BlockSpec((tm, tn), lambda i, j: (i, j + nb)),
                  pl.BlockSpec((rd, tn), lambda i, j: (0, j)),
                  pl.BlockSpec((ad, tn), lambda i, j: (0, j))],
        out_specs=pl.BlockSpec((tm, tn), lambda i, j: (i, j)),
        out_shape=jax.ShapeDtypeStruct((m, d), BF16),
        compiler_params=_cparams(("parallel", "parallel")),
        name="merge",
    )(yr, ya, gates, gates, w_br_r, w_br_a)


def _lane_segment(h):
    return lax.broadcasted_iota(jnp.int32, (SUBLANE, LANE), 1) // h


def _to_chain_tiles(x, out_ref, nbatch, tt, h):
    per = LANE // h
    seg = _lane_segment(h)
    for c in range(nbatch // per):
        for g in range(tt // SUBLANE):
            for col in range(RWKV_HEAD_DIM // per):
                tiles = [x[(c * per + q) * tt + g * SUBLANE:(c * per + q) * tt + (g + 1) * SUBLANE,
                           col * LANE:(col + 1) * LANE] for q in range(per)]
                for jj in range(per):
                    acc = None
                    for q in range(per):
                        sh = ((q - jj) % per) * h
                        t = tiles[q] if sh == 0 else pltpu.roll(tiles[q], sh, axis=1)
                        acc = t if acc is None else jnp.where(seg == q, t, acc)
                    n = col * per + jj
                    out_ref[g, n * SUBLANE:(n + 1) * SUBLANE, c * LANE:(c + 1) * LANE] = acc


def _rwkv_prep_kernel(p_ref, sh_ref, *rest, rd, lw, la, lgp, chain):
    if chain is not None:
        prev_ref, rest = rest[0], rest[1:]
    mu_ref, w0_ref, w2_ref, a0_ref, a2_ref, g2_ref, r_ref, w_ref, k_ref, v_ref, a_ref, g_ref = rest
    if chain is None:
        p = p_ref[...]
        shifted = sh_ref[...]
    else:
        nbatch, tt, h = chain
        first = pl.program_id(0) == 0
        row0 = lax.broadcasted_iota(jnp.int32, (tt, p_ref.shape[2]), 0) == 0
        ps, shs = [], []
        for bb in range(nbatch):
            pb = p_ref[bb]
            above = jnp.where(first, prev_ref[bb], sh_ref[bb, SUBLANE - 1:SUBLANE, :])
            ps.append(pb)
            shs.append(jnp.where(row0, above, pltpu.roll(pb, 1, axis=0)))
        p = jnp.concatenate(ps, axis=0)
        shifted = jnp.concatenate(shs, axis=0)
    xm = p + (shifted - p) * mu_ref[...]
    o = 0
    xr = xm[:, o:o + rd]; o += rd
    xw = xm[:, o:o + lw]; o += lw
    xk = xm[:, o:o + rd]; o += rd
    xv = xm[:, o:o + rd]; o += rd
    xa = xm[:, o:o + la]; o += la
    xg = xm[:, o:o + lgp]
    z = w0_ref[...] + jnp.dot(jnp.tanh(xw).astype(BF16), w2_ref[...], preferred_element_type=F32)
    sp = jnp.maximum(-z, 0.0) + jnp.log(1.0 + jnp.exp(-jnp.abs(z)))
    w_raw = -sp - 0.5
    decay = jnp.exp(-jnp.exp(w_raw))
    a = jax.nn.sigmoid(a0_ref[...] + jnp.dot(xa.astype(BF16), a2_ref[...], preferred_element_type=F32))
    g = jnp.dot(jax.nn.sigmoid(xg).astype(BF16), g2_ref[...], preferred_element_type=F32)
    if chain is None:
        r_ref[...] = xr
        w_ref[...] = decay
        k_ref[...] = xk
        v_ref[...] = xv
        a_ref[...] = a
        g_ref[...] = g
    else:
        for x, ref in ((xr, r_ref), (decay, w_ref), (xk, k_ref), (xv, v_ref), (a, a_ref)):
            _to_chain_tiles(x, ref, nbatch, tt, h)
        for bb in range(nbatch):
            g_ref[bb] = g[bb * tt:(bb + 1) * tt]


def _chain_kernels_fit(b, t, h):
    return LANE % h == 0 and b % (LANE // h) == 0 and t % PREP_STEPS == 0


def _rwkv_prep(p, prev, b, t, mu, w0, w2, a0, a2, g2, rd, lw, la, lgp):
    m, rpp = p.shape
    h = rd // RWKV_HEAD_DIM
    fix = lambda i: (0, 0)
    par_specs = [pl.BlockSpec((1, rpp), fix), pl.BlockSpec((1, rd), fix), pl.BlockSpec((lw, rd), fix),
                 pl.BlockSpec((1, rd), fix), pl.BlockSpec((la, rd), fix), pl.BlockSpec((lgp, rd), fix)]
    if not _chain_kernels_fit(b, t, h):
        shifted = jnp.concatenate([prev[:, None, :], p.reshape(b, t, rpp)[:, :-1]], axis=1).reshape(m, rpp)
        tm = _tile(m, 128, SUBLANE)
        row = lambda i: (i, 0)
        out = jax.ShapeDtypeStruct((m, rd), F32)
        return pl.pallas_call(
            functools.partial(_rwkv_prep_kernel, rd=rd, lw=lw, la=la, lgp=lgp, chain=None),
            grid=(m // tm,),
            in_specs=[pl.BlockSpec((tm, rpp), row), pl.BlockSpec((tm, rpp), row)] + par_specs,
            out_specs=[pl.BlockSpec((tm, rd), row)] * 6,
            out_shape=[out] * 6,
            compiler_params=_cparams(("parallel",)),
            name="rwkv_prep",
        )(p, shifted, mu, w0, w2, a0, a2, g2)
    tt = PREP_STEPS
    nc = b * h
    p3 = p.reshape(b, t, rpp)
    step_blk = lambda i: (0, i, 0)
    chain_out = jax.ShapeDtypeStruct((t // SUBLANE, RWKV_HEAD_DIM * SUBLANE, nc), F32)
    chain_spec = pl.BlockSpec((tt // SUBLANE, RWKV_HEAD_DIM * SUBLANE, nc), lambda i: (i, 0, 0))
    outs = pl.pallas_call(
        functools.partial(_rwkv_prep_kernel, rd=rd, lw=lw, la=la, lgp=lgp, chain=(b, tt, h)),
        grid=(t // tt,),
        in_specs=[pl.BlockSpec((b, tt, rpp), step_blk),
                  pl.BlockSpec((b, SUBLANE, rpp), lambda i: (0, jnp.maximum(i * (tt // SUBLANE) - 1, 0), 0)),
                  pl.BlockSpec((b, 1, rpp), lambda i: (0, 0, 0))] + par_specs,
        out_specs=[chain_spec] * 5 + [pl.BlockSpec((b, tt, rd), step_blk)],
        out_shape=[chain_out] * 5 + [jax.ShapeDtypeStruct((b, t, rd), F32)],
        compiler_params=_cparams(("parallel",)),
        name="rwkv_prep",
    )(p3, p3, prev.reshape(b, 1, rpp), mu, w0, w2, a0, a2, g2)
    return list(outs[:5]) + [outs[5].reshape(m, rd)]


def _rwkv_scan_kernel(r_ref, w_ref, k_ref, v_ref, a_ref, kkw_ref, ka_ref, rk_ref, lng_ref, lnb_ref,
                      s0_ref, y_ref, s_ref, kk_s, nb_s, kp_s, *, ts):
    n = RWKV_HEAD_DIM

    @pl.when(pl.program_id(1) == 0)
    def _():
        s_ref[...] = s0_ref[...]

    def tile(j):
        return slice(j * ts, (j + 1) * ts)

    ss = None
    for j in range(n):
        kk = k_ref[0, tile(j), :] * kkw_ref[j:j + 1, :]
        kk_s[tile(j), :] = kk
        ss = kk * kk if ss is None else ss + kk * kk
    inv = 1.0 / jnp.maximum(jnp.sqrt(ss), 1e-12)
    bonus = None
    for j in range(n):
        kj = k_ref[0, tile(j), :]
        aj = a_ref[0, tile(j), :]
        kk = kk_s[tile(j), :] * inv
        kk_s[tile(j), :] = kk
        nb_s[tile(j), :] = -(kk * aj)
        kp = kj * (1.0 + (aj - 1.0) * ka_ref[j:j + 1, :])
        kp_s[tile(j), :] = kp
        term = r_ref[0, tile(j), :] * kp * rk_ref[j:j + 1, :]
        bonus = term if bonus is None else bonus + term

    def row(ref, j, tt):
        return ref[j * ts + tt:j * ts + tt + 1, :]

    def row3(ref, j, tt):
        return ref[0, j * ts + tt:j * ts + tt + 1, :]

    lng = lng_ref[...]
    lnb = lnb_ref[...]
    sk = None
    for j in range(n):
        term = s_ref[j] * row(kk_s, j, 0)
        sk = term if sk is None else sk + term
    for tt in range(ts):
        v = v_ref[0, pl.ds(tt, n, stride=ts), :]
        y = None
        sk_next = None
        for j in range(n):
            sn = s_ref[j] * row3(w_ref, j, tt) + sk * row(nb_s, j, tt) + v * row(kp_s, j, tt)
            s_ref[j] = sn
            yt = sn * row3(r_ref, j, tt)
            y = yt if y is None else y + yt
            if tt + 1 < ts:
                st = sn * row(kk_s, j, tt + 1)
                sk_next = st if sk_next is None else sk_next + st
        sk = sk_next
        mu = jnp.mean(y, axis=0, keepdims=True)
        yc = y - mu
        var = jnp.mean(yc * yc, axis=0, keepdims=True)
        yn = yc * lax.rsqrt(var + RWKV_LN_EPS) * lng + lnb
        y_ref[0, pl.ds(tt, n, stride=ts), :] = yn + bonus[tt:tt + 1, :] * v


def _rwkv_scan(r, w, k, v, a, kkw, ka, rk, lng, lnb, s0):
    nt, rows, nc = r.shape
    n = RWKV_HEAD_DIM
    ts = rows // n
    seq = pl.BlockSpec((1, rows, LANE), lambda c, tt: (tt, 0, c))
    par = pl.BlockSpec((n, LANE), lambda c, tt: (0, c))
    st = pl.BlockSpec((n, n, LANE), lambda c, tt: (0, 0, c))
    return pl.pallas_call(
        functools.partial(_rwkv_scan_kernel, ts=ts),
        grid=(nc // LANE, nt),
        in_specs=[seq] * 5 + [par] * 5 + [st],
        out_specs=[seq, st],
        out_shape=[jax.ShapeDtypeStruct((nt, rows, nc), F32), jax.ShapeDtypeStruct((n, n, nc), F32)],
        scratch_shapes=[pltpu.VMEM((rows, LANE), F32)] * 3,
        compiler_params=_cparams(("parallel", "arbitrary")),
        name="rwkv_scan",
    )(r, w, k, v, a, kkw, ka, rk, lng, lnb, s0)


def _rwkv_post_kernel(y_ref, g_ref, o_ref, *, nbatch, tt, h):
    per = LANE // h
    seg = _lane_segment(h)
    pair = 2 * SUBLANE
    for c in range(nbatch // per):
        for q in range(per):
            bb = c * per + q
            for g2 in range(tt // pair):
                for col in range(RWKV_HEAD_DIM // per):
                    halves = []
                    for g in (2 * g2, 2 * g2 + 1):
                        acc = None
                        for ii in range(per):
                            i = col * per + ii
                            tile = y_ref[g, i * SUBLANE:(i + 1) * SUBLANE, c * LANE:(c + 1) * LANE]
                            sh = ((ii - q) % per) * h
                            t = tile if sh == 0 else pltpu.roll(tile, sh, axis=1)
                            acc = t if acc is None else jnp.where(seg == ii, t, acc)
                        halves.append(acc)
                    rows = slice(g2 * pair, (g2 + 1) * pair)
                    cols = slice(col * LANE, (col + 1) * LANE)
                    o_ref[bb, rows, cols] = (jnp.concatenate(halves, axis=0) * g_ref[bb, rows, cols]).astype(o_ref.dtype)


def _rwkv_post(y, g, b, t, h):
    rd = RWKV_HEAD_DIM * h
    tt = PREP_STEPS
    step_blk = lambda i: (0, i, 0)
    return pl.pallas_call(
        functools.partial(_rwkv_post_kernel, nbatch=b, tt=tt, h=h),
        grid=(t // tt,),
        in_specs=[pl.BlockSpec((tt // SUBLANE, RWKV_HEAD_DIM * SUBLANE, b * h), lambda i: (i, 0, 0)),
                  pl.BlockSpec((b, tt, rd), step_blk)],
        out_specs=pl.BlockSpec((b, tt, rd), step_blk),
        out_shape=jax.ShapeDtypeStruct((b, t, rd), BF16),
        compiler_params=_cparams(("parallel",)),
        name="rwkv_post",
    )(y, g.reshape(b, t, rd)).reshape(b * t, rd)


def _rope(x, cos2, sin2):
    return x * cos2 + pltpu.roll(x, HEAD_DIM // 2, axis=1) * sin2


def _attn_prep_kernel(a_ref, cos_ref, sin_ref, lng_ref, lnb_ref,
                      q_ref, k_ref, v_ref, qi_ref, ki_ref, wi_ref, *, nh, kvh, ih, wscale):
    cos2 = cos_ref[...]
    sin2 = sin_ref[...]
    o = 0
    for h in range(nh):
        q_ref[:, h * HEAD_DIM:(h + 1) * HEAD_DIM] = _rope(a_ref[:, o:o + HEAD_DIM], cos2, sin2).astype(BF16)
        o += HEAD_DIM
    for h in range(kvh):
        k_ref[:, h * HEAD_DIM:(h + 1) * HEAD_DIM] = _rope(a_ref[:, o:o + HEAD_DIM], cos2, sin2)
        o += HEAD_DIM
    v_ref[...] = a_ref[:, o:o + kvh * HEAD_DIM]
    o += kvh * HEAD_DIM
    for h in range(ih):
        qi_ref[:, h * INDEX_DIM:(h + 1) * INDEX_DIM] = _rope(a_ref[:, o:o + INDEX_DIM], cos2, sin2).astype(BF16)
        o += INDEX_DIM
    x = a_ref[:, o:o + INDEX_DIM]
    o += INDEX_DIM
    mu = jnp.mean(x, axis=-1, keepdims=True)
    xc = x - mu
    var = jnp.mean(xc * xc, axis=-1, keepdims=True)
    kin = xc * lax.rsqrt(var + IDX_LN_EPS) * lng_ref[...] + lnb_ref[...]
    ki_ref[...] = _rope(kin, cos2, sin2)
    wi_ref[...] = a_ref[:, o:o + LANE] * wscale


def _attn_prep(a, cos2, sin2, lng, lnb, nh, kvh, ih):
    m, na = a.shape
    tm = _tile(m, 128, SUBLANE)
    row = lambda i: (i, 0)
    fix = lambda i: (0, 0)
    ad, kvd, idd = nh * HEAD_DIM, kvh * HEAD_DIM, ih * INDEX_DIM
    wscale = float(ih) ** -0.5 * float(INDEX_DIM) ** -0.5
    return pl.pallas_call(
        functools.partial(_attn_prep_kernel, nh=nh, kvh=kvh, ih=ih, wscale=wscale),
        grid=(m // tm,),
        in_specs=[pl.BlockSpec((tm, na), row), pl.BlockSpec((tm, LANE), row), pl.BlockSpec((tm, LANE), row),
                  pl.BlockSpec((1, LANE), fix), pl.BlockSpec((1, LANE), fix)],
        out_specs=[pl.BlockSpec((tm, ad), row), pl.BlockSpec((tm, kvd), row), pl.BlockSpec((tm, kvd), row),
                   pl.BlockSpec((tm, idd), row),
                   pl.BlockSpec((tm, INDEX_DIM), row), pl.BlockSpec((tm, LANE), row)],
        out_shape=[jax.ShapeDtypeStruct((m, ad), BF16), jax.ShapeDtypeStruct((m, kvd), F32),
                   jax.ShapeDtypeStruct((m, kvd), F32), jax.ShapeDtypeStruct((m, idd), BF16),
                   jax.ShapeDtypeStruct((m, INDEX_DIM), F32),
                   jax.ShapeDtypeStruct((m, LANE), F32)],
        compiler_params=_cparams(("parallel",)),
        name="attn_prep",
    )(a, cos2, sin2, lng, lnb)


def _count_rows(ones):
    l = ones.shape[0]
    if l % COUNT_ROWS == 0 and l > COUNT_ROWS:
        ones = jnp.sum(ones.reshape(l // COUNT_ROWS, COUNT_ROWS, LANE), axis=0)
    return jnp.sum(ones, axis=0, keepdims=True)


def _topk_mask(score, key_ref, n_sel):
    l = score.shape[0]
    bits = lax.bitcast_convert_type(score, jnp.int32)
    key_ref[...] = jnp.where(bits < 0, bits ^ jnp.int32(0x7FFFFFFF), bits)
    sign = jnp.int32(-2 ** 31)
    nf = jnp.float32(n_sel)

    def body(i, ans):
        cand = ans | lax.shift_left(jnp.int32(1), 31 - i)
        cnt = _count_rows(jnp.where(key_ref[...] >= (cand ^ sign), 1.0, 0.0))
        return jnp.where(cnt >= nf, cand, ans)

    ans = lax.fori_loop(0, 32, body, jnp.zeros((1, LANE), jnp.int32))
    thr = ans ^ sign
    need = nf - _count_rows(jnp.where(key_ref[...] > thr, 1.0, 0.0))
    ch = PREFIX_CHUNK
    tri = jnp.where(lax.broadcasted_iota(jnp.int32, (ch, ch), 0) > lax.broadcasted_iota(jnp.int32, (ch, ch), 1),
                    1.0, 0.0).astype(BF16)
    run = jnp.zeros((1, LANE), F32)
    ranks = []
    for c in range(l // ch):
        e = jnp.where(key_ref[c * ch:(c + 1) * ch, :] == thr, 1.0, 0.0)
        ranks.append(jnp.dot(tri, e.astype(BF16), preferred_element_type=F32) + run)
        run = run + jnp.sum(e, axis=0, keepdims=True)
    rank = jnp.concatenate(ranks, axis=0)
    key = key_ref[...]
    return jnp.logical_or(key > thr, jnp.logical_and(key == thr, rank < need))


def _dsa_prompt_kernel(qi_ref, wit_ref, ki_ref, q_ref, k_ref, v_ref, o_ref, sc_ref, key_ref,
                       *, n_sel, ih, nh, kvh, qb0):
    l = ki_ref.shape[1]
    qb = qb0 + pl.program_id(1)
    lhs = ki_ref[0].astype(BF16)
    for hp in range(ih // 2):
        c0 = 2 * hp * INDEX_DIM
        c1 = c0 + INDEX_DIM
        rhs = jnp.concatenate([qi_ref[:, c0:c1], qi_ref[:, c1:c1 + INDEX_DIM]], axis=0)
        rel = jnp.maximum(lax.dot_general(lhs, rhs, NT_DIMS, preferred_element_type=F32), 0.0)
        w0 = wit_ref[0, 0, 2 * hp:2 * hp + 1, :]
        w1 = wit_ref[0, 0, 2 * hp + 1:2 * hp + 2, :]
        part = rel[:, :Q_BLOCK] * w0 + rel[:, Q_BLOCK:] * w1
        if hp == 0:
            sc_ref[...] = part
        else:
            sc_ref[...] += part
    s_idx = lax.broadcasted_iota(jnp.int32, (l, Q_BLOCK), 0)
    q_pos = qb * Q_BLOCK + lax.broadcasted_iota(jnp.int32, (l, Q_BLOCK), 1)
    causal = s_idx <= q_pos
    sel = _topk_mask(jnp.where(causal, sc_ref[...], -jnp.inf), key_ref, n_sel)
    bias = jnp.where(jnp.logical_and(sel, causal), 0.0, -jnp.inf).T
    c_exp = float(HEAD_DIM) ** -0.5 * 1.4426950408889634
    rep = nh // kvh
    for g in range(kvh):
        kg = k_ref[0, :, g * HEAD_DIM:(g + 1) * HEAD_DIM].astype(BF16)
        vg = v_ref[0, :, g * HEAD_DIM:(g + 1) * HEAD_DIM].astype(BF16)
        h0 = g * rep
        qg = jnp.concatenate([q_ref[:, (h0 + r) * HEAD_DIM:(h0 + r + 1) * HEAD_DIM] for r in range(rep)], axis=0)
        lg = lax.dot_general(qg, kg, NT_DIMS, preferred_element_type=F32)
        ps, dens = [], []
        for r in range(rep):
            x = lg[r * Q_BLOCK:(r + 1) * Q_BLOCK] + bias
            p = jnp.exp2((x - jnp.max(x, axis=-1, keepdims=True)) * c_exp)
            dens.append(jnp.sum(p, axis=-1, keepdims=True))
            ps.append(p.astype(BF16))
        og = jnp.dot(jnp.concatenate(ps, axis=0), vg, preferred_element_type=F32)
        for r in range(rep):
            o = og[r * Q_BLOCK:(r + 1) * Q_BLOCK] / dens[r]
            o_ref[0, :, (h0 + r) * HEAD_DIM:(h0 + r + 1) * HEAD_DIM] = o.astype(o_ref.dtype)


def _causal_buckets(nq):
    for n in range(MAX_CAUSAL_BUCKETS, 0, -1):
        if nq % n == 0 and (nq // n * Q_BLOCK) % PREFIX_CHUNK == 0:
            return n
    return 1


def _dsa_prompt(qi, wit, ki, q, k, v, b, s, n_sel, ih, nh, kvh):
    nq = s // Q_BLOCK
    nbk = _causal_buckets(nq)
    per = nq // nbk
    per_b = lambda bb, qq: (bb, 0, 0)
    outs = []
    for u in range(nbk):
        le = (u + 1) * per * Q_BLOCK
        row = functools.partial(lambda bb, qq, u: (bb * nq + u * per + qq, 0), u=u)
        wrow = functools.partial(lambda bb, qq, u: (bb, u * per + qq, 0, 0), u=u)
        outs.append(pl.pallas_call(
            functools.partial(_dsa_prompt_kernel, n_sel=n_sel, ih=ih, nh=nh, kvh=kvh, qb0=u * per),
            grid=(b, per),
            in_specs=[pl.BlockSpec((Q_BLOCK, ih * INDEX_DIM), row),
                      pl.BlockSpec((1, 1, ih, Q_BLOCK), wrow),
                      pl.BlockSpec((1, le, INDEX_DIM), per_b),
                      pl.BlockSpec((Q_BLOCK, nh * HEAD_DIM), row),
                      pl.BlockSpec((1, le, kvh * HEAD_DIM), per_b), pl.BlockSpec((1, le, kvh * HEAD_DIM), per_b)],
            out_specs=pl.BlockSpec((1, Q_BLOCK, nh * HEAD_DIM), lambda bb, qq: (bb, qq, 0)),
            out_shape=jax.ShapeDtypeStruct((b, per * Q_BLOCK, nh * HEAD_DIM), BF16),
            scratch_shapes=[pltpu.VMEM((le, Q_BLOCK), F32), pltpu.VMEM((le, Q_BLOCK), jnp.int32)],
            compiler_params=_cparams(("parallel", "arbitrary")),
            name="dsa_prompt",
        )(qi, wit, ki, q, k, v))
    return jnp.concatenate(outs, axis=1).reshape(b * s, nh * HEAD_DIM)


def _dsa_score_kernel(pt_ref, qi_ref, wi_ref, kin_ref, *rest, n_pages):
    pages = rest[:n_pages]
    past_ref, new_ref = rest[n_pages:]
    qi = qi_ref[0]
    wcol = wi_ref[0]
    for p in range(n_pages):
        rel = lax.dot_general(qi, pages[p][0].astype(BF16), NT_DIMS, preferred_element_type=F32)
        past_ref[0, p] = jnp.sum(jnp.maximum(rel, 0.0) * wcol, axis=0, keepdims=True)
    kin = kin_ref[0].astype(BF16).astype(F32)
    rel_new = jnp.maximum(jnp.sum(qi.astype(F32) * kin, axis=-1, keepdims=True), 0.0)
    new = jnp.sum(rel_new * wcol, axis=0, keepdims=True)
    new_ref[0] = jnp.broadcast_to(new, (1, LANE))


def _dsa_scores(page_table, qi, wi, ki_new, pool_idx):
    db, n_pages = page_table.shape
    ih = qi.shape[1]
    psz = pool_idx.shape[1]
    per_b = lambda bb, pt: (bb, 0, 0)
    page_specs = [pl.BlockSpec((1, psz, INDEX_DIM), functools.partial(lambda bb, pt, p: (pt[bb, p], 0, 0), p=p))
                  for p in range(n_pages)]
    grid_spec = pltpu.PrefetchScalarGridSpec(
        num_scalar_prefetch=1,
        grid=(db,),
        in_specs=[pl.BlockSpec((1, ih, INDEX_DIM), per_b),
                  pl.BlockSpec((1, ih, 1), per_b), pl.BlockSpec((1, 1, INDEX_DIM), per_b)] + page_specs,
        out_specs=[pl.BlockSpec((1, n_pages, 1, psz), lambda bb, pt: (bb, 0, 0, 0)),
                   pl.BlockSpec((1, 1, LANE), per_b)],
    )
    return pl.pallas_call(
        functools.partial(_dsa_score_kernel, n_pages=n_pages),
        grid_spec=grid_spec,
        out_shape=[jax.ShapeDtypeStruct((db, n_pages, 1, psz), F32), jax.ShapeDtypeStruct((db, 1, LANE), F32)],
        compiler_params=_cparams(("arbitrary",)),
        name="dsa_sample_scores",
    )(page_table, qi, wi, ki_new, *([pool_idx] * n_pages))


def _mask_kernel(sc_ref, m_ref, key_ref, *, n_sel):
    m_ref[...] = jnp.where(_topk_mask(sc_ref[...], key_ref, n_sel), 1.0, 0.0)


def _select_mask(score_t, n_sel):
    l, n = score_t.shape
    blk = pl.BlockSpec((l, LANE), lambda i: (0, i))
    return pl.pallas_call(
        functools.partial(_mask_kernel, n_sel=n_sel),
        grid=(n // LANE,),
        in_specs=[blk],
        out_specs=blk,
        out_shape=jax.ShapeDtypeStruct((l, n), F32),
        scratch_shapes=[pltpu.VMEM((l, LANE), jnp.int32)],
        compiler_params=_cparams(("parallel",)),
        name="select_mask",
    )(score_t)


def _dsa_sample_kernel(pt_ref, q_ref, bias_ref, kn_ref, vn_ref, *rest, n_pages, nh, kvh):
    kpages = rest[:n_pages]
    vpages = rest[n_pages:2 * n_pages]
    o_ref = rest[2 * n_pages]
    rows = kpages[0].shape[0]
    rep = nh // kvh
    scale = float(HEAD_DIM) ** -0.5
    q = q_ref[0]

    def own(width):
        col = lax.broadcasted_iota(jnp.int32, (nh, width), 1)
        head = lax.broadcasted_iota(jnp.int32, (nh, width), 0)
        return col % kvh == head // rep

    own_page = own(rows)
    lgs = []
    for p in range(n_pages):
        lg = lax.dot_general(q, kpages[p][...].astype(BF16), NT_DIMS, preferred_element_type=F32)
        lgs.append(jnp.where(own_page, lg * scale + bias_ref[0, :, p * rows:(p + 1) * rows], -jnp.inf))
    lg_new = lax.dot_general(q, kn_ref[0].astype(BF16), NT_DIMS, preferred_element_type=F32)
    lg_new = jnp.where(own(LANE), lg_new * scale + bias_ref[0, :, n_pages * rows:n_pages * rows + LANE], -jnp.inf)
    m = jnp.max(lg_new, axis=-1, keepdims=True)
    for lg in lgs:
        m = jnp.maximum(m, jnp.max(lg, axis=-1, keepdims=True))
    p_new = jnp.exp(lg_new - m)
    den = jnp.sum(p_new, axis=-1, keepdims=True)
    acc = jnp.dot(p_new.astype(BF16), vn_ref[0].astype(BF16), preferred_element_type=F32)
    for p in range(n_pages):
        pp = jnp.exp(lgs[p] - m)
        den = den + jnp.sum(pp, axis=-1, keepdims=True)
        acc = acc + jnp.dot(pp.astype(BF16), vpages[p][...].astype(BF16), preferred_element_type=F32)
    o_ref[0] = (acc / den).astype(o_ref.dtype)


def _dsa_sample(page_table, q, bias, k_new, v_new, pool_k, pool_v, nh, kvh, psz):
    db, n_pages = page_table.shape
    rows = psz * kvh
    lb = bias.shape[2]
    per_b = lambda bb, pt: (bb, 0, 0)
    page_specs = [pl.BlockSpec((rows, HEAD_DIM), functools.partial(lambda bb, pt, p: (pt[bb, p], 0), p=p))
                  for p in range(n_pages)]
    grid_spec = pltpu.PrefetchScalarGridSpec(
        num_scalar_prefetch=1,
        grid=(db,),
        in_specs=[pl.BlockSpec((1, nh, HEAD_DIM), per_b), pl.BlockSpec((1, 1, lb), per_b),
                  pl.BlockSpec((1, LANE, HEAD_DIM), per_b), pl.BlockSpec((1, LANE, HEAD_DIM), per_b)]
        + page_specs + page_specs,
        out_specs=pl.BlockSpec((1, nh, HEAD_DIM), per_b),
    )
    return pl.pallas_call(
        functools.partial(_dsa_sample_kernel, n_pages=n_pages, nh=nh, kvh=kvh),
        grid_spec=grid_spec,
        out_shape=jax.ShapeDtypeStruct((db, nh, HEAD_DIM), BF16),
        compiler_params=_cparams(("arbitrary",)),
        name="dsa_sample_attend",
    )(page_table, q, bias, k_new, v_new, *([pool_k] * n_pages), *([pool_v] * n_pages))


def _cross_kernel(q_ref, mk_ref, mv_ref, o_ref, *, mh, tq):
    scale = float(HEAD_DIM) ** -0.5
    for h in range(mh):
        cs = slice(h * HEAD_DIM, (h + 1) * HEAD_DIM)
        qh = q_ref[0, :, cs].astype(BF16)
        if tq < SUBLANE:
            qh = jnp.broadcast_to(qh[0:1], (SUBLANE, HEAD_DIM))
        kh = mk_ref[0, :, cs].astype(BF16)
        vh = mv_ref[0, :, cs].astype(BF16)
        lg = lax.dot_general(qh, kh, NT_DIMS, preferred_element_type=F32) * scale
        p = jnp.exp(lg - jnp.max(lg, axis=-1, keepdims=True))
        den = jnp.sum(p, axis=-1, keepdims=True)
        o = jnp.dot(p.astype(BF16), vh, preferred_element_type=F32) / den
        o_ref[0, :, cs] = o[0:tq].astype(o_ref.dtype)


def _cross_attend(q, mk, mv, mh):
    b, t, md = q.shape
    nm = mk.shape[1]
    tq = _tile(t, 512, SUBLANE)
    return pl.pallas_call(
        functools.partial(_cross_kernel, mh=mh, tq=tq),
        grid=(b, t // tq),
        in_specs=[pl.BlockSpec((1, tq, md), lambda bb, i: (bb, i, 0)),
                  pl.BlockSpec((1, nm, md), lambda bb, i: (bb, 0, 0)),
                  pl.BlockSpec((1, nm, md), lambda bb, i: (bb, 0, 0))],
        out_specs=pl.BlockSpec((1, tq, md), lambda bb, i: (bb, i, 0)),
        out_shape=jax.ShapeDtypeStruct((b, t, md), BF16),
        compiler_params=_cparams(("parallel", "parallel")),
        name="cross_attend",
    )(q, mk, mv)


def _cross_rows_kernel(q_ref, mk_ref, mv_ref, o_ref, *, mh):
    scale = float(HEAD_DIM) ** -0.5
    rows = mk_ref.shape[0]
    q = q_ref[0].astype(BF16)
    lg = lax.dot_general(q, mk_ref[...].astype(BF16), NT_DIMS, preferred_element_type=F32) * scale
    col = lax.broadcasted_iota(jnp.int32, (SUBLANE, rows), 1)
    head = lax.broadcasted_iota(jnp.int32, (SUBLANE, rows), 0)
    lg = jnp.where(col % mh == head % mh, lg, -jnp.inf)
    p = jnp.exp(lg - jnp.max(lg, axis=-1, keepdims=True))
    den = jnp.sum(p, axis=-1, keepdims=True)
    o = jnp.dot(p.astype(BF16), mv_ref[...].astype(BF16), preferred_element_type=F32) / den
    o_ref[0] = o.astype(o_ref.dtype)


def _cross_attend_rows(q, mk, mv, mh, n_mem):
    b = q.shape[0]
    assert mh <= SUBLANE
    q8 = jnp.pad(q.reshape(b, mh, HEAD_DIM), ((0, 0), (0, SUBLANE - mh), (0, 0)))
    rows = n_mem * mh
    o = pl.pallas_call(
        functools.partial(_cross_rows_kernel, mh=mh),
        grid=(b,),
        in_specs=[pl.BlockSpec((1, SUBLANE, HEAD_DIM), lambda bb: (bb, 0, 0)),
                  pl.BlockSpec((rows, HEAD_DIM), lambda bb: (bb, 0)),
                  pl.BlockSpec((rows, HEAD_DIM), lambda bb: (bb, 0))],
        out_specs=pl.BlockSpec((1, SUBLANE, HEAD_DIM), lambda bb: (bb, 0, 0)),
        out_shape=jax.ShapeDtypeStruct((b, SUBLANE, HEAD_DIM), BF16),
        compiler_params=_cparams(("parallel",)),
        name="cross_attend_rows",
    )(q8, mk, mv)
    return o[:, :mh].reshape(b, mh * HEAD_DIM)


def _head_major_to_channel_major(x, h):
    lead = x.shape[:-1]
    return x.reshape(lead + (h, RWKV_HEAD_DIM)).swapaxes(-1, -2).reshape(x.shape)


def _channel_major_to_head_major(x, h):
    lead = x.shape[:-1]
    return x.reshape(lead + (RWKV_HEAD_DIM, h)).swapaxes(-1, -2).reshape(x.shape)


def _rwkv_cols(x, h, rd, lwd, lad, fn):
    o_k = rd + lwd
    o_a = o_k + 2 * rd
    return jnp.concatenate([fn(x[..., :rd], h), x[..., rd:o_k], fn(x[..., o_k:o_k + rd], h),
                            fn(x[..., o_k + rd:o_a], h), x[..., o_a:]], axis=-1)


def _to_chains(x, b, t, h, ts):
    nc = b * h
    y = x.reshape(b, t // ts, ts, RWKV_HEAD_DIM, h).transpose(1, 3, 2, 0, 4)
    y = y.reshape(t // ts, RWKV_HEAD_DIM * ts, nc)
    return jnp.pad(y, ((0, 0), (0, 0), (0, _round_up(nc, LANE) - nc)))


def _from_chains(y, b, t, h, ts):
    nc = b * h
    y = y[:, :, :nc].reshape(t // ts, RWKV_HEAD_DIM, ts, b, h).transpose(3, 0, 2, 1, 4)
    return y.reshape(b * t, RWKV_HEAD_DIM * h)


def _param_chains(p, b, h):
    nc = b * h
    y = jnp.tile(p.reshape(h, RWKV_HEAD_DIM).T, (1, b))
    return jnp.pad(y, ((0, 0), (0, _round_up(nc, LANE) - nc)))


def _rope_tables(pos):
    half = HEAD_DIM // 2
    inv = jnp.power(ROPE_THETA, -jnp.arange(half, dtype=F32) / half)
    ang = pos.astype(F32)[:, None] * inv[None, :]
    cos, sin = jnp.cos(ang), jnp.sin(ang)
    return jnp.concatenate([cos, cos], axis=-1), jnp.concatenate([-sin, sin], axis=-1)


def _mixer_front(x, b, t, pos, lw, dims):
    d, rd, lwd, lad, lgp, rp, rpp, nh, kvh, ih = dims
    m = b * t
    hn = _rmsnorm(x, lw["norm_mix"], BF16)
    gates = _mm(hn, lw["w_in"], n=2 * d, tn_pref=512)
    p = _mm(hn, lw["w_rwkv"])
    att = _mm(hn, lw["w_attn"])
    cos2, sin2 = _rope_tables(pos)
    cos2 = jnp.broadcast_to(cos2[None], (b, t, LANE)).reshape(m, LANE)
    sin2 = jnp.broadcast_to(sin2[None], (b, t, LANE)).reshape(m, LANE)
    q, kr, vv, qi, ki, wi = _attn_prep(att, cos2, sin2, lw["idx_ln_g"], lw["idx_ln_b"], nh, kvh, ih)
    return {"gates": gates, "p": p, "q": q, "k": kr, "v": vv, "qi": qi, "ki": ki, "wi": wi}


def _rwkv_branch(p, b, t, shift_prev, s0, lw, dims):
    d, rd, lwd, lad, lgp, rp, rpp, nh, kvh, ih = dims
    h = rd // RWKV_HEAD_DIM
    p3 = p.reshape(b, t, rpp)
    prev = jnp.pad(_rwkv_cols(shift_prev, h, rd, lwd, lad, _head_major_to_channel_major), ((0, 0), (0, rpp - rp)))
    r, w, k, v, a, g = _rwkv_prep(p, prev, b, t, lw["mu"], lw["w0"], lw["w2"], lw["a0"], lw["a2"], lw["g2"],
                                  rd, lwd, lad, lgp)
    nc = b * h
    in_kernel = _chain_kernels_fit(b, t, h)
    ts = SUBLANE if t % SUBLANE == 0 else 1
    if not in_kernel:
        r, w, k, v, a = [_to_chains(z, b, t, h, ts) for z in (r, w, k, v, a)]
    s0c = jnp.pad(s0.reshape(nc, RWKV_HEAD_DIM, RWKV_HEAD_DIM).transpose(2, 1, 0),
                  ((0, 0), (0, 0), (0, _round_up(nc, LANE) - nc)))
    yc, sc = _rwkv_scan(r, w, k, v, a,
                        _param_chains(lw["kk"], b, h), _param_chains(lw["ka"], b, h),
                        _param_chains(lw["rk"], b, h), _param_chains(lw["ln_g"], b, h),
                        _param_chains(lw["ln_b"], b, h), s0c)
    if in_kernel:
        yg = _rwkv_post(yc, g, b, t, h)
    else:
        yg = _gate_mul(_from_chains(yc, b, t, h, ts), g)
    s_fin = sc[:, :, :nc].transpose(2, 1, 0).reshape(b, h, RWKV_HEAD_DIM, RWKV_HEAD_DIM)
    shift_new = _rwkv_cols(p3[:, -1, :rp], h, rd, lwd, lad, _channel_major_to_head_major)
    return yg, s_fin, shift_new


def _mixer_back(x, f, yg, ya, lw):
    merged = _merge(yg, ya, f["gates"], lw["w_br_rwkv"], lw["w_br_attn"])
    return _mm_resid(merged, lw["w_out"], x)


def _attend_prompt(f, b, s, nh, kvh, ih):
    n_sel = min(TOPK_MAX, s // 4)
    nq = s // Q_BLOCK
    wit = f["wi"][:, :ih].reshape(b, nq, Q_BLOCK, ih).transpose(0, 1, 3, 2)
    return _dsa_prompt(f["qi"], wit, f["ki"].reshape(b, s, INDEX_DIM), f["q"],
                       f["k"].reshape(b, s, kvh * HEAD_DIM), f["v"].reshape(b, s, kvh * HEAD_DIM),
                       b, s, n_sel, ih, nh, kvh)


def _attend_sample(f, pool_k, pool_v, pool_idx, page_table, nh, kvh, ih):
    q, qi, wi, kr, vv, ki = f["q"], f["qi"], f["wi"], f["k"], f["v"], f["ki"]
    db, n_pages = page_table.shape
    psz = pool_idx.shape[1]
    past = n_pages * psz
    n_sel = min(TOPK_MAX, (past + 1) // 4)
    kvd = kvh * HEAD_DIM
    sc_past, sc_new = _dsa_scores(page_table, qi.reshape(db, ih, INDEX_DIM),
                                  wi[:, :ih].reshape(db, ih, 1), ki.reshape(db, 1, INDEX_DIM), pool_idx)
    lp = _round_up(past + 1, PREFIX_CHUNK)
    dbp = _round_up(db, LANE)
    assert (lp - past) * kvh >= LANE
    score = jnp.concatenate([sc_past.reshape(db, past), sc_new[:, 0, :1]], axis=1)
    score_t = jnp.pad(score.T, ((0, lp - past - 1), (0, dbp - db)), constant_values=-jnp.inf)
    mask = _select_mask(score_t, n_sel)[:, :db].T
    bias = jnp.repeat(jnp.where(mask > 0.5, 0.0, -jnp.inf).astype(F32), kvh, axis=1).reshape(db, 1, lp * kvh)
    new_rows = lambda x: jnp.pad(x.reshape(db, kvh, HEAD_DIM), ((0, 0), (0, LANE - kvh), (0, 0)))
    o = _dsa_sample(page_table, q.reshape(db, nh, HEAD_DIM), bias, new_rows(kr), new_rows(vv),
                    pool_k, pool_v, nh, kvh, psz)
    return o.reshape(db, nh * HEAD_DIM)


def _cross_and_ffn(x, b, t, mk, mv, lw, mh, n_mem):
    hc = _rmsnorm(x, lw["norm_cross"], BF16)
    qm = _mm(hc, lw["w_q_mem"])
    if mk.ndim == 2:
        o = _cross_attend_rows(qm, mk, mv, mh, n_mem)
    else:
        o = _cross_attend(qm.reshape(b, t, -1), mk, mv, mh).reshape(b * t, -1)
    x2 = _mm_resid(o, lw["w_o_mem"], x)
    hf = _rmsnorm(x2, lw["norm_ffn"], BF16)
    act = _swiglu(hf, lw["w_up"])
    return _mm_resid(act, lw["w_down"], x2, tn_pref=512, tk_pref=5504)


def kernel(x_prompt, mem_prompt, x_sample, cache_k, cache_v, cache_idx_k, cache_mem_k, cache_mem_v, state_rwkv, state_shift, page_table, norm_mix, w_in, rwkv_mu, rwkv_w0, rwkv_w2, rwkv_a0, rwkv_a2, rwkv_g2, rwkv_kk, rwkv_ka, rwkv_rk, rwkv_ln_g, rwkv_ln_b, idx_ln_g, idx_ln_b, w_br_rwkv, w_br_attn, w_out, norm_cross, norm_mem, w_q_mem, w_k_mem, w_v_mem, w_o_mem, norm_ffn, w_up, w_down, final_norm):
    b, s, d = x_prompt.shape
    db, ts, _ = x_sample.shape
    assert ts == 1, "the sample group decodes one token per sequence"
    depth = w_in.shape[0]
    in_proj = w_in.shape[2]
    rp = rwkv_mu.shape[1]
    lwd, rd = rwkv_w2.shape[1:]
    lad = rwkv_a2.shape[1]
    lg = rwkv_g2.shape[1]
    lgp = _round_up(lg, LANE)
    rpp = rp - lg + lgp
    ad = w_br_attn.shape[1]
    nh = ad // HEAD_DIM
    kvh = cache_k.shape[3]
    kvd = kvh * HEAD_DIM
    ih = (in_proj - 2 * d - rp - ad - 2 * kvd - INDEX_DIM) // (INDEX_DIM + 1)
    mh = cache_mem_k.shape[3]
    n_mem = mem_prompt.shape[1]
    past = page_table.shape[1] * cache_k.shape[2]
    assert rp == 3 * rd + lwd + lad + lg and s % Q_BLOCK == 0 and ih % 2 == 0
    dims = (d, rd, lwd, lad, lgp, rp, rpp, nh, kvh, ih)
    n_pool, psz = cache_k.shape[1:3]
    pool_k = cache_k.reshape(depth * n_pool * psz * kvh, HEAD_DIM)
    pool_v = cache_v.reshape(depth * n_pool * psz * kvh, HEAD_DIM)
    pool_idx = cache_idx_k.reshape(depth * n_pool, psz, INDEX_DIM)
    mem_rows = db * n_mem * mh
    mem_k = cache_mem_k.reshape(depth * mem_rows, HEAD_DIM)
    mem_v = cache_mem_v.reshape(depth * mem_rows, HEAD_DIM)
    hr = rd // RWKV_HEAD_DIM
    cm = functools.partial(_head_major_to_channel_major, h=hr)

    xp = x_prompt.reshape(b * s, d)
    xs = x_sample.reshape(db, d)
    pos_p = jnp.arange(s, dtype=jnp.int32)
    pos_s = past + jnp.arange(ts, dtype=jnp.int32)
    outs = [[] for _ in range(12)]
    for l in range(depth):
        wl = w_in[l]
        o_att = 2 * d + rp
        n_att = ad + 2 * kvd + ih * INDEX_DIM + INDEX_DIM
        lw = {
            "norm_mix": norm_mix[l],
            "w_in": wl,
            "w_rwkv": jnp.pad(_rwkv_cols(wl[:, 2 * d:o_att], hr, rd, lwd, lad, _head_major_to_channel_major),
                              ((0, 0), (0, rpp - rp))).astype(BF16),
            "w_attn": jnp.pad(wl[:, o_att:], ((0, 0), (0, LANE - ih))).astype(BF16),
            "mu": jnp.pad(_rwkv_cols(rwkv_mu[l], hr, rd, lwd, lad, _head_major_to_channel_major),
                          (0, rpp - rp)).reshape(1, rpp),
            "w0": cm(rwkv_w0[l]).reshape(1, rd), "w2": cm(rwkv_w2[l]).astype(BF16),
            "a0": cm(rwkv_a0[l]).reshape(1, rd), "a2": cm(rwkv_a2[l]).astype(BF16),
            "g2": jnp.pad(cm(rwkv_g2[l]), ((0, lgp - lg), (0, 0))).astype(BF16),
            "kk": rwkv_kk[l], "ka": rwkv_ka[l], "rk": rwkv_rk[l].reshape(-1),
            "ln_g": rwkv_ln_g[l], "ln_b": rwkv_ln_b[l],
            "idx_ln_g": idx_ln_g[l].reshape(1, INDEX_DIM), "idx_ln_b": idx_ln_b[l].reshape(1, INDEX_DIM),
            "w_br_rwkv": cm(w_br_rwkv[l].T).T.astype(BF16), "w_br_attn": w_br_attn[l],
            "w_out": w_out[l],
            "norm_cross": norm_cross[l], "norm_ffn": norm_ffn[l],
            "w_q_mem": w_q_mem[l], "w_o_mem": w_o_mem[l],
            "w_up": w_up[l], "w_down": w_down[l].astype(BF16),
        }
        assert n_att + ih == in_proj - o_att
        fp = _mixer_front(xp, b, s, pos_p, lw, dims)
        ygp, sfp, shp = _rwkv_branch(fp["p"], b, s, jnp.zeros((b, rp), F32),
                                     jnp.zeros((b, hr, RWKV_HEAD_DIM, RWKV_HEAD_DIM), F32), lw, dims)
        xp = _mixer_back(xp, fp, ygp, _attend_prompt(fp, b, s, nh, kvh, ih), lw)
        mn = _rmsnorm(mem_prompt.reshape(b * n_mem, d), norm_mem[l], BF16)
        mkp = _mm(mn, w_k_mem[l]).reshape(b, n_mem, mh * HEAD_DIM)
        mvp = _mm(mn, w_v_mem[l]).reshape(b, n_mem, mh * HEAD_DIM)
        xp = _cross_and_ffn(xp, b, s, mkp, mvp, lw, mh, n_mem)
        fs = _mixer_front(xs, db, ts, pos_s, lw, dims)
        ya_s = _attend_sample(fs, pool_k, pool_v, pool_idx, page_table + l * n_pool, nh, kvh, ih)
        ygs, sfs, shs = _rwkv_branch(fs["p"], db, ts, state_shift[l], state_rwkv[l], lw, dims)
        xs = _mixer_back(xs, fs, ygs, ya_s, lw)
        xs = _cross_and_ffn(xs, db, ts, mem_k[l * mem_rows:(l + 1) * mem_rows],
                            mem_v[l * mem_rows:(l + 1) * mem_rows], lw, mh, n_mem)
        vals = (fp["k"].reshape(b, s, kvh, HEAD_DIM), fp["v"].reshape(b, s, kvh, HEAD_DIM),
                fp["ki"].reshape(b, s, INDEX_DIM),
                mkp.reshape(b, n_mem, mh, HEAD_DIM), mvp.reshape(b, n_mem, mh, HEAD_DIM), sfp, shp,
                fs["k"].reshape(db, ts, kvh, HEAD_DIM), fs["v"].reshape(db, ts, kvh, HEAD_DIM),
                fs["ki"].reshape(db, ts, INDEX_DIM), sfs, shs)
        for lst, val in zip(outs, vals):
            lst.append(val)
    y_prompt = _rmsnorm(xp, final_norm, F32).reshape(b, s, d)
    y_sample = _rmsnorm(xs, final_norm, F32).reshape(db, ts, d)
    return (y_prompt, y_sample) + tuple(jnp.stack(o) for o in outs)
```

```python
import functools

import jax
import jax.numpy as jnp
from jax import lax
from jax.experimental import pallas as pl
from jax.experimental.pallas import tpu as pltpu

F32 = jnp.float32
BF16 = jnp.bfloat16
LANE = 128
SUBLANE = 8
VMEM_LIMIT = 56 * 1024 * 1024

RWKV_HEAD_DIM = 64
HEAD_DIM = 128
INDEX_DIM = 128
TOPK_MAX = 256
Q_BLOCK = 128
ROPE_THETA = 10000.0
NORM_EPS = 1e-6
RWKV_LN_EPS = 64e-5
IDX_LN_EPS = 1e-6
PREFIX_CHUNK = 256
COUNT_ROWS = 64
MAX_CAUSAL_BUCKETS = 8
PREP_STEPS = 32
NT_DIMS = (((1,), (1,)), ((), ()))


def _cparams(sem):
    return pltpu.CompilerParams(dimension_semantics=sem, vmem_limit_bytes=VMEM_LIMIT)


def _round_up(n, m):
    return (n + m - 1) // m * m


def _tile(n, pref, unit):
    best = None
    t = unit
    while t <= min(n, pref):
        if n % t == 0:
            best = t
        t += unit
    return best if best is not None else n


def _rmsnorm_kernel(x_ref, g_ref, o_ref):
    x = x_ref[...]
    ms = jnp.mean(x * x, axis=-1, keepdims=True)
    o_ref[...] = (x * lax.rsqrt(ms + NORM_EPS) * g_ref[...]).astype(o_ref.dtype)


def _rmsnorm(x, g, out_dtype):
    m, d = x.shape
    tm = _tile(m, 256, SUBLANE)
    return pl.pallas_call(
        _rmsnorm_kernel,
        grid=(m // tm,),
        in_specs=[pl.BlockSpec((tm, d), lambda i: (i, 0)), pl.BlockSpec((1, d), lambda i: (0, 0))],
        out_specs=pl.BlockSpec((tm, d), lambda i: (i, 0)),
        out_shape=jax.ShapeDtypeStruct((m, d), out_dtype),
        compiler_params=_cparams(("parallel",)),
        name="rmsnorm",
    )(x, g.reshape(1, d))


MXU_COLS = 256


def _col_tile(n, pref):
    return _tile(n, pref, MXU_COLS if n % MXU_COLS == 0 else LANE)


def _mm_kernel(x_ref, w_ref, o_ref):
    o_ref[...] = jnp.dot(x_ref[...], w_ref[...].astype(BF16), preferred_element_type=F32).astype(o_ref.dtype)


def _mm(x, w, n=None, out_dtype=F32, tm_pref=1024, tn_pref=768):
    m, k = x.shape
    n = w.shape[1] if n is None else n
    tm = _tile(m, tm_pref, SUBLANE)
    tn = _col_tile(n, tn_pref)
    return pl.pallas_call(
        _mm_kernel,
        grid=(m // tm, n // tn),
        in_specs=[pl.BlockSpec((tm, k), lambda i, j: (i, 0)), pl.BlockSpec((k, tn), lambda i, j: (0, j))],
        out_specs=pl.BlockSpec((tm, tn), lambda i, j: (i, j)),
        out_shape=jax.ShapeDtypeStruct((m, n), out_dtype),
        compiler_params=_cparams(("parallel", "parallel")),
        name="matmul",
    )(x, w)


def _mm_resid_kernel(x_ref, w_ref, r_ref, o_ref, acc_ref, *, nk):
    kk = pl.program_id(2)

    @pl.when(kk == 0)
    def _():
        acc_ref[...] = r_ref[...]

    acc_ref[...] += jnp.dot(x_ref[...], w_ref[...].astype(BF16), preferred_element_type=F32)

    @pl.when(kk == nk - 1)
    def _():
        o_ref[...] = acc_ref[...]


def _mm_resid(x, w, resid, tm_pref=1024, tn_pref=1024, tk_pref=2048):
    m, k = x.shape
    n = w.shape[1]
    tm = _tile(m, tm_pref, SUBLANE)
    tn = _tile(n, tn_pref, LANE)
    tk = _tile(k, tk_pref, LANE)
    nk = k // tk
    return pl.pallas_call(
        functools.partial(_mm_resid_kernel, nk=nk),
        grid=(m // tm, n // tn, nk),
        in_specs=[pl.BlockSpec((tm, tk), lambda i, j, kk: (i, kk)),
                  pl.BlockSpec((tk, tn), lambda i, j, kk: (kk, j)),
                  pl.BlockSpec((tm, tn), lambda i, j, kk: (i, j))],
        out_specs=pl.BlockSpec((tm, tn), lambda i, j, kk: (i, j)),
        out_shape=jax.ShapeDtypeStruct((m, n), F32),
        scratch_shapes=[pltpu.VMEM((tm, tn), F32)],
        compiler_params=_cparams(("parallel", "parallel", "arbitrary")),
        name="matmul_resid",
    )(x, w, resid)


def _mm_pair_kernel(x_ref, xs_ref, w_ref, o_ref, os_ref):
    w = w_ref[...].astype(BF16)
    o_ref[...] = jnp.dot(x_ref[...], w, preferred_element_type=F32)

    @pl.when(pl.program_id(1) == 0)
    def _():
        os_ref[...] = jnp.dot(xs_ref[...], w, preferred_element_type=F32)


def _mm_pair(x, xs, w, n=None, tm_pref=1024, tn_pref=768):
    m, k = x.shape
    ms = xs.shape[0]
    n = w.shape[1] if n is None else n
    tm = _tile(m, tm_pref, SUBLANE)
    tn = _col_tile(n, tn_pref)
    return pl.pallas_call(
        _mm_pair_kernel,
        grid=(n // tn, m // tm),
        in_specs=[pl.BlockSpec((tm, k), lambda j, i: (i, 0)), pl.BlockSpec((ms, k), lambda j, i: (0, 0)),
                  pl.BlockSpec((k, tn), lambda j, i: (0, j))],
        out_specs=[pl.BlockSpec((tm, tn), lambda j, i: (i, j)), pl.BlockSpec((ms, tn), lambda j, i: (0, j))],
        out_shape=[jax.ShapeDtypeStruct((m, n), F32), jax.ShapeDtypeStruct((ms, n), F32)],
        compiler_params=_cparams(("arbitrary", "arbitrary")),
        name="matmul_pair",
    )(x, xs, w)


def _swiglu_pair_kernel(x_ref, xs_ref, wg_ref, wu_ref, o_ref, os_ref):
    wg = wg_ref[...].astype(BF16)
    wu = wu_ref[...].astype(BF16)

    def act(x):
        g = jnp.dot(x, wg, preferred_element_type=F32)
        u = jnp.dot(x, wu, preferred_element_type=F32)
        return (g * jax.nn.sigmoid(g) * u).astype(BF16)

    o_ref[...] = act(x_ref[...])

    @pl.when(pl.program_id(1) == 0)
    def _():
        os_ref[...] = act(xs_ref[...])


def _swiglu_pair(x, xs, w_up, tm_pref=1024, tn_pref=256):
    m, k = x.shape
    ms = xs.shape[0]
    f = w_up.shape[1] // 2
    tm = _tile(m, tm_pref, SUBLANE)
    tn = _tile(f, tn_pref, LANE)
    nb = f // tn
    return pl.pallas_call(
        _swiglu_pair_kernel,
        grid=(nb, m // tm),
        in_specs=[pl.BlockSpec((tm, k), lambda j, i: (i, 0)), pl.BlockSpec((ms, k), lambda j, i: (0, 0)),
                  pl.BlockSpec((k, tn), lambda j, i: (0, j)),
                  pl.BlockSpec((k, tn), lambda j, i: (0, j + nb))],
        out_specs=[pl.BlockSpec((tm, tn), lambda j, i: (i, j)), pl.BlockSpec((ms, tn), lambda j, i: (0, j))],
        out_shape=[jax.ShapeDtypeStruct((m, f), BF16), jax.ShapeDtypeStruct((ms, f), BF16)],
        compiler_params=_cparams(("arbitrary", "arbitrary")),
        name="swiglu_pair",
    )(x, xs, w_up, w_up)


def _gate_mul_kernel(y_ref, g_ref, o_ref):
    o_ref[...] = (y_ref[...] * g_ref[...]).astype(o_ref.dtype)


def _gate_mul(y, g):
    m, n = y.shape
    tm = _tile(m, 256, SUBLANE)
    blk = pl.BlockSpec((tm, n), lambda i: (i, 0))
    return pl.pallas_call(
        _gate_mul_kernel,
        grid=(m // tm,),
        in_specs=[blk, blk],
        out_specs=blk,
        out_shape=jax.ShapeDtypeStruct((m, n), BF16),
        compiler_params=_cparams(("parallel",)),
        name="gate_mul",
    )(y, g)


def _merge_kernel(yr_ref, ya_ref, gr_ref, ga_ref, wr_ref, wa_ref, o_ref):
    br = jnp.dot(yr_ref[...], wr_ref[...].astype(BF16), preferred_element_type=F32)
    ba = jnp.dot(ya_ref[...], wa_ref[...].astype(BF16), preferred_element_type=F32)
    o_ref[...] = (jax.nn.sigmoid(gr_ref[...]) * br + jax.nn.sigmoid(ga_ref[...]) * ba).astype(o_ref.dtype)


def _merge(yr, ya, gates, w_br_r, w_br_a, tm_pref=1024, tn_pref=512):
    m, rd = yr.shape
    ad = ya.shape[1]
    d = w_br_r.shape[1]
    tm = _tile(m, tm_pref, SUBLANE)
    tn = _tile(d, tn_pref, LANE)
    nb = d // tn
    return pl.pallas_call(
        _merge_kernel,
        grid=(m // tm, nb),
        in_specs=[pl.BlockSpec((tm, rd), lambda i, j: (i, 0)),
                  pl.BlockSpec((tm, ad), lambda i, j: (i, 0)),
                  pl.BlockSpec((tm, tn), lambda i, j: (i, j)),
                  pl.BlockSpec((tm, tn), lambda i, j: (i, j + nb)),
                  pl.BlockSpec((rd, tn), lambda i, j: (0, j)),
                  pl.BlockSpec((ad, tn), lambda i, j: (0, j))],
        out_specs=pl.BlockSpec((tm, tn), lambda i, j: (i, j)),
        out_shape=jax.ShapeDtypeStruct((m, d), BF16),
        compiler_params=_cparams(("parallel", "parallel")),
        name="merge",
    )(yr, ya, gates, gates, w_br_r, w_br_a)


def _lane_segment(h):
    return lax.broadcasted_iota(jnp.int32, (SUBLANE, LANE), 1) // h


def _to_chain_tiles(x, out_ref, nbatch, tt, h):
    per = LANE // h
    seg = _lane_segment(h)
    for c in range(nbatch // per):
        for g in range(tt // SUBLANE):
            for col in range(RWKV_HEAD_DIM // per):
                tiles = [x[(c * per + q) * tt + g * SUBLANE:(c * per + q) * tt + (g + 1) * SUBLANE,
                           col * LANE:(col + 1) * LANE] for q in range(per)]
                for jj in range(per):
                    acc = None
                    for q in range(per):
                        sh = ((q - jj) % per) * h
                        t = tiles[q] if sh == 0 else pltpu.roll(tiles[q], sh, axis=1)
                        acc = t if acc is None else jnp.where(seg == q, t, acc)
                    n = col * per + jj
                    out_ref[g, n * SUBLANE:(n + 1) * SUBLANE, c * LANE:(c + 1) * LANE] = acc


def _rwkv_prep_kernel(p_ref, sh_ref, *rest, rd, lw, la, lgp, chain):
    if chain is not None:
        prev_ref, rest = rest[0], rest[1:]
    mu_ref, w0_ref, w2_ref, a0_ref, a2_ref, g2_ref, r_ref, w_ref, k_ref, v_ref, a_ref, g_ref = rest
    if chain is None:
        p = p_ref[...]
        shifted = sh_ref[...]
    else:
        nbatch, tt, h = chain
        first = pl.program_id(0) == 0
        row0 = lax.broadcasted_iota(jnp.int32, (tt, p_ref.shape[2]), 0) == 0
        ps, shs = [], []
        for bb in range(nbatch):
            pb = p_ref[bb]
            above = jnp.where(first, prev_ref[bb], sh_ref[bb, SUBLANE - 1:SUBLANE, :])
            ps.append(pb)
            shs.append(jnp.where(row0, above, pltpu.roll(pb, 1, axis=0)))
        p = jnp.concatenate(ps, axis=0)
        shifted = jnp.concatenate(shs, axis=0)
    xm = p + (shifted - p) * mu_ref[...]
    o = 0
    xr = xm[:, o:o + rd]; o += rd
    xw = xm[:, o:o + lw]; o += lw
    xk = xm[:, o:o + rd]; o += rd
    xv = xm[:, o:o + rd]; o += rd
    xa = xm[:, o:o + la]; o += la
    xg = xm[:, o:o + lgp]
    z = w0_ref[...] + jnp.dot(jnp.tanh(xw).astype(BF16), w2_ref[...], preferred_element_type=F32)
    sp = jnp.maximum(-z, 0.0) + jnp.log(1.0 + jnp.exp(-jnp.abs(z)))
    w_raw = -sp - 0.5
    decay = jnp.exp(-jnp.exp(w_raw))
    a = jax.nn.sigmoid(a0_ref[...] + jnp.dot(xa.astype(BF16), a2_ref[...], preferred_element_type=F32))
    g = jnp.dot(jax.nn.sigmoid(xg).astype(BF16), g2_ref[...], preferred_element_type=F32)
    if chain is None:
        r_ref[...] = xr
        w_ref[...] = decay
        k_ref[...] = xk
        v_ref[...] = xv
        a_ref[...] = a
        g_ref[...] = g
    else:
        for x, ref in ((xr, r_ref), (decay, w_ref), (xk, k_ref), (xv, v_ref), (a, a_ref)):
            _to_chain_tiles(x, ref, nbatch, tt, h)
        for bb in range(nbatch):
            g_ref[bb] = g[bb * tt:(bb + 1) * tt]


def _chain_kernels_fit(b, t, h):
    return LANE % h == 0 and b % (LANE // h) == 0 and t % PREP_STEPS == 0


def _rwkv_prep(p, prev, b, t, mu, w0, w2, a0, a2, g2, rd, lw, la, lgp):
    m, rpp = p.shape
    h = rd // RWKV_HEAD_DIM
    fix = lambda i: (0, 0)
    par_specs = [pl.BlockSpec((1, rpp), fix), pl.BlockSpec((1, rd), fix), pl.BlockSpec((lw, rd), fix),
                 pl.BlockSpec((1, rd), fix), pl.BlockSpec((la, rd), fix), pl.BlockSpec((lgp, rd), fix)]
    if not _chain_kernels_fit(b, t, h):
        shifted = jnp.concatenate([prev[:, None, :], p.reshape(b, t, rpp)[:, :-1]], axis=1).reshape(m, rpp)
        tm = _tile(m, 128, SUBLANE)
        row = lambda i: (i, 0)
        out = jax.ShapeDtypeStruct((m, rd), F32)
        return pl.pallas_call(
            functools.partial(_rwkv_prep_kernel, rd=rd, lw=lw, la=la, lgp=lgp, chain=None),
            grid=(m // tm,),
            in_specs=[pl.BlockSpec((tm, rpp), row), pl.BlockSpec((tm, rpp), row)] + par_specs,
            out_specs=[pl.BlockSpec((tm, rd), row)] * 6,
            out_shape=[out] * 6,
            compiler_params=_cparams(("parallel",)),
            name="rwkv_prep",
        )(p, shifted, mu, w0, w2, a0, a2, g2)
    tt = PREP_STEPS
    nc = b * h
    p3 = p.reshape(b, t, rpp)
    step_blk = lambda i: (0, i, 0)
    chain_out = jax.ShapeDtypeStruct((t // SUBLANE, RWKV_HEAD_DIM * SUBLANE, nc), F32)
    chain_spec = pl.BlockSpec((tt // SUBLANE, RWKV_HEAD_DIM * SUBLANE, nc), lambda i: (i, 0, 0))
    outs = pl.pallas_call(
        functools.partial(_rwkv_prep_kernel, rd=rd, lw=lw, la=la, lgp=lgp, chain=(b, tt, h)),
        grid=(t // tt,),
        in_specs=[pl.BlockSpec((b, tt, rpp), step_blk),
                  pl.BlockSpec((b, SUBLANE, rpp), lambda i: (0, jnp.maximum(i * (tt // SUBLANE) - 1, 0), 0)),
                  pl.BlockSpec((b, 1, rpp), lambda i: (0, 0, 0))] + par_specs,
        out_specs=[chain_spec] * 5 + [pl.BlockSpec((b, tt, rd), step_blk)],
        out_shape=[chain_out] * 5 + [jax.ShapeDtypeStruct((b, t, rd), F32)],
        compiler_params=_cparams(("parallel",)),
        name="rwkv_prep",
    )(p3, p3, prev.reshape(b, 1, rpp), mu, w0, w2, a0, a2, g2)
    return list(outs[:5]) + [outs[5].reshape(m, rd)]


def _rwkv_scan_kernel(r_ref, w_ref, k_ref, v_ref, a_ref, kkw_ref, ka_ref, rk_ref, lng_ref, lnb_ref,
                      s0_ref, y_ref, s_ref, kk_s, nb_s, kp_s, *, ts):
    n = RWKV_HEAD_DIM

    @pl.when(pl.program_id(1) == 0)
    def _():
        s_ref[...] = s0_ref[...]

    def tile(j):
        return slice(j * ts, (j + 1) * ts)

    ss = None
    for j in range(n):
        kk = k_ref[0, tile(j), :] * kkw_ref[j:j + 1, :]
        kk_s[tile(j), :] = kk
        ss = kk * kk if ss is None else ss + kk * kk
    inv = 1.0 / jnp.maximum(jnp.sqrt(ss), 1e-12)
    bonus = None
    for j in range(n):
        kj = k_ref[0, tile(j), :]
        aj = a_ref[0, tile(j), :]
        kk = kk_s[tile(j), :] * inv
        kk_s[tile(j), :] = kk
        nb_s[tile(j), :] = -(kk * aj)
        kp = kj * (1.0 + (aj - 1.0) * ka_ref[j:j + 1, :])
        kp_s[tile(j), :] = kp
        term = r_ref[0, tile(j), :] * kp * rk_ref[j:j + 1, :]
        bonus = term if bonus is None else bonus + term

    def row(ref, j, tt):
        return ref[j * ts + tt:j * ts + tt + 1, :]

    def row3(ref, j, tt):
        return ref[0, j * ts + tt:j * ts + tt + 1, :]

    lng = lng_ref[...]
    lnb = lnb_ref[...]
    sk = None
    for j in range(n):
        term = s_ref[j] * row(kk_s, j, 0)
        sk = term if sk is None else sk + term
    for tt in range(ts):
        v = v_ref[0, pl.ds(tt, n, stride=ts), :]
        y = None
        sk_next = None
        for j in range(n):
            sn = s_ref[j] * row3(w_ref, j, tt) + sk * row(nb_s, j, tt) + v * row(kp_s, j, tt)
            s_ref[j] = sn
            yt = sn * row3(r_ref, j, tt)
            y = yt if y is None else y + yt
            if tt + 1 < ts:
                st = sn * row(kk_s, j, tt + 1)
                sk_next = st if sk_next is None else sk_next + st
        sk = sk_next
        mu = jnp.mean(y, axis=0, keepdims=True)
        yc = y - mu
        var = jnp.mean(yc * yc, axis=0, keepdims=True)
        yn = yc * lax.rsqrt(var + RWKV_LN_EPS) * lng + lnb
        y_ref[0, pl.ds(tt, n, stride=ts), :] = yn + bonus[tt:tt + 1, :] * v


def _rwkv_scan(r, w, k, v, a, kkw, ka, rk, lng, lnb, s0):
    nt, rows, nc = r.shape
    n = RWKV_HEAD_DIM
    ts = rows // n
    seq = pl.BlockSpec((1, rows, LANE), lambda c, tt: (tt, 0, c))
    par = pl.BlockSpec((n, LANE), lambda c, tt: (0, c))
    st = pl.BlockSpec((n, n, LANE), lambda c, tt: (0, 0, c))
    return pl.pallas_call(
        functools.partial(_rwkv_scan_kernel, ts=ts),
        grid=(nc // LANE, nt),
        in_specs=[seq] * 5 + [par] * 5 + [st],
        out_specs=[seq, st],
        out_shape=[jax.ShapeDtypeStruct((nt, rows, nc), F32), jax.ShapeDtypeStruct((n, n, nc), F32)],
        scratch_shapes=[pltpu.VMEM((rows, LANE), F32)] * 3,
        compiler_params=_cparams(("parallel", "arbitrary")),
        name="rwkv_scan",
    )(r, w, k, v, a, kkw, ka, rk, lng, lnb, s0)


def _rwkv_post_kernel(y_ref, g_ref, o_ref, *, nbatch, tt, h):
    per = LANE // h
    seg = _lane_segment(h)
    pair = 2 * SUBLANE
    for c in range(nbatch // per):
        for q in range(per):
            bb = c * per + q
            for g2 in range(tt // pair):
                for col in range(RWKV_HEAD_DIM // per):
                    halves = []
                    for g in (2 * g2, 2 * g2 + 1):
                        acc = None
                        for ii in range(per):
                            i = col * per + ii
                            tile = y_ref[g, i * SUBLANE:(i + 1) * SUBLANE, c * LANE:(c + 1) * LANE]
                            sh = ((ii - q) % per) * h
                            t = tile if sh == 0 else pltpu.roll(tile, sh, axis=1)
                            acc = t if acc is None else jnp.where(seg == ii, t, acc)
                        halves.append(acc)
                    rows = slice(g2 * pair, (g2 + 1) * pair)
                    cols = slice(col * LANE, (col + 1) * LANE)
                    o_ref[bb, rows, cols] = (jnp.concatenate(halves, axis=0) * g_ref[bb, rows, cols]).astype(o_ref.dtype)


def _rwkv_post(y, g, b, t, h):
    rd = RWKV_HEAD_DIM * h
    tt = PREP_STEPS
    step_blk = lambda i: (0, i, 0)
    return pl.pallas_call(
        functools.partial(_rwkv_post_kernel, nbatch=b, tt=tt, h=h),
        grid=(t // tt,),
        in_specs=[pl.BlockSpec((tt // SUBLANE, RWKV_HEAD_DIM * SUBLANE, b * h), lambda i: (i, 0, 0)),
                  pl.BlockSpec((b, tt, rd), step_blk)],
        out_specs=pl.BlockSpec((b, tt, rd), step_blk),
        out_shape=jax.ShapeDtypeStruct((b, t, rd), BF16),
        compiler_params=_cparams(("parallel",)),
        name="rwkv_post",
    )(y, g.reshape(b, t, rd)).reshape(b * t, rd)


def _rope(x, cos2, sin2):
    return x * cos2 + pltpu.roll(x, HEAD_DIM // 2, axis=1) * sin2


def _attn_prep_kernel(a_ref, cos_ref, sin_ref, lng_ref, lnb_ref,
                      q_ref, k_ref, v_ref, qi_ref, ki_ref, wi_ref, *, nh, kvh, ih, wscale):
    cos2 = cos_ref[...]
    sin2 = sin_ref[...]
    o = 0
    for h in range(nh):
        q_ref[:, h * HEAD_DIM:(h + 1) * HEAD_DIM] = _rope(a_ref[:, o:o + HEAD_DIM], cos2, sin2).astype(BF16)
        o += HEAD_DIM
    for h in range(kvh):
        k_ref[:, h * HEAD_DIM:(h + 1) * HEAD_DIM] = _rope(a_ref[:, o:o + HEAD_DIM], cos2, sin2)
        o += HEAD_DIM
    v_ref[...] = a_ref[:, o:o + kvh * HEAD_DIM]
    o += kvh * HEAD_DIM
    for h in range(ih):
        qi_ref[:, h * INDEX_DIM:(h + 1) * INDEX_DIM] = _rope(a_ref[:, o:o + INDEX_DIM], cos2, sin2).astype(BF16)
        o += INDEX_DIM
    x = a_ref[:, o:o + INDEX_DIM]
    o += INDEX_DIM
    mu = jnp.mean(x, axis=-1, keepdims=True)
    xc = x - mu
    var = jnp.mean(xc * xc, axis=-1, keepdims=True)
    kin = xc * lax.rsqrt(var + IDX_LN_EPS) * lng_ref[...] + lnb_ref[...]
    ki_ref[...] = _rope(kin, cos2, sin2)
    wi_ref[...] = a_ref[:, o:o + LANE] * wscale


def _attn_prep(a, cos2, sin2, lng, lnb, nh, kvh, ih):
    m, na = a.shape
    tm = _tile(m, 128, SUBLANE)
    row = lambda i: (i, 0)
    fix = lambda i: (0, 0)
    ad, kvd, idd = nh * HEAD_DIM, kvh * HEAD_DIM, ih * INDEX_DIM
    wscale = float(ih) ** -0.5 * float(INDEX_DIM) ** -0.5
    return pl.pallas_call(
        functools.partial(_attn_prep_kernel, nh=nh, kvh=kvh, ih=ih, wscale=wscale),
        grid=(m // tm,),
        in_specs=[pl.BlockSpec((tm, na), row), pl.BlockSpec((tm, LANE), row), pl.BlockSpec((tm, LANE), row),
                  pl.BlockSpec((1, LANE), fix), pl.BlockSpec((1, LANE), fix)],
        out_specs=[pl.BlockSpec((tm, ad), row), pl.BlockSpec((tm, kvd), row), pl.BlockSpec((tm, kvd), row),
                   pl.BlockSpec((tm, idd), row),
                   pl.BlockSpec((tm, INDEX_DIM), row), pl.BlockSpec((tm, LANE), row)],
        out_shape=[jax.ShapeDtypeStruct((m, ad), BF16), jax.ShapeDtypeStruct((m, kvd), F32),
                   jax.ShapeDtypeStruct((m, kvd), F32), jax.ShapeDtypeStruct((m, idd), BF16),
                   jax.ShapeDtypeStruct((m, INDEX_DIM), F32),
                   jax.ShapeDtypeStruct((m, LANE), F32)],
        compiler_params=_cparams(("parallel",)),
        name="attn_prep",
    )(a, cos2, sin2, lng, lnb)


def _count_rows(ones):
    l = ones.shape[0]
    if l % COUNT_ROWS == 0 and l > COUNT_ROWS:
        ones = jnp.sum(ones.reshape(l // COUNT_ROWS, COUNT_ROWS, LANE), axis=0)
    return jnp.sum(ones, axis=0, keepdims=True)


def _topk_mask(score, key_ref, n_sel):
    l = score.shape[0]
    bits = lax.bitcast_convert_type(score, jnp.int32)
    key_ref[...] = jnp.where(bits < 0, bits ^ jnp.int32(0x7FFFFFFF), bits)
    sign = jnp.int32(-2 ** 31)
    nf = jnp.float32(n_sel)

    def body(i, ans):
        cand = ans | lax.shift_left(jnp.int32(1), 31 - i)
        cnt = _count_rows(jnp.where(key_ref[...] >= (cand ^ sign), 1.0, 0.0))
        return jnp.where(cnt >= nf, cand, ans)

    ans = lax.fori_loop(0, 32, body, jnp.zeros((1, LANE), jnp.int32))
    thr = ans ^ sign
    need = nf - _count_rows(jnp.where(key_ref[...] > thr, 1.0, 0.0))
    ch = PREFIX_CHUNK
    tri = jnp.where(lax.broadcasted_iota(jnp.int32, (ch, ch), 0) > lax.broadcasted_iota(jnp.int32, (ch, ch), 1),
                    1.0, 0.0).astype(BF16)
    run = jnp.zeros((1, LANE), F32)
    ranks = []
    for c in range(l // ch):
        e = jnp.where(key_ref[c * ch:(c + 1) * ch, :] == thr, 1.0, 0.0)
        ranks.append(jnp.dot(tri, e.astype(BF16), preferred_element_type=F32) + run)
        run = run + jnp.sum(e, axis=0, keepdims=True)
    rank = jnp.concatenate(ranks, axis=0)
    key = key_ref[...]
    return jnp.logical_or(key > thr, jnp.logical_and(key == thr, rank < need))


def _dsa_prompt_kernel(qi_ref, wit_ref, ki_ref, q_ref, k_ref, v_ref, o_ref, sc_ref, key_ref,
                       *, n_sel, ih, nh, kvh, qb0):
    l = ki_ref.shape[1]
    qb = qb0 + pl.program_id(1)
    lhs = ki_ref[0].astype(BF16)
    for hp in range(ih // 2):
        c0 = 2 * hp * INDEX_DIM
        c1 = c0 + INDEX_DIM
        rhs = jnp.concatenate([qi_ref[:, c0:c1], qi_ref[:, c1:c1 + INDEX_DIM]], axis=0)
        rel = jnp.maximum(lax.dot_general(lhs, rhs, NT_DIMS, preferred_element_type=F32), 0.0)
        w0 = wit_ref[0, 0, 2 * hp:2 * hp + 1, :]
        w1 = wit_ref[0, 0, 2 * hp + 1:2 * hp + 2, :]
        part = rel[:, :Q_BLOCK] * w0 + rel[:, Q_BLOCK:] * w1
        if hp == 0:
            sc_ref[...] = part
        else:
            sc_ref[...] += part
    s_idx = lax.broadcasted_iota(jnp.int32, (l, Q_BLOCK), 0)
    q_pos = qb * Q_BLOCK + lax.broadcasted_iota(jnp.int32, (l, Q_BLOCK), 1)
    causal = s_idx <= q_pos
    sel = _topk_mask(jnp.where(causal, sc_ref[...], -jnp.inf), key_ref, n_sel)
    bias = jnp.where(jnp.logical_and(sel, causal), 0.0, -jnp.inf).T
    c_exp = float(HEAD_DIM) ** -0.5 * 1.4426950408889634
    rep = nh // kvh
    for g in range(kvh):
        kg = k_ref[0, :, g * HEAD_DIM:(g + 1) * HEAD_DIM].astype(BF16)
        vg = v_ref[0, :, g * HEAD_DIM:(g + 1) * HEAD_DIM].astype(BF16)
        h0 = g * rep
        qg = jnp.concatenate([q_ref[:, (h0 + r) * HEAD_DIM:(h0 + r + 1) * HEAD_DIM] for r in range(rep)], axis=0)
        lg = lax.dot_general(qg, kg, NT_DIMS, preferred_element_type=F32)
        ps, dens = [], []
        for r in range(rep):
            x = lg[r * Q_BLOCK:(r + 1) * Q_BLOCK] + bias
            p = jnp.exp2((x - jnp.max(x, axis=-1, keepdims=True)) * c_exp)
            dens.append(jnp.sum(p, axis=-1, keepdims=True))
            ps.append(p.astype(BF16))
        og = jnp.dot(jnp.concatenate(ps, axis=0), vg, preferred_element_type=F32)
        for r in range(rep):
            o = og[r * Q_BLOCK:(r + 1) * Q_BLOCK] / dens[r]
            o_ref[0, :, (h0 + r) * HEAD_DIM:(h0 + r + 1) * HEAD_DIM] = o.astype(o_ref.dtype)


def _causal_buckets(nq):
    for n in range(MAX_CAUSAL_BUCKETS, 0, -1):
        if nq % n == 0 and (nq // n * Q_BLOCK) % PREFIX_CHUNK == 0:
            return n
    return 1


def _dsa_prompt(qi, wit, ki, q, k, v, b, s, n_sel, ih, nh, kvh):
    nq = s // Q_BLOCK
    nbk = _causal_buckets(nq)
    per = nq // nbk
    per_b = lambda bb, qq: (bb, 0, 0)
    outs = []
    for u in range(nbk):
        le = (u + 1) * per * Q_BLOCK
        row = functools.partial(lambda bb, qq, u: (bb * nq + u * per + qq, 0), u=u)
        wrow = functools.partial(lambda bb, qq, u: (bb, u * per + qq, 0, 0), u=u)
        outs.append(pl.pallas_call(
            functools.partial(_dsa_prompt_kernel, n_sel=n_sel, ih=ih, nh=nh, kvh=kvh, qb0=u * per),
            grid=(b, per),
            in_specs=[pl.BlockSpec((Q_BLOCK, ih * INDEX_DIM), row),
                      pl.BlockSpec((1, 1, ih, Q_BLOCK), wrow),
                      pl.BlockSpec((1, le, INDEX_DIM), per_b),
                      pl.BlockSpec((Q_BLOCK, nh * HEAD_DIM), row),
                      pl.BlockSpec((1, le, kvh * HEAD_DIM), per_b), pl.BlockSpec((1, le, kvh * HEAD_DIM), per_b)],
            out_specs=pl.BlockSpec((1, Q_BLOCK, nh * HEAD_DIM), lambda bb, qq: (bb, qq, 0)),
            out_shape=jax.ShapeDtypeStruct((b, per * Q_BLOCK, nh * HEAD_DIM), BF16),
            scratch_shapes=[pltpu.VMEM((le, Q_BLOCK), F32), pltpu.VMEM((le, Q_BLOCK), jnp.int32)],
            compiler_params=_cparams(("parallel", "arbitrary")),
            name="dsa_prompt",
        )(qi, wit, ki, q, k, v))
    return jnp.concatenate(outs, axis=1).reshape(b * s, nh * HEAD_DIM)


def _dsa_score_kernel(pt_ref, qi_ref, wi_ref, kin_ref, *rest, n_pages):
    pages = rest[:n_pages]
    past_ref, new_ref = rest[n_pages:]
    qi = qi_ref[0]
    wcol = wi_ref[0]
    for p in range(n_pages):
        rel = lax.dot_general(qi, pages[p][0].astype(BF16), NT_DIMS, preferred_element_type=F32)
        past_ref[0, p] = jnp.sum(jnp.maximum(rel, 0.0) * wcol, axis=0, keepdims=True)
    kin = kin_ref[0].astype(BF16).astype(F32)
    rel_new = jnp.maximum(jnp.sum(qi.astype(F32) * kin, axis=-1, keepdims=True), 0.0)
    new = jnp.sum(rel_new * wcol, axis=0, keepdims=True)
    new_ref[0] = jnp.broadcast_to(new, (1, LANE))


def _dsa_scores(page_table, qi, wi, ki_new, pool_idx):
    db, n_pages = page_table.shape
    ih = qi.shape[1]
    psz = pool_idx.shape[1]
    per_b = lambda bb, pt: (bb, 0, 0)
    page_specs = [pl.BlockSpec((1, psz, INDEX_DIM), functools.partial(lambda bb, pt, p: (pt[bb, p], 0, 0), p=p))
                  for p in range(n_pages)]
    grid_spec = pltpu.PrefetchScalarGridSpec(
        num_scalar_prefetch=1,
        grid=(db,),
        in_specs=[pl.BlockSpec((1, ih, INDEX_DIM), per_b),
                  pl.BlockSpec((1, ih, 1), per_b), pl.BlockSpec((1, 1, INDEX_DIM), per_b)] + page_specs,
        out_specs=[pl.BlockSpec((1, n_pages, 1, psz), lambda bb, pt: (bb, 0, 0, 0)),
                   pl.BlockSpec((1, 1, LANE), per_b)],
    )
    return pl.pallas_call(
        functools.partial(_dsa_score_kernel, n_pages=n_pages),
        grid_spec=grid_spec,
        out_shape=[jax.ShapeDtypeStruct((db, n_pages, 1, psz), F32), jax.ShapeDtypeStruct((db, 1, LANE), F32)],
        compiler_params=_cparams(("arbitrary",)),
        name="dsa_sample_scores",
    )(page_table, qi, wi, ki_new, *([pool_idx] * n_pages))


def _mask_kernel(sc_ref, m_ref, key_ref, *, n_sel):
    m_ref[...] = jnp.where(_topk_mask(sc_ref[...], key_ref, n_sel), 1.0, 0.0)


def _select_mask(score_t, n_sel):
    l, n = score_t.shape
    blk = pl.BlockSpec((l, LANE), lambda i: (0, i))
    return pl.pallas_call(
        functools.partial(_mask_kernel, n_sel=n_sel),
        grid=(n // LANE,),
        in_specs=[blk],
        out_specs=blk,
        out_shape=jax.ShapeDtypeStruct((l, n), F32),
        scratch_shapes=[pltpu.VMEM((l, LANE), jnp.int32)],
        compiler_params=_cparams(("parallel",)),
        name="select_mask",
    )(score_t)


def _dsa_sample_kernel(pt_ref, q_ref, bias_ref, kn_ref, vn_ref, *rest, n_pages, nh, kvh):
    kpages = rest[:n_pages]
    vpages = rest[n_pages:2 * n_pages]
    o_ref = rest[2 * n_pages]
    rows = kpages[0].shape[0]
    rep = nh // kvh
    scale = float(HEAD_DIM) ** -0.5
    q = q_ref[0]

    def own(width):
        col = lax.broadcasted_iota(jnp.int32, (nh, width), 1)
        head = lax.broadcasted_iota(jnp.int32, (nh, width), 0)
        return col % kvh == head // rep

    own_page = own(rows)
    lgs = []
    for p in range(n_pages):
        lg = lax.dot_general(q, kpages[p][...].astype(BF16), NT_DIMS, preferred_element_type=F32)
        lgs.append(jnp.where(own_page, lg * scale + bias_ref[0, :, p * rows:(p + 1) * rows], -jnp.inf))
    lg_new = lax.dot_general(q, kn_ref[0].astype(BF16), NT_DIMS, preferred_element_type=F32)
    lg_new = jnp.where(own(LANE), lg_new * scale + bias_ref[0, :, n_pages * rows:n_pages * rows + LANE], -jnp.inf)
    m = jnp.max(lg_new, axis=-1, keepdims=True)
    for lg in lgs:
        m = jnp.maximum(m, jnp.max(lg, axis=-1, keepdims=True))
    p_new = jnp.exp(lg_new - m)
    den = jnp.sum(p_new, axis=-1, keepdims=True)
    acc = jnp.dot(p_new.astype(BF16), vn_ref[0].astype(BF16), preferred_element_type=F32)
    for p in range(n_pages):
        pp = jnp.exp(lgs[p] - m)
        den = den + jnp.sum(pp, axis=-1, keepdims=True)
        acc = acc + jnp.dot(pp.astype(BF16), vpages[p][...].astype(BF16), preferred_element_type=F32)
    o_ref[0] = (acc / den).astype(o_ref.dtype)


def _dsa_sample(page_table, q, bias, k_new, v_new, pool_k, pool_v, nh, kvh, psz):
    db, n_pages = page_table.shape
    rows = psz * kvh
    lb = bias.shape[2]
    per_b = lambda bb, pt: (bb, 0, 0)
    page_specs = [pl.BlockSpec((rows, HEAD_DIM), functools.partial(lambda bb, pt, p: (pt[bb, p], 0), p=p))
                  for p in range(n_pages)]
    grid_spec = pltpu.PrefetchScalarGridSpec(
        num_scalar_prefetch=1,
        grid=(db,),
        in_specs=[pl.BlockSpec((1, nh, HEAD_DIM), per_b), pl.BlockSpec((1, 1, lb), per_b),
                  pl.BlockSpec((1, LANE, HEAD_DIM), per_b), pl.BlockSpec((1, LANE, HEAD_DIM), per_b)]
        + page_specs + page_specs,
        out_specs=pl.BlockSpec((1, nh, HEAD_DIM), per_b),
    )
    return pl.pallas_call(
        functools.partial(_dsa_sample_kernel, n_pages=n_pages, nh=nh, kvh=kvh),
        grid_spec=grid_spec,
        out_shape=jax.ShapeDtypeStruct((db, nh, HEAD_DIM), BF16),
        compiler_params=_cparams(("arbitrary",)),
        name="dsa_sample_attend",
    )(page_table, q, bias, k_new, v_new, *([pool_k] * n_pages), *([pool_v] * n_pages))


def _cross_kernel(q_ref, mk_ref, mv_ref, o_ref, *, mh, tq):
    scale = float(HEAD_DIM) ** -0.5
    for h in range(mh):
        cs = slice(h * HEAD_DIM, (h + 1) * HEAD_DIM)
        qh = q_ref[0, :, cs].astype(BF16)
        if tq < SUBLANE:
            qh = jnp.broadcast_to(qh[0:1], (SUBLANE, HEAD_DIM))
        kh = mk_ref[0, :, cs].astype(BF16)
        vh = mv_ref[0, :, cs].astype(BF16)
        lg = lax.dot_general(qh, kh, NT_DIMS, preferred_element_type=F32) * scale
        p = jnp.exp(lg - jnp.max(lg, axis=-1, keepdims=True))
        den = jnp.sum(p, axis=-1, keepdims=True)
        o = jnp.dot(p.astype(BF16), vh, preferred_element_type=F32) / den
        o_ref[0, :, cs] = o[0:tq].astype(o_ref.dtype)


def _cross_attend(q, mk, mv, mh):
    b, t, md = q.shape
    nm = mk.shape[1]
    tq = _tile(t, 512, SUBLANE)
    return pl.pallas_call(
        functools.partial(_cross_kernel, mh=mh, tq=tq),
        grid=(b, t // tq),
        in_specs=[pl.BlockSpec((1, tq, md), lambda bb, i: (bb, i, 0)),
                  pl.BlockSpec((1, nm, md), lambda bb, i: (bb, 0, 0)),
                  pl.BlockSpec((1, nm, md), lambda bb, i: (bb, 0, 0))],
        out_specs=pl.BlockSpec((1, tq, md), lambda bb, i: (bb, i, 0)),
        out_shape=jax.ShapeDtypeStruct((b, t, md), BF16),
        compiler_params=_cparams(("parallel", "parallel")),
        name="cross_attend",
    )(q, mk, mv)


def _cross_rows_kernel(q_ref, mk_ref, mv_ref, o_ref, *, mh):
    scale = float(HEAD_DIM) ** -0.5
    rows = mk_ref.shape[0]
    q = q_ref[0].astype(BF16)
    lg = lax.dot_general(q, mk_ref[...].astype(BF16), NT_DIMS, preferred_element_type=F32) * scale
    col = lax.broadcasted_iota(jnp.int32, (SUBLANE, rows), 1)
    head = lax.broadcasted_iota(jnp.int32, (SUBLANE, rows), 0)
    lg = jnp.where(col % mh == head % mh, lg, -jnp.inf)
    p = jnp.exp(lg - jnp.max(lg, axis=-1, keepdims=True))
    den = jnp.sum(p, axis=-1, keepdims=True)
    o = jnp.dot(p.astype(BF16), mv_ref[...].astype(BF16), preferred_element_type=F32) / den
    o_ref[0] = o.astype(o_ref.dtype)


def _cross_attend_rows(q, mk, mv, mh, n_mem):
    b = q.shape[0]
    assert mh <= SUBLANE
    q8 = jnp.pad(q.reshape(b, mh, HEAD_DIM), ((0, 0), (0, SUBLANE - mh), (0, 0)))
    rows = n_mem * mh
    o = pl.pallas_call(
        functools.partial(_cross_rows_kernel, mh=mh),
        grid=(b,),
        in_specs=[pl.BlockSpec((1, SUBLANE, HEAD_DIM), lambda bb: (bb, 0, 0)),
                  pl.BlockSpec((rows, HEAD_DIM), lambda bb: (bb, 0)),
                  pl.BlockSpec((rows, HEAD_DIM), lambda bb: (bb, 0))],
        out_specs=pl.BlockSpec((1, SUBLANE, HEAD_DIM), lambda bb: (bb, 0, 0)),
        out_shape=jax.ShapeDtypeStruct((b, SUBLANE, HEAD_DIM), BF16),
        compiler_params=_cparams(("parallel",)),
        name="cross_attend_rows",
    )(q8, mk, mv)
    return o[:, :mh].reshape(b, mh * HEAD_DIM)


def _head_major_to_channel_major(x, h):
    lead = x.shape[:-1]
    return x.reshape(lead + (h, RWKV_HEAD_DIM)).swapaxes(-1, -2).reshape(x.shape)


def _channel_major_to_head_major(x, h):
    lead = x.shape[:-1]
    return x.reshape(lead + (RWKV_HEAD_DIM, h)).swapaxes(-1, -2).reshape(x.shape)


def _rwkv_cols(x, h, rd, lwd, lad, fn):
    o_k = rd + lwd
    o_a = o_k + 2 * rd
    return jnp.concatenate([fn(x[..., :rd], h), x[..., rd:o_k], fn(x[..., o_k:o_k + rd], h),
                            fn(x[..., o_k + rd:o_a], h), x[..., o_a:]], axis=-1)


def _to_chains(x, b, t, h, ts):
    nc = b * h
    y = x.reshape(b, t // ts, ts, RWKV_HEAD_DIM, h).transpose(1, 3, 2, 0, 4)
    y = y.reshape(t // ts, RWKV_HEAD_DIM * ts, nc)
    return jnp.pad(y, ((0, 0), (0, 0), (0, _round_up(nc, LANE) - nc)))


def _from_chains(y, b, t, h, ts):
    nc = b * h
    y = y[:, :, :nc].reshape(t // ts, RWKV_HEAD_DIM, ts, b, h).transpose(3, 0, 2, 1, 4)
    return y.reshape(b * t, RWKV_HEAD_DIM * h)


def _param_chains(p, b, h):
    nc = b * h
    y = jnp.tile(p.reshape(h, RWKV_HEAD_DIM).T, (1, b))
    return jnp.pad(y, ((0, 0), (0, _round_up(nc, LANE) - nc)))


def _rope_tables(pos):
    half = HEAD_DIM // 2
    inv = jnp.power(ROPE_THETA, -jnp.arange(half, dtype=F32) / half)
    ang = pos.astype(F32)[:, None] * inv[None, :]
    cos, sin = jnp.cos(ang), jnp.sin(ang)
    return jnp.concatenate([cos, cos], axis=-1), jnp.concatenate([-sin, sin], axis=-1)


def _mixer_front(xp, xs, shape_p, shape_s, pos_p, pos_s, lw, dims):
    d, rd, lwd, lad, lgp, rp, rpp, nh, kvh, ih = dims
    hp = _rmsnorm(xp, lw["norm_mix"], BF16)
    hs = _rmsnorm(xs, lw["norm_mix"], BF16)
    gates = _mm_pair(hp, hs, lw["w_in"], n=2 * d, tn_pref=512)
    p = _mm_pair(hp, hs, lw["w_rwkv"])
    fronts = []
    for g, (hn, (b, t), pos) in enumerate(((hp, shape_p, pos_p), (hs, shape_s, pos_s))):
        m = b * t
        att = _mm(hn, lw["w_attn"])
        cos2, sin2 = _rope_tables(pos)
        cos2 = jnp.broadcast_to(cos2[None], (b, t, LANE)).reshape(m, LANE)
        sin2 = jnp.broadcast_to(sin2[None], (b, t, LANE)).reshape(m, LANE)
        q, kr, vv, qi, ki, wi = _attn_prep(att, cos2, sin2, lw["idx_ln_g"], lw["idx_ln_b"], nh, kvh, ih)
        fronts.append({"gates": gates[g], "p": p[g], "q": q, "k": kr, "v": vv, "qi": qi, "ki": ki, "wi": wi})
    return fronts


def _rwkv_branch(p, b, t, shift_prev, s0, lw, dims):
    d, rd, lwd, lad, lgp, rp, rpp, nh, kvh, ih = dims
    h = rd // RWKV_HEAD_DIM
    p3 = p.reshape(b, t, rpp)
    prev = jnp.pad(_rwkv_cols(shift_prev, h, rd, lwd, lad, _head_major_to_channel_major), ((0, 0), (0, rpp - rp)))
    r, w, k, v, a, g = _rwkv_prep(p, prev, b, t, lw["mu"], lw["w0"], lw["w2"], lw["a0"], lw["a2"], lw["g2"],
                                  rd, lwd, lad, lgp)
    nc = b * h
    in_kernel = _chain_kernels_fit(b, t, h)
    ts = SUBLANE if t % SUBLANE == 0 else 1
    if not in_kernel:
        r, w, k, v, a = [_to_chains(z, b, t, h, ts) for z in (r, w, k, v, a)]
    s0c = jnp.pad(s0.reshape(nc, RWKV_HEAD_DIM, RWKV_HEAD_DIM).transpose(2, 1, 0),
                  ((0, 0), (0, 0), (0, _round_up(nc, LANE) - nc)))
    yc, sc = _rwkv_scan(r, w, k, v, a,
                        _param_chains(lw["kk"], b, h), _param_chains(lw["ka"], b, h),
                        _param_chains(lw["rk"], b, h), _param_chains(lw["ln_g"], b, h),
                        _param_chains(lw["ln_b"], b, h), s0c)
    if in_kernel:
        yg = _rwkv_post(yc, g, b, t, h)
    else:
        yg = _gate_mul(_from_chains(yc, b, t, h, ts), g)
    s_fin = sc[:, :, :nc].transpose(2, 1, 0).reshape(b, h, RWKV_HEAD_DIM, RWKV_HEAD_DIM)
    shift_new = _rwkv_cols(p3[:, -1, :rp], h, rd, lwd, lad, _channel_major_to_head_major)
    return yg, s_fin, shift_new


def _mixer_back(x, f, yg, ya, lw):
    merged = _merge(yg, ya, f["gates"], lw["w_br_rwkv"], lw["w_br_attn"])
    return _mm_resid(merged, lw["w_out"], x)


def _attend_prompt(f, b, s, nh, kvh, ih):
    n_sel = min(TOPK_MAX, s // 4)
    nq = s // Q_BLOCK
    wit = f["wi"][:, :ih].reshape(b, nq, Q_BLOCK, ih).transpose(0, 1, 3, 2)
    return _dsa_prompt(f["qi"], wit, f["ki"].reshape(b, s, INDEX_DIM), f["q"],
                       f["k"].reshape(b, s, kvh * HEAD_DIM), f["v"].reshape(b, s, kvh * HEAD_DIM),
                       b, s, n_sel, ih, nh, kvh)


def _attend_sample(f, pool_k, pool_v, pool_idx, page_table, nh, kvh, ih):
    q, qi, wi, kr, vv, ki = f["q"], f["qi"], f["wi"], f["k"], f["v"], f["ki"]
    db, n_pages = page_table.shape
    psz = pool_idx.shape[1]
    past = n_pages * psz
    n_sel = min(TOPK_MAX, (past + 1) // 4)
    kvd = kvh * HEAD_DIM
    sc_past, sc_new = _dsa_scores(page_table, qi.reshape(db, ih, INDEX_DIM),
                                  wi[:, :ih].reshape(db, ih, 1), ki.reshape(db, 1, INDEX_DIM), pool_idx)
    lp = _round_up(past + 1, PREFIX_CHUNK)
    dbp = _round_up(db, LANE)
    assert (lp - past) * kvh >= LANE
    score = jnp.concatenate([sc_past.reshape(db, past), sc_new[:, 0, :1]], axis=1)
    score_t = jnp.pad(score.T, ((0, lp - past - 1), (0, dbp - db)), constant_values=-jnp.inf)
    mask = _select_mask(score_t, n_sel)[:, :db].T
    bias = jnp.repeat(jnp.where(mask > 0.5, 0.0, -jnp.inf).astype(F32), kvh, axis=1).reshape(db, 1, lp * kvh)
    new_rows = lambda x: jnp.pad(x.reshape(db, kvh, HEAD_DIM), ((0, 0), (0, LANE - kvh), (0, 0)))
    o = _dsa_sample(page_table, q.reshape(db, nh, HEAD_DIM), bias, new_rows(kr), new_rows(vv),
                    pool_k, pool_v, nh, kvh, psz)
    return o.reshape(db, nh * HEAD_DIM)


def _cross(x, b, t, mk, mv, lw, mh, n_mem):
    hc = _rmsnorm(x, lw["norm_cross"], BF16)
    qm = _mm(hc, lw["w_q_mem"])
    if mk.ndim == 2:
        o = _cross_attend_rows(qm, mk, mv, mh, n_mem)
    else:
        o = _cross_attend(qm.reshape(b, t, -1), mk, mv, mh).reshape(b * t, -1)
    return _mm_resid(o, lw["w_o_mem"], x)


def _ffn(xp, xs, lw):
    hp = _rmsnorm(xp, lw["norm_ffn"], BF16)
    hs = _rmsnorm(xs, lw["norm_ffn"], BF16)
    act_p, act_s = _swiglu_pair(hp, hs, lw["w_up"])
    return (_mm_resid(act_p, lw["w_down"], xp, tn_pref=512, tk_pref=5504),
            _mm_resid(act_s, lw["w_down"], xs, tn_pref=512, tk_pref=5504))


def kernel(x_prompt, mem_prompt, x_sample, cache_k, cache_v, cache_idx_k, cache_mem_k, cache_mem_v, state_rwkv, state_shift, page_table, norm_mix, w_in, rwkv_mu, rwkv_w0, rwkv_w2, rwkv_a0, rwkv_a2, rwkv_g2, rwkv_kk, rwkv_ka, rwkv_rk, rwkv_ln_g, rwkv_ln_b, idx_ln_g, idx_ln_b, w_br_rwkv, w_br_attn, w_out, norm_cross, norm_mem, w_q_mem, w_k_mem, w_v_mem, w_o_mem, norm_ffn, w_up, w_down, final_norm):
    b, s, d = x_prompt.shape
    db, ts, _ = x_sample.shape
    assert ts == 1, "the sample group decodes one token per sequence"
    depth = w_in.shape[0]
    in_proj = w_in.shape[2]
    rp = rwkv_mu.shape[1]
    lwd, rd = rwkv_w2.shape[1:]
    lad = rwkv_a2.shape[1]
    lg = rwkv_g2.shape[1]
    lgp = _round_up(lg, LANE)
    rpp = rp - lg + lgp
    ad = w_br_attn.shape[1]
    nh = ad // HEAD_DIM
    kvh = cache_k.shape[3]
    kvd = kvh * HEAD_DIM
    ih = (in_proj - 2 * d - rp - ad - 2 * kvd - INDEX_DIM) // (INDEX_DIM + 1)
    mh = cache_mem_k.shape[3]
    n_mem = mem_prompt.shape[1]
    past = page_table.shape[1] * cache_k.shape[2]
    assert rp == 3 * rd + lwd + lad + lg and s % Q_BLOCK == 0 and ih % 2 == 0
    dims = (d, rd, lwd, lad, lgp, rp, rpp, nh, kvh, ih)
    n_pool, psz = cache_k.shape[1:3]
    pool_k = cache_k.reshape(depth * n_pool * psz * kvh, HEAD_DIM)
    pool_v = cache_v.reshape(depth * n_pool * psz * kvh, HEAD_DIM)
    pool_idx = cache_idx_k.reshape(depth * n_pool, psz, INDEX_DIM)
    mem_rows = db * n_mem * mh
    mem_k = cache_mem_k.reshape(depth * mem_rows, HEAD_DIM)
    mem_v = cache_mem_v.reshape(depth * mem_rows, HEAD_DIM)
    hr = rd // RWKV_HEAD_DIM
    cm = functools.partial(_head_major_to_channel_major, h=hr)

    xp = x_prompt.reshape(b * s, d)
    xs = x_sample.reshape(db, d)
    pos_p = jnp.arange(s, dtype=jnp.int32)
    pos_s = past + jnp.arange(ts, dtype=jnp.int32)
    outs = [[] for _ in range(12)]
    for l in range(depth):
        wl = w_in[l]
        o_att = 2 * d + rp
        n_att = ad + 2 * kvd + ih * INDEX_DIM + INDEX_DIM
        lw = {
            "norm_mix": norm_mix[l],
            "w_in": wl,
            "w_rwkv": jnp.pad(_rwkv_cols(wl[:, 2 * d:o_att], hr, rd, lwd, lad, _head_major_to_channel_major),
                              ((0, 0), (0, rpp - rp))).astype(BF16),
            "w_attn": jnp.pad(wl[:, o_att:], ((0, 0), (0, LANE - ih))).astype(BF16),
            "mu": jnp.pad(_rwkv_cols(rwkv_mu[l], hr, rd, lwd, lad, _head_major_to_channel_major),
                          (0, rpp - rp)).reshape(1, rpp),
            "w0": cm(rwkv_w0[l]).reshape(1, rd), "w2": cm(rwkv_w2[l]).astype(BF16),
            "a0": cm(rwkv_a0[l]).reshape(1, rd), "a2": cm(rwkv_a2[l]).astype(BF16),
            "g2": jnp.pad(cm(rwkv_g2[l]), ((0, lgp - lg), (0, 0))).astype(BF16),
            "kk": rwkv_kk[l], "ka": rwkv_ka[l], "rk": rwkv_rk[l].reshape(-1),
            "ln_g": rwkv_ln_g[l], "ln_b": rwkv_ln_b[l],
            "idx_ln_g": idx_ln_g[l].reshape(1, INDEX_DIM), "idx_ln_b": idx_ln_b[l].reshape(1, INDEX_DIM),
            "w_br_rwkv": cm(w_br_rwkv[l].T).T.astype(BF16), "w_br_attn": w_br_attn[l],
            "w_out": w_out[l],
            "norm_cross": norm_cross[l], "norm_ffn": norm_ffn[l],
            "w_q_mem": w_q_mem[l], "w_o_mem": w_o_mem[l],
            "w_up": w_up[l], "w_down": w_down[l].astype(BF16),
        }
        assert n_att + ih == in_proj - o_att
        fp, fs = _mixer_front(xp, xs, (b, s), (db, ts), pos_p, pos_s, lw, dims)
        ygp, sfp, shp = _rwkv_branch(fp["p"], b, s, jnp.zeros((b, rp), F32),
                                     jnp.zeros((b, hr, RWKV_HEAD_DIM, RWKV_HEAD_DIM), F32), lw, dims)
        xp = _mixer_back(xp, fp, ygp, _attend_prompt(fp, b, s, nh, kvh, ih), lw)
        mn = _rmsnorm(mem_prompt.reshape(b * n_mem, d), norm_mem[l], BF16)
        mkp = _mm(mn, w_k_mem[l]).reshape(b, n_mem, mh * HEAD_DIM)
        mvp = _mm(mn, w_v_mem[l]).reshape(b, n_mem, mh * HEAD_DIM)
        xp = _cross(xp, b, s, mkp, mvp, lw, mh, n_mem)
        ya_s = _attend_sample(fs, pool_k, pool_v, pool_idx, page_table + l * n_pool, nh, kvh, ih)
        ygs, sfs, shs = _rwkv_branch(fs["p"], db, ts, state_shift[l], state_rwkv[l], lw, dims)
        xs = _mixer_back(xs, fs, ygs, ya_s, lw)
        xs = _cross(xs, db, ts, mem_k[l * mem_rows:(l + 1) * mem_rows],
                    mem_v[l * mem_rows:(l + 1) * mem_rows], lw, mh, n_mem)
        xp, xs = _ffn(xp, xs, lw)
        vals = (fp["k"].reshape(b, s, kvh, HEAD_DIM), fp["v"].reshape(b, s, kvh, HEAD_DIM),
                fp["ki"].reshape(b, s, INDEX_DIM),
                mkp.reshape(b, n_mem, mh, HEAD_DIM), mvp.reshape(b, n_mem, mh, HEAD_DIM), sfp, shp,
                fs["k"].reshape(db, ts, kvh, HEAD_DIM), fs["v"].reshape(db, ts, kvh, HEAD_DIM),
                fs["ki"].reshape(db, ts, INDEX_DIM), sfs, shs)
        for lst, val in zip(outs, vals):
            lst.append(val)
    y_prompt = _rmsnorm(xp, final_norm, F32).reshape(b, s, d)
    y_sample = _rmsnorm(xs, final_norm, F32).reshape(db, ts, d)
    return (y_prompt, y_sample) + tuple(jnp.stack(o) for o in outs)
```

```python
import functools

import jax
import jax.numpy as jnp
from jax import lax
from jax.experimental import pallas as pl
from jax.experimental.pallas import tpu as pltpu

F32 = jnp.float32
BF16 = jnp.bfloat16
LANE = 128
SUBLANE = 8
VMEM_LIMIT = 56 * 1024 * 1024

RWKV_HEAD_DIM = 64
HEAD_DIM = 128
INDEX_DIM = 128
TOPK_MAX = 256
Q_BLOCK = 128
ROPE_THETA = 10000.0
NORM_EPS = 1e-6
RWKV_LN_EPS = 64e-5
IDX_LN_EPS = 1e-6
PREFIX_CHUNK = 256
COUNT_ROWS = 64
MAX_CAUSAL_BUCKETS = 8
PREP_STEPS = 32
CROSS_SEQS_PER_STEP = 8
NT_DIMS = (((1,), (1,)), ((), ()))


def _cparams(sem):
    return pltpu.CompilerParams(dimension_semantics=sem, vmem_limit_bytes=VMEM_LIMIT)


def _round_up(n, m):
    return (n + m - 1) // m * m


def _tile(n, pref, unit):
    best = None
    t = unit
    while t <= min(n, pref):
        if n % t == 0:
            best = t
        t += unit
    return best if best is not None else n


def _rmsnorm_kernel(x_ref, g_ref, o_ref):
    x = x_ref[...]
    ms = jnp.mean(x * x, axis=-1, keepdims=True)
    o_ref[...] = (x * lax.rsqrt(ms + NORM_EPS) * g_ref[...]).astype(o_ref.dtype)


def _rmsnorm(x, g, out_dtype):
    m, d = x.shape
    tm = _tile(m, 256, SUBLANE)
    return pl.pallas_call(
        _rmsnorm_kernel,
        grid=(m // tm,),
        in_specs=[pl.BlockSpec((tm, d), lambda i: (i, 0)), pl.BlockSpec((1, d), lambda i: (0, 0))],
        out_specs=pl.BlockSpec((tm, d), lambda i: (i, 0)),
        out_shape=jax.ShapeDtypeStruct((m, d), out_dtype),
        compiler_params=_cparams(("parallel",)),
        name="rmsnorm",
    )(x, g.reshape(1, d))


MXU_COLS = 256


def _col_tile(n, pref):
    return _tile(n, pref, MXU_COLS if n % MXU_COLS == 0 else LANE)


def _mm_kernel(x_ref, w_ref, o_ref):
    o_ref[...] = jnp.dot(x_ref[...], w_ref[...].astype(BF16), preferred_element_type=F32).astype(o_ref.dtype)


def _mm(x, w, n=None, out_dtype=F32, tm_pref=1024, tn_pref=768):
    m, k = x.shape
    n = w.shape[1] if n is None else n
    tm = _tile(m, tm_pref, SUBLANE)
    tn = _col_tile(n, tn_pref)
    return pl.pallas_call(
        _mm_kernel,
        grid=(m // tm, n // tn),
        in_specs=[pl.BlockSpec((tm, k), lambda i, j: (i, 0)), pl.BlockSpec((k, tn), lambda i, j: (0, j))],
        out_specs=pl.BlockSpec((tm, tn), lambda i, j: (i, j)),
        out_shape=jax.ShapeDtypeStruct((m, n), out_dtype),
        compiler_params=_cparams(("parallel", "parallel")),
        name="matmul",
    )(x, w)


def _norm_mm_kernel(x_ref, g_ref, w_ref, o_ref):
    x = x_ref[...]
    ms = jnp.mean(x * x, axis=-1, keepdims=True)
    h = (x * lax.rsqrt(ms + NORM_EPS) * g_ref[...]).astype(BF16)
    o_ref[...] = jnp.dot(h, w_ref[...].astype(BF16), preferred_element_type=F32)


def _norm_mm(x, g, w, tm_pref=512):
    m, k = x.shape
    n = w.shape[1]
    tm = _tile(m, tm_pref, SUBLANE)
    return pl.pallas_call(
        _norm_mm_kernel,
        grid=(m // tm,),
        in_specs=[pl.BlockSpec((tm, k), lambda i: (i, 0)), pl.BlockSpec((1, k), lambda i: (0, 0)),
                  pl.BlockSpec((k, n), lambda i: (0, 0))],
        out_specs=pl.BlockSpec((tm, n), lambda i: (i, 0)),
        out_shape=jax.ShapeDtypeStruct((m, n), F32),
        compiler_params=_cparams(("parallel",)),
        name="norm_matmul",
    )(x, g.reshape(1, k), w)


def _mm_resid_kernel(x_ref, w_ref, r_ref, o_ref, acc_ref, *, nk):
    kk = pl.program_id(2)

    @pl.when(kk == 0)
    def _():
        acc_ref[...] = r_ref[...]

    acc_ref[...] += jnp.dot(x_ref[...], w_ref[...].astype(BF16), preferred_element_type=F32)

    @pl.when(kk == nk - 1)
    def _():
        o_ref[...] = acc_ref[...]


def _mm_resid(x, w, resid, tm_pref=1024, tn_pref=1024, tk_pref=2048):
    m, k = x.shape
    n = w.shape[1]
    tm = _tile(m, tm_pref, SUBLANE)
    tn = _tile(n, tn_pref, LANE)
    tk = _tile(k, tk_pref, LANE)
    nk = k // tk
    return pl.pallas_call(
        functools.partial(_mm_resid_kernel, nk=nk),
        grid=(m // tm, n // tn, nk),
        in_specs=[pl.BlockSpec((tm, tk), lambda i, j, kk: (i, kk)),
                  pl.BlockSpec((tk, tn), lambda i, j, kk: (kk, j)),
                  pl.BlockSpec((tm, tn), lambda i, j, kk: (i, j))],
        out_specs=pl.BlockSpec((tm, tn), lambda i, j, kk: (i, j)),
        out_shape=jax.ShapeDtypeStruct((m, n), F32),
        scratch_shapes=[pltpu.VMEM((tm, tn), F32)],
        compiler_params=_cparams(("parallel", "parallel", "arbitrary")),
        name="matmul_resid",
    )(x, w, resid)


def _swiglu_kernel(x_ref, wg_ref, wu_ref, o_ref):
    x = x_ref[...]
    g = jnp.dot(x, wg_ref[...].astype(BF16), preferred_element_type=F32)
    u = jnp.dot(x, wu_ref[...].astype(BF16), preferred_element_type=F32)
    o_ref[...] = (g * jax.nn.sigmoid(g) * u).astype(o_ref.dtype)


def _swiglu(x, w_up, tm_pref=1024, tn_pref=256):
    m, k = x.shape
    f = w_up.shape[1] // 2
    tm = _tile(m, tm_pref, SUBLANE)
    tn = _tile(f, tn_pref, LANE)
    nb = f // tn
    return pl.pallas_call(
        _swiglu_kernel,
        grid=(m // tm, nb),
        in_specs=[pl.BlockSpec((tm, k), lambda i, j: (i, 0)),
                  pl.BlockSpec((k, tn), lambda i, j: (0, j)),
                  pl.BlockSpec((k, tn), lambda i, j: (0, j + nb))],
        out_specs=pl.BlockSpec((tm, tn), lambda i, j: (i, j)),
        out_shape=jax.ShapeDtypeStruct((m, f), BF16),
        compiler_params=_cparams(("parallel", "parallel")),
        name="swiglu",
    )(x, w_up, w_up)


def _gate_mul_kernel(y_ref, g_ref, o_ref):
    o_ref[...] = (y_ref[...] * g_ref[...]).astype(o_ref.dtype)


def _gate_mul(y, g):
    m, n = y.shape
    tm = _tile(m, 256, SUBLANE)
    blk = pl.BlockSpec((tm, n), lambda i: (i, 0))
    return pl.pallas_call(
        _gate_mul_kernel,
        grid=(m // tm,),
        in_specs=[blk, blk],
        out_specs=blk,
        out_shape=jax.ShapeDtypeStruct((m, n), BF16),
        compiler_params=_cparams(("parallel",)),
        name="gate_mul",
    )(y, g)


def _merge_kernel(yr_ref, ya_ref, gr_ref, ga_ref, wr_ref, wa_ref, o_ref):
    br = jnp.dot(yr_ref[...], wr_ref[...].astype(BF16), preferred_element_type=F32)
    ba = jnp.dot(ya_ref[...], wa_ref[...].astype(BF16), preferred_element_type=F32)
    o_ref[...] = (jax.nn.sigmoid(gr_ref[...]) * br + jax.nn.sigmoid(ga_ref[...]) * ba).astype(o_ref.dtype)


def _merge(yr, ya, gates, w_br_r, w_br_a, tm_pref=1024, tn_pref=512):
    m, rd = yr.shape
    ad = ya.shape[1]
    d = w_br_r.shape[1]
    tm = _tile(m, tm_pref, SUBLANE)
    tn = _tile(d, tn_pref, LANE)
    nb = d // tn
    return pl.pallas_call(
        _merge_kernel,
        grid=(m // tm, nb),
        in_specs=[pl.BlockSpec((tm, rd), lambda i, j: (i, 0)),
                  pl.BlockSpec((tm, ad), lambda i, j: (i, 0)),
                  pl.BlockSpec((tm, tn), lambda i, j: (i, j)),
                  pl.BlockSpec((tm, tn), lambda i, j: (i, j + nb)),
                  pl.BlockSpec((rd, tn), lambda i, j: (0, j)),
                  pl.BlockSpec((ad, tn), lambda i, j: (0, j))],
        out_specs=pl.BlockSpec((tm, tn), lambda i, j: (i, j)),
        out_shape=jax.ShapeDtypeStruct((m, d), BF16),
        compiler_params=_cparams(("parallel", "parallel")),
        name="merge",
    )(yr, ya, gates, gates, w_br_r, w_br_a)


def _lane_segment(h):
    return lax.broadcasted_iota(jnp.int32, (SUBLANE, LANE), 1) // h


def _to_chain_tiles(x, out_ref, nbatch, tt, h):
    per = LANE // h
    seg = _lane_segment(h)
    for c in range(nbatch // per):
        for g in range(tt // SUBLANE):
            for col in range(RWKV_HEAD_DIM // per):
                tiles = [x[(c * per + q) * tt + g * SUBLANE:(c * per + q) * tt + (g + 1) * SUBLANE,
                           col * LANE:(col + 1) * LANE] for q in range(per)]
                for jj in range(per):
                    acc = None
                    for q in range(per):
                        sh = ((q - jj) % per) * h
                        t = tiles[q] if sh == 0 else pltpu.roll(tiles[q], sh, axis=1)
                        acc = t if acc is None else jnp.where(seg == q, t, acc)
                    n = col * per + jj
                    out_ref[g, n * SUBLANE:(n + 1) * SUBLANE, c * LANE:(c + 1) * LANE] = acc


def _rwkv_prep_kernel(p_ref, sh_ref, *rest, rd, lw, la, lgp, chain):
    if chain is not None:
        prev_ref, rest = rest[0], rest[1:]
    mu_ref, w0_ref, w2_ref, a0_ref, a2_ref, g2_ref, r_ref, w_ref, k_ref, v_ref, a_ref, g_ref = rest
    if chain is None:
        p = p_ref[...]
        shifted = sh_ref[...]
    else:
        nbatch, tt, h = chain
        first = pl.program_id(0) == 0
        row0 = lax.broadcasted_iota(jnp.int32, (tt, p_ref.shape[2]), 0) == 0
        ps, shs = [], []
        for bb in range(nbatch):
            pb = p_ref[bb]
            above = jnp.where(first, prev_ref[bb], sh_ref[bb, SUBLANE - 1:SUBLANE, :])
            ps.append(pb)
            shs.append(jnp.where(row0, above, pltpu.roll(pb, 1, axis=0)))
        p = jnp.concatenate(ps, axis=0)
        shifted = jnp.concatenate(shs, axis=0)
    xm = p + (shifted - p) * mu_ref[...]
    o = 0
    xr = xm[:, o:o + rd]; o += rd
    xw = xm[:, o:o + lw]; o += lw
    xk = xm[:, o:o + rd]; o += rd
    xv = xm[:, o:o + rd]; o += rd
    xa = xm[:, o:o + la]; o += la
    xg = xm[:, o:o + lgp]
    z = w0_ref[...] + jnp.dot(jnp.tanh(xw).astype(BF16), w2_ref[...], preferred_element_type=F32)
    sp = jnp.maximum(-z, 0.0) + jnp.log(1.0 + jnp.exp(-jnp.abs(z)))
    w_raw = -sp - 0.5
    decay = jnp.exp(-jnp.exp(w_raw))
    a = jax.nn.sigmoid(a0_ref[...] + jnp.dot(xa.astype(BF16), a2_ref[...], preferred_element_type=F32))
    g = jnp.dot(jax.nn.sigmoid(xg).astype(BF16), g2_ref[...], preferred_element_type=F32)
    if chain is None:
        r_ref[...] = xr
        w_ref[...] = decay
        k_ref[...] = xk
        v_ref[...] = xv
        a_ref[...] = a
        g_ref[...] = g
    else:
        for x, ref in ((xr, r_ref), (decay, w_ref), (xk, k_ref), (xv, v_ref), (a, a_ref)):
            _to_chain_tiles(x, ref, nbatch, tt, h)
        for bb in range(nbatch):
            g_ref[bb] = g[bb * tt:(bb + 1) * tt]


def _chain_kernels_fit(b, t, h):
    return LANE % h == 0 and b % (LANE // h) == 0 and t % PREP_STEPS == 0


def _rwkv_prep(p, prev, b, t, mu, w0, w2, a0, a2, g2, rd, lw, la, lgp):
    m, rpp = p.shape
    h = rd // RWKV_HEAD_DIM
    fix = lambda i: (0, 0)
    par_specs = [pl.BlockSpec((1, rpp), fix), pl.BlockSpec((1, rd), fix), pl.BlockSpec((lw, rd), fix),
                 pl.BlockSpec((1, rd), fix), pl.BlockSpec((la, rd), fix), pl.BlockSpec((lgp, rd), fix)]
    if not _chain_kernels_fit(b, t, h):
        shifted = jnp.concatenate([prev[:, None, :], p.reshape(b, t, rpp)[:, :-1]], axis=1).reshape(m, rpp)
        tm = _tile(m, 128, SUBLANE)
        row = lambda i: (i, 0)
        out = jax.ShapeDtypeStruct((m, rd), F32)
        return pl.pallas_call(
            functools.partial(_rwkv_prep_kernel, rd=rd, lw=lw, la=la, lgp=lgp, chain=None),
            grid=(m // tm,),
            in_specs=[pl.BlockSpec((tm, rpp), row), pl.BlockSpec((tm, rpp), row)] + par_specs,
            out_specs=[pl.BlockSpec((tm, rd), row)] * 6,
            out_shape=[out] * 6,
            compiler_params=_cparams(("parallel",)),
            name="rwkv_prep",
        )(p, shifted, mu, w0, w2, a0, a2, g2)
    tt = PREP_STEPS
    nc = b * h
    p3 = p.reshape(b, t, rpp)
    step_blk = lambda i: (0, i, 0)
    chain_out = jax.ShapeDtypeStruct((t // SUBLANE, RWKV_HEAD_DIM * SUBLANE, nc), F32)
    chain_spec = pl.BlockSpec((tt // SUBLANE, RWKV_HEAD_DIM * SUBLANE, nc), lambda i: (i, 0, 0))
    outs = pl.pallas_call(
        functools.partial(_rwkv_prep_kernel, rd=rd, lw=lw, la=la, lgp=lgp, chain=(b, tt, h)),
        grid=(t // tt,),
        in_specs=[pl.BlockSpec((b, tt, rpp), step_blk),
                  pl.BlockSpec((b, SUBLANE, rpp), lambda i: (0, jnp.maximum(i * (tt // SUBLANE) - 1, 0), 0)),
                  pl.BlockSpec((b, 1, rpp), lambda i: (0, 0, 0))] + par_specs,
        out_specs=[chain_spec] * 5 + [pl.BlockSpec((b, tt, rd), step_blk)],
        out_shape=[chain_out] * 5 + [jax.ShapeDtypeStruct((b, t, rd), F32)],
        compiler_params=_cparams(("parallel",)),
        name="rwkv_prep",
    )(p3, p3, prev.reshape(b, 1, rpp), mu, w0, w2, a0, a2, g2)
    return list(outs[:5]) + [outs[5].reshape(m, rd)]


def _rwkv_scan_kernel(r_ref, w_ref, k_ref, v_ref, a_ref, kkw_ref, ka_ref, rk_ref, lng_ref, lnb_ref,
                      s0_ref, y_ref, s_ref, kk_s, nb_s, kp_s, *, ts):
    n = RWKV_HEAD_DIM

    @pl.when(pl.program_id(1) == 0)
    def _():
        s_ref[...] = s0_ref[...]

    def tile(j):
        return slice(j * ts, (j + 1) * ts)

    ss = None
    for j in range(n):
        kk = k_ref[0, tile(j), :] * kkw_ref[j:j + 1, :]
        kk_s[tile(j), :] = kk
        ss = kk * kk if ss is None else ss + kk * kk
    inv = 1.0 / jnp.maximum(jnp.sqrt(ss), 1e-12)
    bonus = None
    for j in range(n):
        kj = k_ref[0, tile(j), :]
        aj = a_ref[0, tile(j), :]
        kk = kk_s[tile(j), :] * inv
        kk_s[tile(j), :] = kk
        nb_s[tile(j), :] = -(kk * aj)
        kp = kj * (1.0 + (aj - 1.0) * ka_ref[j:j + 1, :])
        kp_s[tile(j), :] = kp
        term = r_ref[0, tile(j), :] * kp * rk_ref[j:j + 1, :]
        bonus = term if bonus is None else bonus + term

    def row(ref, j, tt):
        return ref[j * ts + tt:j * ts + tt + 1, :]

    def row3(ref, j, tt):
        return ref[0, j * ts + tt:j * ts + tt + 1, :]

    lng = lng_ref[...]
    lnb = lnb_ref[...]
    sk = None
    for j in range(n):
        term = s_ref[j] * row(kk_s, j, 0)
        sk = term if sk is None else sk + term
    for tt in range(ts):
        v = v_ref[0, pl.ds(tt, n, stride=ts), :]
        y = None
        sk_next = None
        for j in range(n):
            sn = s_ref[j] * row3(w_ref, j, tt) + sk * row(nb_s, j, tt) + v * row(kp_s, j, tt)
            s_ref[j] = sn
            yt = sn * row3(r_ref, j, tt)
            y = yt if y is None else y + yt
            if tt + 1 < ts:
                st = sn * row(kk_s, j, tt + 1)
                sk_next = st if sk_next is None else sk_next + st
        sk = sk_next
        mu = jnp.mean(y, axis=0, keepdims=True)
        yc = y - mu
        var = jnp.mean(yc * yc, axis=0, keepdims=True)
        yn = yc * lax.rsqrt(var + RWKV_LN_EPS) * lng + lnb
        y_ref[0, pl.ds(tt, n, stride=ts), :] = yn + bonus[tt:tt + 1, :] * v


def _rwkv_scan(r, w, k, v, a, kkw, ka, rk, lng, lnb, s0):
    nt, rows, nc = r.shape
    n = RWKV_HEAD_DIM
    ts = rows // n
    seq = pl.BlockSpec((1, rows, LANE), lambda c, tt: (tt, 0, c))
    par = pl.BlockSpec((n, LANE), lambda c, tt: (0, c))
    st = pl.BlockSpec((n, n, LANE), lambda c, tt: (0, 0, c))
    return pl.pallas_call(
        functools.partial(_rwkv_scan_kernel, ts=ts),
        grid=(nc // LANE, nt),
        in_specs=[seq] * 5 + [par] * 5 + [st],
        out_specs=[seq, st],
        out_shape=[jax.ShapeDtypeStruct((nt, rows, nc), F32), jax.ShapeDtypeStruct((n, n, nc), F32)],
        scratch_shapes=[pltpu.VMEM((rows, LANE), F32)] * 3,
        compiler_params=_cparams(("parallel", "arbitrary")),
        name="rwkv_scan",
    )(r, w, k, v, a, kkw, ka, rk, lng, lnb, s0)


def _rwkv_post_kernel(y_ref, g_ref, o_ref, *, nbatch, tt, h):
    per = LANE // h
    seg = _lane_segment(h)
    pair = 2 * SUBLANE
    for c in range(nbatch // per):
        for q in range(per):
            bb = c * per + q
            for g2 in range(tt // pair):
                for col in range(RWKV_HEAD_DIM // per):
                    halves = []
                    for g in (2 * g2, 2 * g2 + 1):
                        acc = None
                        for ii in range(per):
                            i = col * per + ii
                            tile = y_ref[g, i * SUBLANE:(i + 1) * SUBLANE, c * LANE:(c + 1) * LANE]
                            sh = ((ii - q) % per) * h
                            t = tile if sh == 0 else pltpu.roll(tile, sh, axis=1)
                            acc = t if acc is None else jnp.where(seg == ii, t, acc)
                        halves.append(acc)
                    rows = slice(g2 * pair, (g2 + 1) * pair)
                    cols = slice(col * LANE, (col + 1) * LANE)
                    o_ref[bb, rows, cols] = (jnp.concatenate(halves, axis=0) * g_ref[bb, rows, cols]).astype(o_ref.dtype)


def _rwkv_post(y, g, b, t, h):
    rd = RWKV_HEAD_DIM * h
    tt = PREP_STEPS
    step_blk = lambda i: (0, i, 0)
    return pl.pallas_call(
        functools.partial(_rwkv_post_kernel, nbatch=b, tt=tt, h=h),
        grid=(t // tt,),
        in_specs=[pl.BlockSpec((tt // SUBLANE, RWKV_HEAD_DIM * SUBLANE, b * h), lambda i: (i, 0, 0)),
                  pl.BlockSpec((b, tt, rd), step_blk)],
        out_specs=pl.BlockSpec((b, tt, rd), step_blk),
        out_shape=jax.ShapeDtypeStruct((b, t, rd), BF16),
        compiler_params=_cparams(("parallel",)),
        name="rwkv_post",
    )(y, g.reshape(b, t, rd)).reshape(b * t, rd)


def _rope(x, cos2, sin2):
    return x * cos2 + pltpu.roll(x, HEAD_DIM // 2, axis=1) * sin2


def _attn_prep_kernel(a_ref, cos_ref, sin_ref, lng_ref, lnb_ref,
                      q_ref, k_ref, v_ref, qi_ref, ki_ref, wi_ref, *, nh, kvh, ih, wscale):
    cos2 = cos_ref[...]
    sin2 = sin_ref[...]
    o = 0
    for h in range(nh):
        q_ref[:, h * HEAD_DIM:(h + 1) * HEAD_DIM] = _rope(a_ref[:, o:o + HEAD_DIM], cos2, sin2).astype(BF16)
        o += HEAD_DIM
    for h in range(kvh):
        k_ref[:, h * HEAD_DIM:(h + 1) * HEAD_DIM] = _rope(a_ref[:, o:o + HEAD_DIM], cos2, sin2)
        o += HEAD_DIM
    v_ref[...] = a_ref[:, o:o + kvh * HEAD_DIM]
    o += kvh * HEAD_DIM
    for h in range(ih):
        qi_ref[:, h * INDEX_DIM:(h + 1) * INDEX_DIM] = _rope(a_ref[:, o:o + INDEX_DIM], cos2, sin2).astype(BF16)
        o += INDEX_DIM
    x = a_ref[:, o:o + INDEX_DIM]
    o += INDEX_DIM
    mu = jnp.mean(x, axis=-1, keepdims=True)
    xc = x - mu
    var = jnp.mean(xc * xc, axis=-1, keepdims=True)
    kin = xc * lax.rsqrt(var + IDX_LN_EPS) * lng_ref[...] + lnb_ref[...]
    ki_ref[...] = _rope(kin, cos2, sin2)
    wi_ref[...] = a_ref[:, o:o + LANE] * wscale


def _attn_prep(a, cos2, sin2, lng, lnb, nh, kvh, ih):
    m, na = a.shape
    tm = _tile(m, 128, SUBLANE)
    row = lambda i: (i, 0)
    fix = lambda i: (0, 0)
    ad, kvd, idd = nh * HEAD_DIM, kvh * HEAD_DIM, ih * INDEX_DIM
    wscale = float(ih) ** -0.5 * float(INDEX_DIM) ** -0.5
    return pl.pallas_call(
        functools.partial(_attn_prep_kernel, nh=nh, kvh=kvh, ih=ih, wscale=wscale),
        grid=(m // tm,),
        in_specs=[pl.BlockSpec((tm, na), row), pl.BlockSpec((tm, LANE), row), pl.BlockSpec((tm, LANE), row),
                  pl.BlockSpec((1, LANE), fix), pl.BlockSpec((1, LANE), fix)],
        out_specs=[pl.BlockSpec((tm, ad), row), pl.BlockSpec((tm, kvd), row), pl.BlockSpec((tm, kvd), row),
                   pl.BlockSpec((tm, idd), row),
                   pl.BlockSpec((tm, INDEX_DIM), row), pl.BlockSpec((tm, LANE), row)],
        out_shape=[jax.ShapeDtypeStruct((m, ad), BF16), jax.ShapeDtypeStruct((m, kvd), F32),
                   jax.ShapeDtypeStruct((m, kvd), F32), jax.ShapeDtypeStruct((m, idd), BF16),
                   jax.ShapeDtypeStruct((m, INDEX_DIM), F32),
                   jax.ShapeDtypeStruct((m, LANE), F32)],
        compiler_params=_cparams(("parallel",)),
        name="attn_prep",
    )(a, cos2, sin2, lng, lnb)


def _count_rows(ones):
    l = ones.shape[0]
    if l % COUNT_ROWS == 0 and l > COUNT_ROWS:
        ones = jnp.sum(ones.reshape(l // COUNT_ROWS, COUNT_ROWS, LANE), axis=0)
    return jnp.sum(ones, axis=0, keepdims=True)


def _topk_mask(score, key_ref, n_sel):
    l = score.shape[0]
    bits = lax.bitcast_convert_type(score, jnp.int32)
    key_ref[...] = jnp.where(bits < 0, bits ^ jnp.int32(0x7FFFFFFF), bits)
    sign = jnp.int32(-2 ** 31)
    nf = jnp.float32(n_sel)

    def body(i, ans):
        cand = ans | lax.shift_left(jnp.int32(1), 31 - i)
        cnt = _count_rows(jnp.where(key_ref[...] >= (cand ^ sign), 1.0, 0.0))
        return jnp.where(cnt >= nf, cand, ans)

    ans = lax.fori_loop(0, 32, body, jnp.zeros((1, LANE), jnp.int32))
    thr = ans ^ sign
    need = nf - _count_rows(jnp.where(key_ref[...] > thr, 1.0, 0.0))
    ch = PREFIX_CHUNK
    tri = jnp.where(lax.broadcasted_iota(jnp.int32, (ch, ch), 0) > lax.broadcasted_iota(jnp.int32, (ch, ch), 1),
                    1.0, 0.0).astype(BF16)
    run = jnp.zeros((1, LANE), F32)
    ranks = []
    for c in range(l // ch):
        e = jnp.where(key_ref[c * ch:(c + 1) * ch, :] == thr, 1.0, 0.0)
        ranks.append(jnp.dot(tri, e.astype(BF16), preferred_element_type=F32) + run)
        run = run + jnp.sum(e, axis=0, keepdims=True)
    rank = jnp.concatenate(ranks, axis=0)
    key = key_ref[...]
    return jnp.logical_or(key > thr, jnp.logical_and(key == thr, rank < need))


def _dsa_prompt_kernel(qi_ref, wit_ref, ki_ref, q_ref, k_ref, v_ref, o_ref, sc_ref, key_ref,
                       *, n_sel, ih, nh, kvh, qb0):
    l = ki_ref.shape[1]
    qb = qb0 + pl.program_id(1)
    lhs = ki_ref[0].astype(BF16)
    for hp in range(ih // 2):
        c0 = 2 * hp * INDEX_DIM
        c1 = c0 + INDEX_DIM
        rhs = jnp.concatenate([qi_ref[:, c0:c1], qi_ref[:, c1:c1 + INDEX_DIM]], axis=0)
        rel = jnp.maximum(lax.dot_general(lhs, rhs, NT_DIMS, preferred_element_type=F32), 0.0)
        w0 = wit_ref[0, 0, 2 * hp:2 * hp + 1, :]
        w1 = wit_ref[0, 0, 2 * hp + 1:2 * hp + 2, :]
        part = rel[:, :Q_BLOCK] * w0 + rel[:, Q_BLOCK:] * w1
        if hp == 0:
            sc_ref[...] = part
        else:
            sc_ref[...] += part
    s_idx = lax.broadcasted_iota(jnp.int32, (l, Q_BLOCK), 0)
    q_pos = qb * Q_BLOCK + lax.broadcasted_iota(jnp.int32, (l, Q_BLOCK), 1)
    causal = s_idx <= q_pos
    sel = _topk_mask(jnp.where(causal, sc_ref[...], -jnp.inf), key_ref, n_sel)
    bias = jnp.where(jnp.logical_and(sel, causal), 0.0, -jnp.inf).T
    c_exp = float(HEAD_DIM) ** -0.5 * 1.4426950408889634
    rep = nh // kvh
    for g in range(kvh):
        kg = k_ref[0, :, g * HEAD_DIM:(g + 1) * HEAD_DIM].astype(BF16)
        vg = v_ref[0, :, g * HEAD_DIM:(g + 1) * HEAD_DIM].astype(BF16)
        h0 = g * rep
        qg = jnp.concatenate([q_ref[:, (h0 + r) * HEAD_DIM:(h0 + r + 1) * HEAD_DIM] for r in range(rep)], axis=0)
        lg = lax.dot_general(qg, kg, NT_DIMS, preferred_element_type=F32)
        ps, dens = [], []
        for r in range(rep):
            x = lg[r * Q_BLOCK:(r + 1) * Q_BLOCK] + bias
            p = jnp.exp2((x - jnp.max(x, axis=-1, keepdims=True)) * c_exp)
            dens.append(jnp.sum(p, axis=-1, keepdims=True))
            ps.append(p.astype(BF16))
        og = jnp.dot(jnp.concatenate(ps, axis=0), vg, preferred_element_type=F32)
        for r in range(rep):
            o = og[r * Q_BLOCK:(r + 1) * Q_BLOCK] / dens[r]
            o_ref[0, :, (h0 + r) * HEAD_DIM:(h0 + r + 1) * HEAD_DIM] = o.astype(o_ref.dtype)


def _causal_buckets(nq):
    for n in range(MAX_CAUSAL_BUCKETS, 0, -1):
        if nq % n == 0 and (nq // n * Q_BLOCK) % PREFIX_CHUNK == 0:
            return n
    return 1


def _dsa_prompt(qi, wit, ki, q, k, v, b, s, n_sel, ih, nh, kvh):
    nq = s // Q_BLOCK
    nbk = _causal_buckets(nq)
    per = nq // nbk
    per_b = lambda bb, qq: (bb, 0, 0)
    outs = []
    for u in range(nbk):
        le = (u + 1) * per * Q_BLOCK
        row = functools.partial(lambda bb, qq, u: (bb * nq + u * per + qq, 0), u=u)
        wrow = functools.partial(lambda bb, qq, u: (bb, u * per + qq, 0, 0), u=u)
        outs.append(pl.pallas_call(
            functools.partial(_dsa_prompt_kernel, n_sel=n_sel, ih=ih, nh=nh, kvh=kvh, qb0=u * per),
            grid=(b, per),
            in_specs=[pl.BlockSpec((Q_BLOCK, ih * INDEX_DIM), row),
                      pl.BlockSpec((1, 1, ih, Q_BLOCK), wrow),
                      pl.BlockSpec((1, le, INDEX_DIM), per_b),
                      pl.BlockSpec((Q_BLOCK, nh * HEAD_DIM), row),
                      pl.BlockSpec((1, le, kvh * HEAD_DIM), per_b), pl.BlockSpec((1, le, kvh * HEAD_DIM), per_b)],
            out_specs=pl.BlockSpec((1, Q_BLOCK, nh * HEAD_DIM), lambda bb, qq: (bb, qq, 0)),
            out_shape=jax.ShapeDtypeStruct((b, per * Q_BLOCK, nh * HEAD_DIM), BF16),
            scratch_shapes=[pltpu.VMEM((le, Q_BLOCK), F32), pltpu.VMEM((le, Q_BLOCK), jnp.int32)],
            compiler_params=_cparams(("parallel", "arbitrary")),
            name="dsa_prompt",
        )(qi, wit, ki, q, k, v))
    return jnp.concatenate(outs, axis=1).reshape(b * s, nh * HEAD_DIM)


def _dsa_score_kernel(pt_ref, qi_ref, wi_ref, kin_ref, *rest, n_pages):
    pages = rest[:n_pages]
    past_ref, new_ref = rest[n_pages:]
    qi = qi_ref[0]
    wcol = wi_ref[0]
    for p in range(n_pages):
        rel = lax.dot_general(qi, pages[p][0].astype(BF16), NT_DIMS, preferred_element_type=F32)
        past_ref[0, p] = jnp.sum(jnp.maximum(rel, 0.0) * wcol, axis=0, keepdims=True)
    kin = kin_ref[0].astype(BF16).astype(F32)
    rel_new = jnp.maximum(jnp.sum(qi.astype(F32) * kin, axis=-1, keepdims=True), 0.0)
    new = jnp.sum(rel_new * wcol, axis=0, keepdims=True)
    new_ref[0] = jnp.broadcast_to(new, (1, LANE))


def _dsa_scores(page_table, qi, wi, ki_new, pool_idx):
    db, n_pages = page_table.shape
    ih = qi.shape[1]
    psz = pool_idx.shape[1]
    per_b = lambda bb, pt: (bb, 0, 0)
    page_specs = [pl.BlockSpec((1, psz, INDEX_DIM), functools.partial(lambda bb, pt, p: (pt[bb, p], 0, 0), p=p))
                  for p in range(n_pages)]
    grid_spec = pltpu.PrefetchScalarGridSpec(
        num_scalar_prefetch=1,
        grid=(db,),
        in_specs=[pl.BlockSpec((1, ih, INDEX_DIM), per_b),
                  pl.BlockSpec((1, ih, 1), per_b), pl.BlockSpec((1, 1, INDEX_DIM), per_b)] + page_specs,
        out_specs=[pl.BlockSpec((1, n_pages, 1, psz), lambda bb, pt: (bb, 0, 0, 0)),
                   pl.BlockSpec((1, 1, LANE), per_b)],
    )
    return pl.pallas_call(
        functools.partial(_dsa_score_kernel, n_pages=n_pages),
        grid_spec=grid_spec,
        out_shape=[jax.ShapeDtypeStruct((db, n_pages, 1, psz), F32), jax.ShapeDtypeStruct((db, 1, LANE), F32)],
        compiler_params=_cparams(("arbitrary",)),
        name="dsa_sample_scores",
    )(page_table, qi, wi, ki_new, *([pool_idx] * n_pages))


def _mask_kernel(sc_ref, m_ref, key_ref, *, n_sel):
    m_ref[...] = jnp.where(_topk_mask(sc_ref[...], key_ref, n_sel), 1.0, 0.0)


def _select_mask(score_t, n_sel):
    l, n = score_t.shape
    blk = pl.BlockSpec((l, LANE), lambda i: (0, i))
    return pl.pallas_call(
        functools.partial(_mask_kernel, n_sel=n_sel),
        grid=(n // LANE,),
        in_specs=[blk],
        out_specs=blk,
        out_shape=jax.ShapeDtypeStruct((l, n), F32),
        scratch_shapes=[pltpu.VMEM((l, LANE), jnp.int32)],
        compiler_params=_cparams(("parallel",)),
        name="select_mask",
    )(score_t)


def _dsa_sample_kernel(pt_ref, q_ref, bias_ref, kn_ref, vn_ref, *rest, n_pages, nh, kvh):
    kpages = rest[:n_pages]
    vpages = rest[n_pages:2 * n_pages]
    o_ref = rest[2 * n_pages]
    rows = kpages[0].shape[0]
    rep = nh // kvh
    scale = float(HEAD_DIM) ** -0.5
    q = q_ref[0]

    def own(width):
        col = lax.broadcasted_iota(jnp.int32, (nh, width), 1)
        head = lax.broadcasted_iota(jnp.int32, (nh, width), 0)
        return col % kvh == head // rep

    own_page = own(rows)
    lgs = []
    for p in range(n_pages):
        lg = lax.dot_general(q, kpages[p][...].astype(BF16), NT_DIMS, preferred_element_type=F32)
        lgs.append(jnp.where(own_page, lg * scale + bias_ref[0, :, p * rows:(p + 1) * rows], -jnp.inf))
    lg_new = lax.dot_general(q, kn_ref[0].astype(BF16), NT_DIMS, preferred_element_type=F32)
    lg_new = jnp.where(own(LANE), lg_new * scale + bias_ref[0, :, n_pages * rows:n_pages * rows + LANE], -jnp.inf)
    m = jnp.max(lg_new, axis=-1, keepdims=True)
    for lg in lgs:
        m = jnp.maximum(m, jnp.max(lg, axis=-1, keepdims=True))
    p_new = jnp.exp(lg_new - m)
    den = jnp.sum(p_new, axis=-1, keepdims=True)
    acc = jnp.dot(p_new.astype(BF16), vn_ref[0].astype(BF16), preferred_element_type=F32)
    for p in range(n_pages):
        pp = jnp.exp(lgs[p] - m)
        den = den + jnp.sum(pp, axis=-1, keepdims=True)
        acc = acc + jnp.dot(pp.astype(BF16), vpages[p][...].astype(BF16), preferred_element_type=F32)
    o_ref[0] = (acc / den).astype(o_ref.dtype)


def _dsa_sample(page_table, q, bias, k_new, v_new, pool_k, pool_v, nh, kvh, psz):
    db, n_pages = page_table.shape
    rows = psz * kvh
    lb = bias.shape[2]
    per_b = lambda bb, pt: (bb, 0, 0)
    page_specs = [pl.BlockSpec((rows, HEAD_DIM), functools.partial(lambda bb, pt, p: (pt[bb, p], 0), p=p))
                  for p in range(n_pages)]
    grid_spec = pltpu.PrefetchScalarGridSpec(
        num_scalar_prefetch=1,
        grid=(db,),
        in_specs=[pl.BlockSpec((1, nh, HEAD_DIM), per_b), pl.BlockSpec((1, 1, lb), per_b),
                  pl.BlockSpec((1, LANE, HEAD_DIM), per_b), pl.BlockSpec((1, LANE, HEAD_DIM), per_b)]
        + page_specs + page_specs,
        out_specs=pl.BlockSpec((1, nh, HEAD_DIM), per_b),
    )
    return pl.pallas_call(
        functools.partial(_dsa_sample_kernel, n_pages=n_pages, nh=nh, kvh=kvh),
        grid_spec=grid_spec,
        out_shape=jax.ShapeDtypeStruct((db, nh, HEAD_DIM), BF16),
        compiler_params=_cparams(("arbitrary",)),
        name="dsa_sample_attend",
    )(page_table, q, bias, k_new, v_new, *([pool_k] * n_pages), *([pool_v] * n_pages))


def _cross_kernel(q_ref, mk_ref, mv_ref, o_ref, *, mh, tq):
    scale = float(HEAD_DIM) ** -0.5
    for h in range(mh):
        cs = slice(h * HEAD_DIM, (h + 1) * HEAD_DIM)
        qh = q_ref[0, :, cs].astype(BF16)
        if tq < SUBLANE:
            qh = jnp.broadcast_to(qh[0:1], (SUBLANE, HEAD_DIM))
        kh = mk_ref[0, :, cs].astype(BF16)
        vh = mv_ref[0, :, cs].astype(BF16)
        lg = lax.dot_general(qh, kh, NT_DIMS, preferred_element_type=F32) * scale
        p = jnp.exp(lg - jnp.max(lg, axis=-1, keepdims=True))
        den = jnp.sum(p, axis=-1, keepdims=True)
        o = jnp.dot(p.astype(BF16), vh, preferred_element_type=F32) / den
        o_ref[0, :, cs] = o[0:tq].astype(o_ref.dtype)


def _cross_attend(q, mk, mv, mh):
    b, t, md = q.shape
    nm = mk.shape[1]
    tq = _tile(t, 512, SUBLANE)
    return pl.pallas_call(
        functools.partial(_cross_kernel, mh=mh, tq=tq),
        grid=(b, t // tq),
        in_specs=[pl.BlockSpec((1, tq, md), lambda bb, i: (bb, i, 0)),
                  pl.BlockSpec((1, nm, md), lambda bb, i: (bb, 0, 0)),
                  pl.BlockSpec((1, nm, md), lambda bb, i: (bb, 0, 0))],
        out_specs=pl.BlockSpec((1, tq, md), lambda bb, i: (bb, i, 0)),
        out_shape=jax.ShapeDtypeStruct((b, t, md), BF16),
        compiler_params=_cparams(("parallel", "parallel")),
        name="cross_attend",
    )(q, mk, mv)


def _cross_rows_kernel(q_ref, mk_ref, mv_ref, o_ref, *, mh):
    scale = float(HEAD_DIM) ** -0.5
    nseq = q_ref.shape[0]
    rows = mk_ref.shape[0] // nseq
    col = lax.broadcasted_iota(jnp.int32, (SUBLANE, rows), 1)
    head = lax.broadcasted_iota(jnp.int32, (SUBLANE, rows), 0)
    own = col % mh == head % mh
    for s in range(nseq):
        rs = slice(s * rows, (s + 1) * rows)
        q = q_ref[s].astype(BF16)
        lg = lax.dot_general(q, mk_ref[rs, :].astype(BF16), NT_DIMS, preferred_element_type=F32) * scale
        lg = jnp.where(own, lg, -jnp.inf)
        p = jnp.exp(lg - jnp.max(lg, axis=-1, keepdims=True))
        den = jnp.sum(p, axis=-1, keepdims=True)
        o = jnp.dot(p.astype(BF16), mv_ref[rs, :].astype(BF16), preferred_element_type=F32) / den
        o_ref[s] = o.astype(o_ref.dtype)


def _cross_attend_rows(q, mk, mv, mh, n_mem):
    b = q.shape[0]
    assert mh <= SUBLANE
    q8 = jnp.pad(q.reshape(b, mh, HEAD_DIM), ((0, 0), (0, SUBLANE - mh), (0, 0)))
    rows = n_mem * mh
    g = _tile(b, CROSS_SEQS_PER_STEP, 1)
    o = pl.pallas_call(
        functools.partial(_cross_rows_kernel, mh=mh),
        grid=(b // g,),
        in_specs=[pl.BlockSpec((g, SUBLANE, HEAD_DIM), lambda bb: (bb, 0, 0)),
                  pl.BlockSpec((g * rows, HEAD_DIM), lambda bb: (bb, 0)),
                  pl.BlockSpec((g * rows, HEAD_DIM), lambda bb: (bb, 0))],
        out_specs=pl.BlockSpec((g, SUBLANE, HEAD_DIM), lambda bb: (bb, 0, 0)),
        out_shape=jax.ShapeDtypeStruct((b, SUBLANE, HEAD_DIM), BF16),
        compiler_params=_cparams(("parallel",)),
        name="cross_attend_rows",
    )(q8, mk, mv)
    return o[:, :mh].reshape(b, mh * HEAD_DIM)


def _head_major_to_channel_major(x, h):
    lead = x.shape[:-1]
    return x.reshape(lead + (h, RWKV_HEAD_DIM)).swapaxes(-1, -2).reshape(x.shape)


def _channel_major_to_head_major(x, h):
    lead = x.shape[:-1]
    return x.reshape(lead + (RWKV_HEAD_DIM, h)).swapaxes(-1, -2).reshape(x.shape)


def _rwkv_cols(x, h, rd, lwd, lad, fn):
    o_k = rd + lwd
    o_a = o_k + 2 * rd
    return jnp.concatenate([fn(x[..., :rd], h), x[..., rd:o_k], fn(x[..., o_k:o_k + rd], h),
                            fn(x[..., o_k + rd:o_a], h), x[..., o_a:]], axis=-1)


def _to_chains(x, b, t, h, ts):
    nc = b * h
    y = x.reshape(b, t // ts, ts, RWKV_HEAD_DIM, h).transpose(1, 3, 2, 0, 4)
    y = y.reshape(t // ts, RWKV_HEAD_DIM * ts, nc)
    return jnp.pad(y, ((0, 0), (0, 0), (0, _round_up(nc, LANE) - nc)))


def _from_chains(y, b, t, h, ts):
    nc = b * h
    y = y[:, :, :nc].reshape(t // ts, RWKV_HEAD_DIM, ts, b, h).transpose(3, 0, 2, 1, 4)
    return y.reshape(b * t, RWKV_HEAD_DIM * h)


def _param_chains(p, b, h):
    nc = b * h
    y = jnp.tile(p.reshape(h, RWKV_HEAD_DIM).T, (1, b))
    return jnp.pad(y, ((0, 0), (0, _round_up(nc, LANE) - nc)))


def _rope_tables(pos):
    half = HEAD_DIM // 2
    inv = jnp.power(ROPE_THETA, -jnp.arange(half, dtype=F32) / half)
    ang = pos.astype(F32)[:, None] * inv[None, :]
    cos, sin = jnp.cos(ang), jnp.sin(ang)
    return jnp.concatenate([cos, cos], axis=-1), jnp.concatenate([-sin, sin], axis=-1)


def _mixer_front(x, b, t, pos, lw, dims):
    d, rd, lwd, lad, lgp, rp, rpp, nh, kvh, ih = dims
    m = b * t
    hn = _rmsnorm(x, lw["norm_mix"], BF16)
    gates = _mm(hn, lw["w_in"], n=2 * d, tn_pref=512)
    p = _mm(hn, lw["w_rwkv"])
    att = _mm(hn, lw["w_attn"])
    cos2, sin2 = _rope_tables(pos)
    cos2 = jnp.broadcast_to(cos2[None], (b, t, LANE)).reshape(m, LANE)
    sin2 = jnp.broadcast_to(sin2[None], (b, t, LANE)).reshape(m, LANE)
    q, kr, vv, qi, ki, wi = _attn_prep(att, cos2, sin2, lw["idx_ln_g"], lw["idx_ln_b"], nh, kvh, ih)
    return {"gates": gates, "p": p, "q": q, "k": kr, "v": vv, "qi": qi, "ki": ki, "wi": wi}


def _rwkv_branch(p, b, t, shift_prev, s0, lw, dims):
    d, rd, lwd, lad, lgp, rp, rpp, nh, kvh, ih = dims
    h = rd // RWKV_HEAD_DIM
    p3 = p.reshape(b, t, rpp)
    prev = jnp.pad(_rwkv_cols(shift_prev, h, rd, lwd, lad, _head_major_to_channel_major), ((0, 0), (0, rpp - rp)))
    r, w, k, v, a, g = _rwkv_prep(p, prev, b, t, lw["mu"], lw["w0"], lw["w2"], lw["a0"], lw["a2"], lw["g2"],
                                  rd, lwd, lad, lgp)
    nc = b * h
    in_kernel = _chain_kernels_fit(b, t, h)
    ts = SUBLANE if t % SUBLANE == 0 else 1
    if not in_kernel:
        r, w, k, v, a = [_to_chains(z, b, t, h, ts) for z in (r, w, k, v, a)]
    s0c = jnp.pad(s0.reshape(nc, RWKV_HEAD_DIM, RWKV_HEAD_DIM).transpose(2, 1, 0),
                  ((0, 0), (0, 0), (0, _round_up(nc, LANE) - nc)))
    yc, sc = _rwkv_scan(r, w, k, v, a,
                        _param_chains(lw["kk"], b, h), _param_chains(lw["ka"], b, h),
                        _param_chains(lw["rk"], b, h), _param_chains(lw["ln_g"], b, h),
                        _param_chains(lw["ln_b"], b, h), s0c)
    if in_kernel:
        yg = _rwkv_post(yc, g, b, t, h)
    else:
        yg = _gate_mul(_from_chains(yc, b, t, h, ts), g)
    s_fin = sc[:, :, :nc].transpose(2, 1, 0).reshape(b, h, RWKV_HEAD_DIM, RWKV_HEAD_DIM)
    shift_new = _rwkv_cols(p3[:, -1, :rp], h, rd, lwd, lad, _channel_major_to_head_major)
    return yg, s_fin, shift_new


def _mixer_back(x, f, yg, ya, lw):
    merged = _merge(yg, ya, f["gates"], lw["w_br_rwkv"], lw["w_br_attn"])
    return _mm_resid(merged, lw["w_out"], x)


def _attend_prompt(f, b, s, nh, kvh, ih):
    n_sel = min(TOPK_MAX, s // 4)
    nq = s // Q_BLOCK
    wit = f["wi"][:, :ih].reshape(b, nq, Q_BLOCK, ih).transpose(0, 1, 3, 2)
    return _dsa_prompt(f["qi"], wit, f["ki"].reshape(b, s, INDEX_DIM), f["q"],
                       f["k"].reshape(b, s, kvh * HEAD_DIM), f["v"].reshape(b, s, kvh * HEAD_DIM),
                       b, s, n_sel, ih, nh, kvh)


def _attend_sample(f, pool_k, pool_v, pool_idx, page_table, nh, kvh, ih):
    q, qi, wi, kr, vv, ki = f["q"], f["qi"], f["wi"], f["k"], f["v"], f["ki"]
    db, n_pages = page_table.shape
    psz = pool_idx.shape[1]
    past = n_pages * psz
    n_sel = min(TOPK_MAX, (past + 1) // 4)
    kvd = kvh * HEAD_DIM
    sc_past, sc_new = _dsa_scores(page_table, qi.reshape(db, ih, INDEX_DIM),
                                  wi[:, :ih].reshape(db, ih, 1), ki.reshape(db, 1, INDEX_DIM), pool_idx)
    lp = _round_up(past + 1, PREFIX_CHUNK)
    dbp = _round_up(db, LANE)
    assert (lp - past) * kvh >= LANE
    score = jnp.concatenate([sc_past.reshape(db, past), sc_new[:, 0, :1]], axis=1)
    score_t = jnp.pad(score.T, ((0, lp - past - 1), (0, dbp - db)), constant_values=-jnp.inf)
    mask = _select_mask(score_t, n_sel)[:, :db].T
    bias = jnp.repeat(jnp.where(mask > 0.5, 0.0, -jnp.inf).astype(F32), kvh, axis=1).reshape(db, 1, lp * kvh)
    new_rows = lambda x: jnp.pad(x.reshape(db, kvh, HEAD_DIM), ((0, 0), (0, LANE - kvh), (0, 0)))
    o = _dsa_sample(page_table, q.reshape(db, nh, HEAD_DIM), bias, new_rows(kr), new_rows(vv),
                    pool_k, pool_v, nh, kvh, psz)
    return o.reshape(db, nh * HEAD_DIM)


def _cross_and_ffn(x, b, t, mk, mv, lw, mh, n_mem):
    qm = _norm_mm(x, lw["norm_cross"], lw["w_q_mem"])
    if mk.ndim == 2:
        o = _cross_attend_rows(qm, mk, mv, mh, n_mem)
    else:
        o = _cross_attend(qm.reshape(b, t, -1), mk, mv, mh).reshape(b * t, -1)
    x2 = _mm_resid(o, lw["w_o_mem"], x)
    hf = _rmsnorm(x2, lw["norm_ffn"], BF16)
    act = _swiglu(hf, lw["w_up"])
    return _mm_resid(act, lw["w_down"], x2, tn_pref=512, tk_pref=5504)


def kernel(x_prompt, mem_prompt, x_sample, cache_k, cache_v, cache_idx_k, cache_mem_k, cache_mem_v, state_rwkv, state_shift, page_table, norm_mix, w_in, rwkv_mu, rwkv_w0, rwkv_w2, rwkv_a0, rwkv_a2, rwkv_g2, rwkv_kk, rwkv_ka, rwkv_rk, rwkv_ln_g, rwkv_ln_b, idx_ln_g, idx_ln_b, w_br_rwkv, w_br_attn, w_out, norm_cross, norm_mem, w_q_mem, w_k_mem, w_v_mem, w_o_mem, norm_ffn, w_up, w_down, final_norm):
    b, s, d = x_prompt.shape
    db, ts, _ = x_sample.shape
    assert ts == 1, "the sample group decodes one token per sequence"
    depth = w_in.shape[0]
    in_proj = w_in.shape[2]
    rp = rwkv_mu.shape[1]
    lwd, rd = rwkv_w2.shape[1:]
    lad = rwkv_a2.shape[1]
    lg = rwkv_g2.shape[1]
    lgp = _round_up(lg, LANE)
    rpp = rp - lg + lgp
    ad = w_br_attn.shape[1]
    nh = ad // HEAD_DIM
    kvh = cache_k.shape[3]
    kvd = kvh * HEAD_DIM
    ih = (in_proj - 2 * d - rp - ad - 2 * kvd - INDEX_DIM) // (INDEX_DIM + 1)
    mh = cache_mem_k.shape[3]
    n_mem = mem_prompt.shape[1]
    past = page_table.shape[1] * cache_k.shape[2]
    assert rp == 3 * rd + lwd + lad + lg and s % Q_BLOCK == 0 and ih % 2 == 0
    dims = (d, rd, lwd, lad, lgp, rp, rpp, nh, kvh, ih)
    n_pool, psz = cache_k.shape[1:3]
    pool_k = cache_k.reshape(depth * n_pool * psz * kvh, HEAD_DIM)
    pool_v = cache_v.reshape(depth * n_pool * psz * kvh, HEAD_DIM)
    pool_idx = cache_idx_k.reshape(depth * n_pool, psz, INDEX_DIM)
    mem_rows = db * n_mem * mh
    mem_k = cache_mem_k.reshape(depth * mem_rows, HEAD_DIM)
    mem_v = cache_mem_v.reshape(depth * mem_rows, HEAD_DIM)
    hr = rd // RWKV_HEAD_DIM
    cm = functools.partial(_head_major_to_channel_major, h=hr)

    xp = x_prompt.reshape(b * s, d)
    xs = x_sample.reshape(db, d)
    pos_p = jnp.arange(s, dtype=jnp.int32)
    pos_s = past + jnp.arange(ts, dtype=jnp.int32)
    outs = [[] for _ in range(12)]
    for l in range(depth):
        wl = w_in[l]
        o_att = 2 * d + rp
        n_att = ad + 2 * kvd + ih * INDEX_DIM + INDEX_DIM
        lw = {
            "norm_mix": norm_mix[l],
            "w_in": wl,
            "w_rwkv": jnp.pad(_rwkv_cols(wl[:, 2 * d:o_att], hr, rd, lwd, lad, _head_major_to_channel_major),
                              ((0, 0), (0, rpp - rp))).astype(BF16),
            "w_attn": jnp.pad(wl[:, o_att:], ((0, 0), (0, LANE - ih))).astype(BF16),
            "mu": jnp.pad(_rwkv_cols(rwkv_mu[l], hr, rd, lwd, lad, _head_major_to_channel_major),
                          (0, rpp - rp)).reshape(1, rpp),
            "w0": cm(rwkv_w0[l]).reshape(1, rd), "w2": cm(rwkv_w2[l]).astype(BF16),
            "a0": cm(rwkv_a0[l]).reshape(1, rd), "a2": cm(rwkv_a2[l]).astype(BF16),
            "g2": jnp.pad(cm(rwkv_g2[l]), ((0, lgp - lg), (0, 0))).astype(BF16),
            "kk": rwkv_kk[l], "ka": rwkv_ka[l], "rk": rwkv_rk[l].reshape(-1),
            "ln_g": rwkv_ln_g[l], "ln_b": rwkv_ln_b[l],
            "idx_ln_g": idx_ln_g[l].reshape(1, INDEX_DIM), "idx_ln_b": idx_ln_b[l].reshape(1, INDEX_DIM),
            "w_br_rwkv": cm(w_br_rwkv[l].T).T.astype(BF16), "w_br_attn": w_br_attn[l],
            "w_out": w_out[l],
            "norm_cross": norm_cross[l], "norm_ffn": norm_ffn[l],
            "w_q_mem": w_q_mem[l], "w_o_mem": w_o_mem[l],
            "w_up": w_up[l], "w_down": w_down[l].astype(BF16),
        }
        assert n_att + ih == in_proj - o_att
        fp = _mixer_front(xp, b, s, pos_p, lw, dims)
        ygp, sfp, shp = _rwkv_branch(fp["p"], b, s, jnp.zeros((b, rp), F32),
                                     jnp.zeros((b, hr, RWKV_HEAD_DIM, RWKV_HEAD_DIM), F32), lw, dims)
        xp = _mixer_back(xp, fp, ygp, _attend_prompt(fp, b, s, nh, kvh, ih), lw)
        mem2d = mem_prompt.reshape(b * n_mem, d)
        mkp = _norm_mm(mem2d, norm_mem[l], w_k_mem[l]).reshape(b, n_mem, mh * HEAD_DIM)
        mvp = _norm_mm(mem2d, norm_mem[l], w_v_mem[l]).reshape(b, n_mem, mh * HEAD_DIM)
        xp = _cross_and_ffn(xp, b, s, mkp, mvp, lw, mh, n_mem)
        fs = _mixer_front(xs, db, ts, pos_s, lw, dims)
        ya_s = _attend_sample(fs, pool_k, pool_v, pool_idx, page_table + l * n_pool, nh, kvh, ih)
        ygs, sfs, shs = _rwkv_branch(fs["p"], db, ts, state_shift[l], state_rwkv[l], lw, dims)
        xs = _mixer_back(xs, fs, ygs, ya_s, lw)
        xs = _cross_and_ffn(xs, db, ts, mem_k[l * mem_rows:(l + 1) * mem_rows],
                            mem_v[l * mem_rows:(l + 1) * mem_rows], lw, mh, n_mem)
        vals = (fp["k"].reshape(b, s, kvh, HEAD_DIM), fp["v"].reshape(b, s, kvh, HEAD_DIM),
                fp["ki"].reshape(b, s, INDEX_DIM),
                mkp.reshape(b, n_mem, mh, HEAD_DIM), mvp.reshape(b, n_mem, mh, HEAD_DIM), sfp, shp,
                fs["k"].reshape(db, ts, kvh, HEAD_DIM), fs["v"].reshape(db, ts, kvh, HEAD_DIM),
                fs["ki"].reshape(db, ts, INDEX_DIM), sfs, shs)
        for lst, val in zip(outs, vals):
            lst.append(val)
    y_prompt = _rmsnorm(xp, final_norm, F32).reshape(b, s, d)
    y_sample = _rmsnorm(xs, final_norm, F32).reshape(db, ts, d)
    return (y_prompt, y_sample) + tuple(jnp.stack(o) for o in outs)
```
